```python
import jax
import jax.numpy as jnp
from jax import lax
import numpy as np

D_MODEL = 1024
BATCH = 2
SEQ = 8192
DEPTH = 2
DEC_BATCH = 128
DEC_SEQ = 1
PAST_LEN = 2048
PAGE_SIZE = 128

N_A_LAYERS = DEPTH // 2
N_B_LAYERS = DEPTH - N_A_LAYERS
HEAD_A = 64
N_HEADS_A = D_MODEL // HEAD_A
LORA_DECAY = 64
LORA_ICLR = 64
LORA_GATE = 160
N_MIX = 6
GN_EPS = 64e-5
HEAD_B = 64
N_HEADS_B = D_MODEL // HEAD_B
DILATED_GROUPS = ((128, 1), (512, 4), (2048, 16))
N_GROUPS_B = len(DILATED_GROUPS)
BAND = DILATED_GROUPS[0][0] // DILATED_GROUPS[0][1]
QB = 128
W_MAX = max(w for w, _ in DILATED_GROUPS)
SCALE_B = HEAD_B ** -0.5
NUM_BUCKETS = 32
MAX_DISTANCE = 2048
N_EXPERT_GROUPS = 4
EXPERTS_PER_GROUP = 8
TOP_K_IN_GROUP = 2
D_EXPERT = 512
RMS_EPS = 1e-6
NEG_INF = -1e30

kernel_name = 'yoco_rwkv7_dilated_hmoe_step'


def rmsnorm(x, g):
    x32 = x.astype(jnp.float32)
    y = x32 * lax.rsqrt(jnp.mean(x32 * x32, axis=-1, keepdims=True) + RMS_EPS)
    return (y * g.astype(jnp.float32)).astype(x.dtype)


def modulate(h, shift, scale):
    return h * (1 + scale[:, None]) + shift[:, None]


def t5_bucket(dist):
    dist = np.asarray(dist)
    max_exact = NUM_BUCKETS // 2
    log_ratio = np.log(np.maximum(dist, 1) / max_exact) / np.log(MAX_DISTANCE / max_exact)
    large = np.minimum(max_exact + (log_ratio * (NUM_BUCKETS - max_exact)).astype(np.int32), NUM_BUCKETS - 1)
    return np.where(dist < max_exact, dist, large).astype(np.int32)


def rwkv7_step(s, inp):
    r_t, w_t, k_t, v_t, a_t, b_t = inp
    sa = jnp.einsum('bhij,bhj->bhi', s, a_t)
    s = s * w_t[:, :, None, :] + sa[..., None] * b_t[:, :, None, :] + v_t[..., None] * k_t[:, :, None, :]
    return s, jnp.einsum('bhij,bhj->bhi', s, r_t)


def rwkv7_mix(h, shift_prev, s0, mu, w_rkv, w0, w1, w2, a0, a1, a2, g1, g2, k_k, k_a, r_k, ln_w, ln_b, w_o):
    f32 = jnp.float32
    b, t, _ = h.shape
    h_prev = jnp.concatenate([shift_prev[:, None].astype(h.dtype), h[:, :-1]], axis=1)
    xx = h_prev - h
    xs = h[None] + xx[None] * mu[:, None, None, :]
    rkv = jnp.einsum('sbtd,sde->sbte', xs[:3], w_rkv)

    def heads(z):
        return z.astype(f32).reshape(b, t, N_HEADS_A, HEAD_A)

    r, k, v = heads(rkv[0]), heads(rkv[1]), heads(rkv[2])
    w_log = -jax.nn.softplus(-(w0 + jnp.tanh(xs[3] @ w1) @ w2).astype(f32)) - 0.5
    decay = heads(jnp.exp(-jnp.exp(w_log)))
    a = heads(jax.nn.sigmoid((a0 + (xs[4] @ a1) @ a2).astype(f32)))
    g = jax.nn.sigmoid(xs[5] @ g1) @ g2
    kk = k * k_k.astype(f32).reshape(N_HEADS_A, HEAD_A)
    kk = kk / jnp.maximum(jnp.sqrt(jnp.sum(kk * kk, axis=-1, keepdims=True)), 1e-12)
    k = k * (1 + (a - 1) * k_a.astype(f32).reshape(N_HEADS_A, HEAD_A))
    seq = tuple(jnp.moveaxis(z, 1, 0) for z in (r, decay, k, v, -kk, kk * a))
    s_fin, ys = lax.scan(rwkv7_step, s0.astype(f32), seq)
    y = jnp.moveaxis(ys, 0, 1)
    mean = jnp.mean(y, axis=-1, keepdims=True)
    var = jnp.mean(jnp.square(y - mean), axis=-1, keepdims=True)
    yn = ((y - mean) * lax.rsqrt(var + GN_EPS)).reshape(b, t, D_MODEL) * ln_w.astype(f32) + ln_b.astype(f32)
    bonus = (jnp.sum(r * k * r_k.astype(f32), axis=-1, keepdims=True) * v).reshape(b, t, D_MODEL)
    out = ((yn + bonus) * g.astype(f32)).astype(h.dtype) @ w_o
    return out, s_fin, h[:, -1]


def hier_moe(h, w_rg, b_rg, w_re, b_re, w_gu, w_down):
    f32 = jnp.float32
    shp = h.shape
    hf = h.reshape(-1, D_MODEL)
    n = hf.shape[0]
    pg = jax.nn.softmax((hf @ w_rg + b_rg).astype(f32), axis=-1)
    p_top, g_sel = lax.top_k(pg, 1)
    le = (hf @ w_re + b_re).astype(f32).reshape(n, N_EXPERT_GROUPS, EXPERTS_PER_GROUP)
    le_g = jnp.take_along_axis(le, g_sel[:, :, None], axis=1)[:, 0]
    v_top, i_top = lax.top_k(le_g, TOP_K_IN_GROUP)
    w_top = jax.nn.softmax(v_top, axis=-1) * p_top
    w_in = jnp.sum(jax.nn.one_hot(i_top, EXPERTS_PER_GROUP, dtype=f32) * w_top[..., None], axis=1)
    w_all = jax.nn.one_hot(g_sel[:, 0], N_EXPERT_GROUPS, dtype=f32)[:, :, None] * w_in[:, None, :]
    y = jnp.zeros((n, D_MODEL), f32)
    for gi in range(N_EXPERT_GROUPS):
        gate, up = jnp.split(jnp.einsum('nd,edf->nef', hf, w_gu[gi]), 2, axis=-1)
        act = jax.nn.silu(gate) * up * w_all[:, gi, :, None].astype(h.dtype)
        y = y + jnp.einsum('nef,efd->nd', act, w_down[gi]).astype(f32)
    return y.astype(h.dtype).reshape(shp)


def shared_kv(x, c, g_kv, w_ada_kv, b_ada_kv, w_kv):
    b, t, _ = x.shape
    sh, sc = jnp.split(jax.nn.silu(c) @ w_ada_kv + b_ada_kv, 2, axis=-1)
    hk = modulate(rmsnorm(x, g_kv), sh, sc)
    kv = (hk @ w_kv).reshape(b, t, 2, N_HEADS_B, HEAD_B)
    return kv[:, :, 0], kv[:, :, 1]


def band_attn_prompt(q, k, v, table, dil):
    f32 = jnp.float32
    b, t = q.shape[:2]
    L = t // dil
    nblk = -(-L // QB)
    lp = nblk * QB

    def to_blocks(z):
        z = z.reshape(b, L, dil, N_HEADS_B, HEAD_B).transpose(0, 2, 1, 3, 4).reshape(b * dil, L, N_HEADS_B, HEAD_B)
        z = jnp.pad(z, ((0, 0), (0, lp - L), (0, 0), (0, 0)))
        return z.reshape(b * dil, nblk, QB, N_HEADS_B, HEAD_B)

    def band(z):
        prev = jnp.concatenate([jnp.zeros_like(z[:, :1]), z[:, :-1]], axis=1)
        return jnp.concatenate([prev, z], axis=2)

    qb = to_blocks(q)
    kband, vband = band(to_blocks(k)), band(to_blocks(v))
    delta = np.arange(QB)[:, None] + QB - np.arange(2 * QB)[None, :]
    in_band = (delta >= 0) & (delta <= BAND)
    valid = in_band[None] & ((np.arange(nblk) > 0)[:, None, None] | (np.arange(2 * QB) >= QB)[None, None, :])
    bias = jnp.transpose(table[t5_bucket(np.clip(delta, 0, BAND) * dil)], (2, 0, 1)).astype(f32)
    logits = jnp.einsum('bnqhd,bnkhd->bnhqk', qb, kband, preferred_element_type=f32) * SCALE_B + bias[None, None]
    logits = jnp.where(valid[None, :, None], logits, NEG_INF)
    m = jnp.max(logits, axis=-1, keepdims=True)
    p = jnp.exp(logits - m)
    s = jnp.sum(p, axis=-1)
    o = jnp.einsum('bnhqk,bnkhd->bnqhd', p, vband.astype(f32)) / jnp.transpose(s, (0, 1, 3, 2))[..., None]
    lse = jnp.transpose(m[..., 0] + jnp.log(s), (0, 1, 3, 2))
    o = o.reshape(b * dil, lp, N_HEADS_B, HEAD_B)[:, :L].reshape(b, dil, L, N_HEADS_B, HEAD_B)
    o = o.transpose(0, 2, 1, 3, 4).reshape(b, t, N_HEADS_B, HEAD_B)
    lse = lse.reshape(b * dil, lp, N_HEADS_B)[:, :L].reshape(b, dil, L, N_HEADS_B).transpose(0, 2, 1, 3).reshape(b, t, N_HEADS_B)
    return o, lse


def window_attn_sample(q, k_all, v_all, table, dil, n_buf):
    f32 = jnp.float32
    s_len = q.shape[1]
    idx = n_buf + np.arange(s_len)[:, None] - dil * np.arange(BAND + 1)[None, :]
    valid = idx >= 0
    idxc = np.maximum(idx, 0)
    kg, vg = k_all[:, idxc], v_all[:, idxc]
    bias = table[t5_bucket(dil * np.arange(BAND + 1))].astype(f32)
    logits = jnp.einsum('bshd,bsjhd->bshj', q, kg, preferred_element_type=f32) * SCALE_B + bias.T[None, None]
    logits = jnp.where(valid[None, :, None, :], logits, NEG_INF)
    m = jnp.max(logits, axis=-1, keepdims=True)
    p = jnp.exp(logits - m)
    s = jnp.sum(p, axis=-1)
    o = jnp.einsum('bshj,bsjhd->bshd', p, vg.astype(f32)) / s[..., None]
    return o, m[..., 0] + jnp.log(s)


def dilated_attention(h, k_ctx, v_ctx, n_buf, w_q, w_o, rel_bias):
    b, t, _ = h.shape
    q = (h @ w_q).reshape(b, t, N_GROUPS_B, N_HEADS_B, HEAD_B)
    outs, lses = [], []
    for gi, (_, dil) in enumerate(DILATED_GROUPS):
        table = rel_bias[:, gi * N_HEADS_B:(gi + 1) * N_HEADS_B]
        if n_buf is None:
            o, lse = band_attn_prompt(q[:, :, gi], k_ctx, v_ctx, table, dil)
        else:
            o, lse = window_attn_sample(q[:, :, gi], k_ctx, v_ctx, table, dil, n_buf)
        outs.append(o)
        lses.append(lse)
    wts = jax.nn.softmax(jnp.stack(lses), axis=0)
    o = jnp.sum(wts[..., None] * jnp.stack(outs), axis=0)
    return o.reshape(b, t, N_HEADS_B * HEAD_B).astype(h.dtype) @ w_o


def run_trunk(x, c, wkv0, shift0, k_buf, v_buf, p):
    new_wkv, new_shift = [], []
    k_sh = v_sh = k_ctx = v_ctx = n_buf = None
    for l in range(DEPTH):
        mods = jax.nn.silu(c) @ p['w_ada'][l] + p['b_ada'][l]
        sh_m, sc_m, gt_m, sh_c, sc_c, gt_c = jnp.split(mods, 6, axis=-1)
        h = modulate(rmsnorm(x, p['g_norm'][l, 0]), sh_m, sc_m)
        if l < N_A_LAYERS:
            mix, s_fin, last = rwkv7_mix(h, shift0[l], wkv0[l], p['rw_mu'][l], p['rw_w_rkv'][l], p['rw_w0'][l],
                                         p['rw_w1'][l], p['rw_w2'][l], p['rw_a0'][l], p['rw_a1'][l], p['rw_a2'][l],
                                         p['rw_g1'][l], p['rw_g2'][l], p['rw_k_k'][l], p['rw_k_a'][l], p['rw_r_k'][l],
                                         p['rw_ln_w'][l], p['rw_ln_b'][l], p['rw_w_o'][l])
            new_wkv.append(s_fin)
            new_shift.append(last)
        else:
            lb = l - N_A_LAYERS
            mix = dilated_attention(h, k_ctx, v_ctx, n_buf, p['at_w_q'][lb], p['at_w_o'][lb], p['rel_bias'])
        x = x + gt_m[:, None] * mix
        h = modulate(rmsnorm(x, p['g_norm'][l, 1]), sh_c, sc_c)
        x = x + gt_c[:, None] * hier_moe(h, p['moe_w_rg'][l], p['moe_b_rg'][l], p['moe_w_re'][l], p['moe_b_re'][l],
                                         p['moe_w_gu'][l], p['moe_w_down'][l])
        if l == N_A_LAYERS - 1:
            k_sh, v_sh = shared_kv(x, c, p['g_kv'], p['w_ada_kv'], p['b_ada_kv'], p['w_kv'])
            if k_buf is None:
                k_ctx, v_ctx = k_sh, v_sh
            else:
                n_buf = k_buf.shape[1]
                k_ctx = jnp.concatenate([k_buf.astype(k_sh.dtype), k_sh], axis=1)
                v_ctx = jnp.concatenate([v_buf.astype(v_sh.dtype), v_sh], axis=1)
    y = rmsnorm(x, p['g_final'])
    return y, jnp.stack(new_wkv), jnp.stack(new_shift), k_sh, v_sh


def setup_inputs(seed: int = 0) -> dict:
    key = jax.random.key(seed)
    keys = iter(list(jax.random.split(key, 64)))
    f32 = jnp.float32
    D = D_MODEL
    na, nb = N_A_LAYERS, N_B_LAYERS
    G, E, F = N_EXPERT_GROUPS, EXPERTS_PER_GROUP, D_EXPERT
    kv_rows = min(W_MAX, PAST_LEN)

    def nrm(shape, scale):
        return scale * jax.random.normal(next(keys), shape, f32)

    def uni(shape):
        return jax.random.uniform(next(keys), shape, f32)

    return {
        'x_prompt': nrm((BATCH, SEQ, D), 1.0),
        'x_sample': nrm((DEC_BATCH, DEC_SEQ, D), 1.0),
        'state_wkv': nrm((na, DEC_BATCH, N_HEADS_A, HEAD_A, HEAD_A), 0.5),
        'state_shift': nrm((na, DEC_BATCH, D), 1.0),
        'cache_k': nrm((DEC_BATCH, kv_rows, N_HEADS_B, HEAD_B), 1.0),
        'cache_v': nrm((DEC_BATCH, kv_rows, N_HEADS_B, HEAD_B), 1.0),
        'c_prompt': nrm((BATCH, D), 1.0),
        'c_sample': nrm((DEC_BATCH, D), 1.0),
        'w_ada': nrm((DEPTH, D, 6 * D), 0.5 * D ** -0.5),
        'b_ada': nrm((DEPTH, 6 * D), 0.01),
        'g_norm': 1.0 + nrm((DEPTH, 2, D), 0.1),
        'rw_mu': uni((na, N_MIX, D)),
        'rw_w_rkv': nrm((na, 3, D, D), D ** -0.5),
        'rw_w0': nrm((na, D), 0.5),
        'rw_w1': nrm((na, D, LORA_DECAY), D ** -0.5),
        'rw_w2': nrm((na, LORA_DECAY, D), 0.3 * LORA_DECAY ** -0.5),
        'rw_a0': nrm((na, D), 0.1),
        'rw_a1': nrm((na, D, LORA_ICLR), D ** -0.5),
        'rw_a2': nrm((na, LORA_ICLR, D), 0.3 * LORA_ICLR ** -0.5),
        'rw_g1': nrm((na, D, LORA_GATE), D ** -0.5),
        'rw_g2': nrm((na, LORA_GATE, D), LORA_GATE ** -0.5),
        'rw_k_k': 0.85 + nrm((na, D), 0.1),
        'rw_k_a': 1.0 + nrm((na, D), 0.1),
        'rw_r_k': nrm((na, N_HEADS_A, HEAD_A), 0.1),
        'rw_ln_w': 1.0 + nrm((na, D), 0.1),
        'rw_ln_b': nrm((na, D), 0.01),
        'rw_w_o': nrm((na, D, D), D ** -0.5),
        'moe_w_rg': nrm((DEPTH, D, G), D ** -0.5),
        'moe_b_rg': nrm((DEPTH, G), 0.01),
        'moe_w_re': nrm((DEPTH, D, G * E), D ** -0.5),
        'moe_b_re': nrm((DEPTH, G * E), 0.01),
        'moe_w_gu': nrm((DEPTH, G, E, D, 2 * F), D ** -0.5),
        'moe_w_down': nrm((DEPTH, G, E, F, D), F ** -0.5),
        'w_ada_kv': nrm((D, 2 * D), 0.5 * D ** -0.5),
        'b_ada_kv': nrm((2 * D,), 0.01),
        'g_kv': 1.0 + nrm((D,), 0.1),
        'w_kv': nrm((D, 2 * N_HEADS_B * HEAD_B), D ** -0.5),
        'at_w_q': nrm((nb, D, N_GROUPS_B * N_HEADS_B * HEAD_B), D ** -0.5),
        'at_w_o': nrm((nb, N_HEADS_B * HEAD_B, D), (N_HEADS_B * HEAD_B) ** -0.5),
        'rel_bias': nrm((NUM_BUCKETS, N_GROUPS_B * N_HEADS_B), 0.5),
        'g_final': 1.0 + nrm((D,), 0.1),
    }


def reference(x_prompt, x_sample, state_wkv, state_shift, cache_k, cache_v, c_prompt, c_sample,
              w_ada, b_ada, g_norm, rw_mu, rw_w_rkv, rw_w0, rw_w1, rw_w2, rw_a0, rw_a1, rw_a2, rw_g1, rw_g2,
              rw_k_k, rw_k_a, rw_r_k, rw_ln_w, rw_ln_b, rw_w_o, moe_w_rg, moe_b_rg, moe_w_re, moe_b_re,
              moe_w_gu, moe_w_down, w_ada_kv, b_ada_kv, g_kv, w_kv, at_w_q, at_w_o, rel_bias, g_final):
    p = dict(w_ada=w_ada, b_ada=b_ada, g_norm=g_norm, rw_mu=rw_mu, rw_w_rkv=rw_w_rkv, rw_w0=rw_w0, rw_w1=rw_w1,
             rw_w2=rw_w2, rw_a0=rw_a0, rw_a1=rw_a1, rw_a2=rw_a2, rw_g1=rw_g1, rw_g2=rw_g2, rw_k_k=rw_k_k,
             rw_k_a=rw_k_a, rw_r_k=rw_r_k, rw_ln_w=rw_ln_w, rw_ln_b=rw_ln_b, rw_w_o=rw_w_o, moe_w_rg=moe_w_rg,
             moe_b_rg=moe_b_rg, moe_w_re=moe_w_re, moe_b_re=moe_b_re, moe_w_gu=moe_w_gu, moe_w_down=moe_w_down,
             w_ada_kv=w_ada_kv, b_ada_kv=b_ada_kv, g_kv=g_kv, w_kv=w_kv, at_w_q=at_w_q, at_w_o=at_w_o,
             rel_bias=rel_bias, g_final=g_final)
    b_p, t_p = x_prompt.shape[:2]
    wkv0_p = jnp.zeros((N_A_LAYERS, b_p, N_HEADS_A, HEAD_A, HEAD_A), jnp.float32)
    shift0_p = jnp.zeros((N_A_LAYERS, b_p, D_MODEL), x_prompt.dtype)
    y_prompt, wkv_prompt, shift_prompt, k_full, v_full = run_trunk(x_prompt, c_prompt, wkv0_p, shift0_p, None, None, p)
    keep = min(W_MAX, t_p)
    k_prompt, v_prompt = k_full[:, t_p - keep:], v_full[:, t_p - keep:]
    y_sample, wkv_sample, shift_sample, k_sample, v_sample = run_trunk(x_sample, c_sample, state_wkv, state_shift,
                                                                       cache_k, cache_v, p)
    return (y_prompt, y_sample, wkv_prompt, shift_prompt, k_prompt, v_prompt,
            wkv_sample, shift_sample, k_sample, v_sample)
```

```python
import functools

import numpy as np
import jax
import jax.numpy as jnp
from jax import lax
from jax.experimental import pallas as pl
from jax.experimental.pallas import tpu as pltpu

F32 = jnp.float32
BF16 = jnp.bfloat16

D_MODEL = 1024
N_HEADS = 16
HEAD = 64
LANES = 128
N_PAIRS = D_MODEL // LANES
N_MIX = 6
GN_EPS = 64e-5
RMS_EPS = 1e-6
NEG_INF = -1e30
DILATIONS = (1, 4, 16)
BAND = 128
QB = 128
SCALE_B = HEAD ** -0.5
NUM_BUCKETS = 32
MAX_DISTANCE = 2048
N_GROUPS_E = 4
N_EXP_PER_GROUP = 8
N_EXPERTS = N_GROUPS_E * N_EXP_PER_GROUP
D_EXPERT = 512
ROUTER_LANE0 = N_GROUPS_E
SCAN_CHUNK = 64

VMEM_LIMIT = 56 * 1024 * 1024


def _cparams(sem):
    return pltpu.CompilerParams(dimension_semantics=sem, vmem_limit_bytes=VMEM_LIMIT)


def _dot(a, b):
    return jnp.dot(a.astype(BF16), b.astype(BF16), preferred_element_type=F32)


def _dot_nt(a, b):
    return lax.dot_general(a.astype(BF16), b.astype(BF16), (((1,), (1,)), ((), ())),
                           preferred_element_type=F32)


def _split3(a):
    hi = a.astype(BF16)
    r1 = a - hi.astype(F32)
    mid = r1.astype(BF16)
    lo = (r1 - mid.astype(F32)).astype(BF16)
    return hi, mid, lo


def _dot3(a, b01):
    hi, mid, lo = _split3(a)
    b = b01.astype(BF16)
    return (jnp.dot(hi, b, preferred_element_type=F32) + jnp.dot(mid, b, preferred_element_type=F32)
            + jnp.dot(lo, b, preferred_element_type=F32))


def _dot3_left(a01, b):
    hi, mid, lo = _split3(b)
    a = a01.astype(BF16)
    return (jnp.dot(a, hi, preferred_element_type=F32) + jnp.dot(a, mid, preferred_element_type=F32)
            + jnp.dot(a, lo, preferred_element_type=F32))


def _sigmoid(x):
    return 1.0 / (1.0 + jnp.exp(-x))


def _silu(x):
    return x * _sigmoid(x)


def _softplus(z):
    return jnp.maximum(z, 0.0) + jnp.log(1.0 + jnp.exp(-jnp.abs(z)))


def _norm_mod(x, g, sh, sc):
    ms = jnp.mean(x * x, axis=-1, keepdims=True)
    return (x * lax.rsqrt(ms + RMS_EPS) * g) * (1.0 + sc) + sh


def _head_sum_matrix():
    r = lax.broadcasted_iota(jnp.int32, (LANES, LANES), 0) >= HEAD
    c = lax.broadcasted_iota(jnp.int32, (LANES, LANES), 1) >= HEAD
    return jnp.where(r == c, 1.0, 0.0).astype(BF16)


def _head_sum(z, m128):
    parts = [_dot3(z[:, LANES * p:LANES * (p + 1)], m128) for p in range(N_PAIRS)]
    return jnp.concatenate(parts, axis=1)


def _row_spec(arr, tm):
    n = arr.shape[-1]
    if arr.shape[1] == 1:
        return pl.BlockSpec((1, 1, n), lambda b, i: (b, 0, 0))
    return pl.BlockSpec((1, tm, n), lambda b, i: (b, i, 0))


def _const_spec(arr):
    nd = arr.ndim
    return pl.BlockSpec(arr.shape, lambda b, i, _nd=nd: (0,) * _nd)


def _ada_kernel(c_ref, w_ref, b_ref, o_ref):
    o_ref[...] = _dot(_silu(c_ref[...]), w_ref[...]) + b_ref[...]


def _ada_linear(c_all, w, b):
    m, k = c_all.shape
    n = w.shape[1]
    tn = 1024
    return pl.pallas_call(
        _ada_kernel,
        out_shape=jax.ShapeDtypeStruct((m, n), F32),
        grid=(n // tn,),
        in_specs=[pl.BlockSpec((m, k), lambda j: (0, 0)),
                  pl.BlockSpec((k, tn), lambda j: (0, j)),
                  pl.BlockSpec((1, tn), lambda j: (0, j))],
        out_specs=pl.BlockSpec((m, tn), lambda j: (0, j)),
        compiler_params=_cparams(("parallel",)),
        name="ada_linear",
    )(c_all, w, b.reshape(1, n))


def _norm_mod_kernel(x_ref, g_ref, sh_ref, sc_ref, o_ref):
    o_ref[0] = _norm_mod(x_ref[0], g_ref[...], sh_ref[0], sc_ref[0])


def _norm_mod_call(x, g, sh, sc, tm):
    b, t, d = x.shape
    g2 = g.reshape(1, d)
    return pl.pallas_call(
        _norm_mod_kernel,
        out_shape=jax.ShapeDtypeStruct(x.shape, F32),
        grid=(b, t // tm),
        in_specs=[_row_spec(x, tm), _const_spec(g2), _row_spec(sh, tm), _row_spec(sc, tm)],
        out_specs=pl.BlockSpec((1, tm, d), lambda bb, i: (bb, i, 0)),
        compiler_params=_cparams(("parallel", "parallel")),
        name="norm_mod",
    )(x, g2, sh, sc)


def _final_norm_kernel(x_ref, g_ref, o_ref):
    x = x_ref[0]
    ms = jnp.mean(x * x, axis=-1, keepdims=True)
    o_ref[0] = x * lax.rsqrt(ms + RMS_EPS) * g_ref[...]


def _final_norm_call(x, g, tm):
    b, t, d = x.shape
    g2 = g.reshape(1, d)
    return pl.pallas_call(
        _final_norm_kernel,
        out_shape=jax.ShapeDtypeStruct(x.shape, F32),
        grid=(b, t // tm),
        in_specs=[_row_spec(x, tm), _const_spec(g2)],
        out_specs=pl.BlockSpec((1, tm, d), lambda bb, i: (bb, i, 0)),
        compiler_params=_cparams(("parallel", "parallel")),
        name="final_norm",
    )(x, g2)


def _linear_kernel(*refs, has_norm, has_resid):
    it = iter(refs)
    x_ref = next(it)
    w_ref = next(it)
    if has_norm:
        g_ref, sh_ref, sc_ref = next(it), next(it), next(it)
    if has_resid:
        xr_ref, gt_ref = next(it), next(it)
    o_ref = next(it)
    x = x_ref[0]
    if has_norm:
        x = _norm_mod(x, g_ref[...], sh_ref[0], sc_ref[0])
    out = _dot(x, w_ref[...])
    if has_resid:
        out = xr_ref[0] + gt_ref[0] * out
    o_ref[0] = out


def _linear_call(x, w, tm, norm=None, resid=None, name="linear"):
    b, t, k = x.shape
    n = w.shape[1]
    args = [x, w]
    specs = [_row_spec(x, tm), _const_spec(w)]
    if norm is not None:
        g, sh, sc = norm
        g2 = g.reshape(1, k)
        args += [g2, sh, sc]
        specs += [_const_spec(g2), _row_spec(sh, tm), _row_spec(sc, tm)]
    if resid is not None:
        xr, gt = resid
        args += [xr, gt]
        specs += [_row_spec(xr, tm), _row_spec(gt, tm)]
    return pl.pallas_call(
        functools.partial(_linear_kernel, has_norm=norm is not None, has_resid=resid is not None),
        out_shape=jax.ShapeDtypeStruct((b, t, n), F32),
        grid=(b, t // tm),
        in_specs=specs,
        out_specs=pl.BlockSpec((1, tm, n), lambda bb, i: (bb, i, 0)),
        compiler_params=_cparams(("parallel", "parallel")),
        name=name,
    )(*args)


def _rwkv_proj_kernel(h_ref, hp_ref, mu_ref, vec_ref, wrkv_ref, w1_ref, w2_ref, a1_ref, a2_ref, g1_ref, g2_ref,
                      r_ref, lw_ref, k_ref, v_ref, a_ref, b_ref, g_ref, bonus_ref):
    h = h_ref[0]
    xx = hp_ref[0] - h

    def mix(i):
        return h + xx * mu_ref[i:i + 1, :]

    w0, a0, k_k, k_a, r_k = (vec_ref[i:i + 1, :] for i in range(5))
    m128 = _head_sum_matrix()
    r = _dot(mix(0), wrkv_ref[0])
    k = _dot(mix(1), wrkv_ref[1])
    v = _dot(mix(2), wrkv_ref[2])
    wl = w0 + _dot(jnp.tanh(_dot(mix(3), w1_ref[...])), w2_ref[...])
    w_log = -_softplus(-wl) - 0.5
    a_sig = _sigmoid(a0 + _dot(_dot(mix(4), a1_ref[...]), a2_ref[...]))
    g = _dot(_sigmoid(_dot(mix(5), g1_ref[...])), g2_ref[...])
    kk = k * k_k
    kk = kk / jnp.maximum(jnp.sqrt(_head_sum(kk * kk, m128)), 1e-12)
    k2 = k * (1.0 + (a_sig - 1.0) * k_a)
    r_ref[0] = r
    lw_ref[0] = -jnp.exp(w_log)
    k_ref[0] = k2
    v_ref[0] = v
    a_ref[0] = -kk
    b_ref[0] = kk * a_sig
    g_ref[0] = g
    bonus_ref[0] = _head_sum(r * k2 * r_k, m128) * v


def _rwkv_proj_call(h, hp, mu, vecs, wrkv, w1, w2, a1, a2, g1, g2, tm):
    b, t, d = h.shape
    consts = [mu, vecs, wrkv, w1, w2, a1, a2, g1, g2]
    out = jax.ShapeDtypeStruct((b, t, d), F32)
    ospec = pl.BlockSpec((1, tm, d), lambda bb, i: (bb, i, 0))
    return pl.pallas_call(
        _rwkv_proj_kernel,
        out_shape=[out] * 8,
        grid=(b, t // tm),
        in_specs=[_row_spec(h, tm), _row_spec(hp, tm)] + [_const_spec(c) for c in consts],
        out_specs=[ospec] * 8,
        compiler_params=_cparams(("parallel", "parallel")),
        name="rwkv_proj",
    )(h, hp, *consts)


def _rwkv_scan_kernel(r_ref, lw_ref, k_ref, v_ref, a_ref, b_ref, y_ref, hout_ref, h_scr):
    c = pl.program_id(1)
    nc = pl.num_programs(1)
    C = SCAN_CHUNK

    @pl.when(c == 0)
    def _():
        h_scr[...] = jnp.zeros_like(h_scr)

    lw = lw_ref[0]
    ri = lax.broadcasted_iota(jnp.int32, (C, C), 0)
    ci = lax.broadcasted_iota(jnp.int32, (C, C), 1)
    tril = jnp.where(ri >= ci, 1.0, 0.0).astype(BF16)
    cum = _dot3_left(tril, lw)
    gam = jnp.exp(cum)
    ginv = jnp.exp(-cum)
    g_last = gam[C - 1:C, :]
    rt = r_ref[0] * gam
    at = a_ref[0] * jnp.exp(cum - lw)
    bt = b_ref[0] * ginv
    kt = k_ref[0] * ginv
    bh = bt * g_last
    kh = kt * g_last
    v = v_ref[0]

    head0 = lax.broadcasted_iota(jnp.int32, (C, LANES), 1) < HEAD
    tt = lax.broadcasted_iota(jnp.int32, (2 * C, 2 * C), 0) & (C - 1)
    ss = lax.broadcasted_iota(jnp.int32, (2 * C, 2 * C), 1) & (C - 1)
    strict = tt > ss
    incl = tt >= ss
    n_double = C.bit_length() - 1

    ys = []
    for p in range(N_PAIRS):
        sl = slice(LANES * p, LANES * (p + 1))

        def two(x):
            xs = x[:, sl]
            return jnp.concatenate([jnp.where(head0, xs, 0.0), jnp.where(head0, 0.0, xs)], axis=0)

        a2, r2, b2, k2, v2, bh2, kh2 = (two(z) for z in (at, rt, bt, kt, v, bh, kh))
        ar = jnp.concatenate([a2, r2], axis=0).astype(BF16)
        bk = jnp.concatenate([b2, k2], axis=0).astype(BF16)
        s = _dot_nt(ar, bk)
        n2 = 2 * C
        l_ab = jnp.where(strict, s[:n2, :n2], 0.0)
        l_ak = jnp.where(strict, s[:n2, n2:], 0.0)
        t_rb = jnp.where(incl, s[n2:, :n2], 0.0)
        t_rk = jnp.where(incl, s[n2:, n2:], 0.0)
        hp = h_scr[p]
        arh = _dot(ar, hp)
        lv = _dot(jnp.concatenate([l_ak, t_rk], axis=0), v2)
        u = arh[:n2] + lv[:n2]
        pw = l_ab
        for it in range(n_double):
            u = u + _dot(pw, u)
            if it + 1 < n_double:
                pw = _dot(pw, pw)
        y2 = arh[n2:] + lv[n2:] + _dot(t_rb, u)
        ys.append(y2[:C] + y2[C:])
        gcol = jnp.transpose(jnp.broadcast_to(g_last[:, sl], (LANES, LANES)))
        w2t = jnp.transpose(jnp.concatenate([bh2, kh2], axis=0))
        z2 = jnp.concatenate([u, v2], axis=0)
        h_scr[p] = gcol * hp + _dot(w2t, z2)
    y_ref[0] = jnp.concatenate(ys, axis=1)

    @pl.when(c == nc - 1)
    def _():
        hout_ref[0] = h_scr[...]


def _rwkv_scan_call(r, lw, k, v, a, bb):
    b, t, d = r.shape
    C = SCAN_CHUNK
    spec = pl.BlockSpec((1, C, d), lambda bi, c: (bi, c, 0))
    y, hout = pl.pallas_call(
        _rwkv_scan_kernel,
        out_shape=[jax.ShapeDtypeStruct((b, t, d), F32),
                   jax.ShapeDtypeStruct((b, N_PAIRS, LANES, LANES), F32)],
        grid=(b, t // C),
        in_specs=[spec] * 6,
        out_specs=[spec, pl.BlockSpec((1, N_PAIRS, LANES, LANES), lambda bi, c: (bi, 0, 0, 0))],
        scratch_shapes=[pltpu.VMEM((N_PAIRS, LANES, LANES), F32)],
        compiler_params=_cparams(("parallel", "arbitrary")),
        name="rwkv_scan",
    )(r, lw, k, v, a, bb)
    hh = hout.reshape(b, N_PAIRS, 2, HEAD, 2, HEAD)
    hh = jnp.stack([hh[:, :, 0, :, 0, :], hh[:, :, 1, :, 1, :]], axis=2).reshape(b, N_HEADS, HEAD, HEAD)
    return y, jnp.swapaxes(hh, -1, -2)


def _rwkv_step_kernel(s_ref, z_ref, repj_ref, repi_ref, segi_ref, so_ref, y_ref):
    s = s_ref[...]
    z = z_ref[...]
    repj, repi, segi = repj_ref[...], repi_ref[...], segi_ref[...]
    r, lw, k, v, a, b = (z[:, HEAD * i:HEAD * (i + 1)] for i in range(6))
    at = _dot3(a, repj)
    sa = _dot3(s * at, segi)
    sn = (s * jnp.exp(_dot3(lw, repj)) + _dot3(sa, repi) * _dot3(b, repj)
          + _dot3(v, repi) * _dot3(k, repj))
    so_ref[...] = sn
    y_ref[...] = _dot3(sn * _dot3(r, repj), segi)


def _rwkv_step_call(state, r, lw, k, v, a, bb):
    nb = state.shape[0]
    rows = nb * N_HEADS
    flat = HEAD * HEAD
    s2 = state.reshape(rows, flat)
    z = jnp.concatenate([q.reshape(rows, HEAD) for q in (r, lw, k, v, a, bb)], axis=1)
    lane = np.arange(flat)
    repj = jnp.asarray(np.arange(HEAD)[:, None] == (lane % HEAD)[None, :], BF16)
    repi = jnp.asarray(np.arange(HEAD)[:, None] == (lane // HEAD)[None, :], BF16)
    segi = jnp.asarray((lane // HEAD)[:, None] == np.arange(HEAD)[None, :], BF16)
    tmr = 128
    so, y = pl.pallas_call(
        _rwkv_step_kernel,
        out_shape=[jax.ShapeDtypeStruct((rows, flat), F32), jax.ShapeDtypeStruct((rows, HEAD), F32)],
        grid=(rows // tmr,),
        in_specs=[pl.BlockSpec((tmr, flat), lambda i: (i, 0)),
                  pl.BlockSpec((tmr, 6 * HEAD), lambda i: (i, 0)),
                  pl.BlockSpec((HEAD, flat), lambda i: (0, 0)),
                  pl.BlockSpec((HEAD, flat), lambda i: (0, 0)),
                  pl.BlockSpec((flat, HEAD), lambda i: (0, 0))],
        out_specs=[pl.BlockSpec((tmr, flat), lambda i: (i, 0)), pl.BlockSpec((tmr, HEAD), lambda i: (i, 0))],
        compiler_params=_cparams(("parallel",)),
        name="rwkv_step",
    )(s2, z, repj, repi, segi)
    return y.reshape(1, nb, D_MODEL), so.reshape(nb, N_HEADS, HEAD, HEAD)


def _rwkv_out_kernel(y_ref, bonus_ref, g_ref, x_ref, gt_ref, ln_ref, wo_ref, o_ref):
    y = y_ref[0]
    m128 = _head_sum_matrix()
    mean = _head_sum(y, m128) * (1.0 / HEAD)
    dlt = y - mean
    var = _head_sum(dlt * dlt, m128) * (1.0 / HEAD)
    yn = dlt * lax.rsqrt(var + GN_EPS) * ln_ref[0:1, :] + ln_ref[1:2, :]
    out = _dot((yn + bonus_ref[0]) * g_ref[0], wo_ref[...])
    o_ref[0] = x_ref[0] + gt_ref[0] * out


def _rwkv_out_call(y, bonus, g, x, gt, ln, wo, tm):
    b, t, d = y.shape
    return pl.pallas_call(
        _rwkv_out_kernel,
        out_shape=jax.ShapeDtypeStruct((b, t, d), F32),
        grid=(b, t // tm),
        in_specs=[_row_spec(y, tm), _row_spec(bonus, tm), _row_spec(g, tm), _row_spec(x, tm), _row_spec(gt, tm),
                  _const_spec(ln), _const_spec(wo)],
        out_specs=pl.BlockSpec((1, tm, d), lambda bb, i: (bb, i, 0)),
        compiler_params=_cparams(("parallel", "parallel")),
        name="rwkv_out",
    )(y, bonus, g, x, gt, ln, wo)


def _router_kernel(x_ref, g_ref, sh_ref, sc_ref, wr_ref, br_ref, hf_ref, wall_ref):
    hf = _norm_mod(x_ref[0], g_ref[...], sh_ref[0], sc_ref[0])
    hf_ref[0] = hf.astype(BF16)
    logits = jnp.dot(hf, wr_ref[...], preferred_element_type=F32, precision=lax.Precision.HIGHEST) + br_ref[...]
    lane = lax.broadcasted_iota(jnp.int32, logits.shape, 1).astype(F32)
    big = float(1 << 20)
    gl = jnp.where(lane < N_GROUPS_E, logits, NEG_INF)
    gmax = jnp.max(gl, axis=-1, keepdims=True)
    p_top = 1.0 / jnp.sum(jnp.exp(gl - gmax), axis=-1, keepdims=True)
    g_sel = jnp.min(jnp.where(gl == gmax, lane, big), axis=-1, keepdims=True)
    lo = ROUTER_LANE0 + N_EXP_PER_GROUP * g_sel
    el = jnp.where((lane >= lo) & (lane < lo + N_EXP_PER_GROUP), logits, NEG_INF)
    v1 = jnp.max(el, axis=-1, keepdims=True)
    i1 = jnp.min(jnp.where(el == v1, lane, big), axis=-1, keepdims=True)
    el2 = jnp.where(lane == i1, NEG_INF, el)
    v2 = jnp.max(el2, axis=-1, keepdims=True)
    i2 = jnp.min(jnp.where(el2 == v2, lane, big), axis=-1, keepdims=True)
    e21 = jnp.exp(v2 - v1)
    w1 = p_top / (1.0 + e21)
    w2 = p_top * e21 / (1.0 + e21)
    wall_ref[0] = jnp.where(lane == i1, w1, 0.0) + jnp.where(lane == i2, w2, 0.0)


def _router_call(x, g, sh, sc, wr, br, tm):
    b, t, d = x.shape
    g2 = g.reshape(1, d)
    return pl.pallas_call(
        _router_kernel,
        out_shape=[jax.ShapeDtypeStruct((b, t, d), BF16), jax.ShapeDtypeStruct((b, t, LANES), F32)],
        grid=(b, t // tm),
        in_specs=[_row_spec(x, tm), _const_spec(g2), _row_spec(sh, tm), _row_spec(sc, tm),
                  _const_spec(wr), _const_spec(br)],
        out_specs=[pl.BlockSpec((1, tm, d), lambda bb, i: (bb, i, 0)),
                   pl.BlockSpec((1, tm, LANES), lambda bb, i: (bb, i, 0))],
        compiler_params=_cparams(("parallel", "parallel")),
        name="moe_router",
    )(x, g2, sh, sc, wr, br)


def _moe_dense_kernel(hf_ref, wall_ref, wgu_ref, wd_ref, x_ref, gt_ref, o_ref, acc_ref):
    e = pl.program_id(2)

    @pl.when(e == 0)
    def _():
        acc_ref[...] = jnp.zeros_like(acc_ref)

    wall = wall_ref[0]
    lane = lax.broadcasted_iota(jnp.int32, wall.shape, 1)
    wcol = jnp.sum(jnp.where(lane == e + ROUTER_LANE0, wall, 0.0), axis=-1, keepdims=True)
    gu = jnp.dot(hf_ref[0], wgu_ref[0], preferred_element_type=F32)
    act = _silu(gu[:, :D_EXPERT]) * gu[:, D_EXPERT:] * wcol
    acc_ref[...] += _dot(act, wd_ref[0])

    @pl.when(e == pl.num_programs(2) - 1)
    def _():
        o_ref[0] = x_ref[0] + gt_ref[0] * acc_ref[...]


def _moe_dense_call(hf, wall, wgu, wd, x, gt, tm):
    b, t, d = x.shape

    def rs(arr):
        n = arr.shape[-1]
        if arr.shape[1] == 1:
            return pl.BlockSpec((1, 1, n), lambda bb, i, e: (bb, 0, 0))
        return pl.BlockSpec((1, tm, n), lambda bb, i, e: (bb, i, 0))

    return pl.pallas_call(
        _moe_dense_kernel,
        out_shape=jax.ShapeDtypeStruct((b, t, d), F32),
        grid=(b, t // tm, N_EXPERTS),
        in_specs=[rs(hf), rs(wall),
                  pl.BlockSpec((1, d, 2 * D_EXPERT), lambda bb, i, e: (e, 0, 0)),
                  pl.BlockSpec((1, D_EXPERT, d), lambda bb, i, e: (e, 0, 0)),
                  rs(x), rs(gt)],
        out_specs=pl.BlockSpec((1, tm, d), lambda bb, i, e: (bb, i, 0)),
        scratch_shapes=[pltpu.VMEM((tm, d), F32)],
        compiler_params=_cparams(("parallel", "parallel", "arbitrary")),
        name="moe_dense",
    )(hf, wall, wgu, wd, x, gt)


def _t5_bucket(dist):
    dist = np.asarray(dist)
    max_exact = NUM_BUCKETS // 2
    log_ratio = np.log(np.maximum(dist, 1) / max_exact) / np.log(MAX_DISTANCE / max_exact)
    large = np.minimum(max_exact + (log_ratio * (NUM_BUCKETS - max_exact)).astype(np.int32), NUM_BUCKETS - 1)
    return np.where(dist < max_exact, dist, large).astype(np.int32)


def _attn_prompt_kernel(q_ref, kp_ref, kc_ref, vp_ref, vc_ref, bias_ref, o_ref, lse_ref):
    i = pl.program_id(2)
    q = q_ref[0].astype(BF16)
    kcat = jnp.concatenate([kp_ref[0], kc_ref[0]], axis=0).astype(BF16)
    vcat = jnp.concatenate([vp_ref[0], vc_ref[0]], axis=0).astype(BF16)
    head0 = lax.broadcasted_iota(jnp.int32, (QB, LANES), 1) < HEAD
    col = lax.broadcasted_iota(jnp.int32, (QB, 2 * QB), 1)
    key_ok = col >= jnp.where(i > 0, 0, QB)
    lane = lax.broadcasted_iota(jnp.int32, (QB, LANES), 1)
    lse_tile = jnp.zeros((QB, LANES), F32)
    outs = []
    zero = jnp.zeros((), BF16)
    for p in range(N_PAIRS):
        sl = slice(LANES * p, LANES * (p + 1))
        qp, kp, vp = q[:, sl], kcat[:, sl], vcat[:, sl]
        o_pair = []
        for hh in range(2):
            h = 2 * p + hh
            qm = jnp.where(head0, qp, zero) if hh == 0 else jnp.where(head0, zero, qp)
            logits = _dot_nt(qm, kp) * SCALE_B + bias_ref[h]
            logits = jnp.where(key_ok, logits, NEG_INF)
            m = jnp.max(logits, axis=-1, keepdims=True)
            pr = jnp.exp(logits - m)
            ssum = jnp.sum(pr, axis=-1, keepdims=True)
            o_pair.append(_dot(pr, vp) / ssum)
            lse_tile = lse_tile + jnp.where(lane == h, m + jnp.log(ssum), 0.0)
        outs.append(jnp.where(head0, o_pair[0], o_pair[1]))
    o_ref[0] = jnp.concatenate(outs, axis=1)
    lse_ref[0] = lse_tile


def _attn_prompt_call(q, k, v, bias, gi, dil):
    b, t, _ = k.shape
    L = t // dil
    nblk = L // QB
    d = D_MODEL
    qv = q.reshape(b, L, dil * 3 * d)
    kv = k.reshape(b, L, dil * d)
    vv = v.reshape(b, L, dil * d)
    cur = pl.BlockSpec((1, QB, d), lambda bb, r, i: (bb, i, r))
    prev = pl.BlockSpec((1, QB, d), lambda bb, r, i: (bb, jnp.maximum(i - 1, 0), r))
    o, lse = pl.pallas_call(
        _attn_prompt_kernel,
        out_shape=[jax.ShapeDtypeStruct((b, L, dil * d), F32), jax.ShapeDtypeStruct((b, L, dil * LANES), F32)],
        grid=(b, dil, nblk),
        in_specs=[pl.BlockSpec((1, QB, d), lambda bb, r, i: (bb, i, r * 3 + gi)),
                  prev, cur, prev, cur,
                  pl.BlockSpec(bias.shape, lambda bb, r, i: (0, 0, 0))],
        out_specs=[cur, pl.BlockSpec((1, QB, LANES), lambda bb, r, i: (bb, i, r))],
        compiler_params=_cparams(("parallel", "parallel", "parallel")),
        name=f"attn_prompt_d{dil}",
    )(qv, kv, kv, vv, vv, bias)
    return o.reshape(b, t, d), lse.reshape(b, t, LANES)


def _attn_merge_kernel(o0_ref, o1_ref, o2_ref, l0_ref, l1_ref, l2_ref, e_ref, o_ref):
    l0, l1, l2 = l0_ref[0], l1_ref[0], l2_ref[0]
    mx = jnp.maximum(jnp.maximum(l0, l1), l2)
    e0, e1, e2 = jnp.exp(l0 - mx), jnp.exp(l1 - mx), jnp.exp(l2 - mx)
    inv = 1.0 / (e0 + e1 + e2)
    em = e_ref[...]
    o_ref[0] = (_dot3(e0 * inv, em) * o0_ref[0] + _dot3(e1 * inv, em) * o1_ref[0]
                + _dot3(e2 * inv, em) * o2_ref[0])


def _head_expand_matrix():
    return jnp.asarray(np.arange(LANES)[:, None] == (np.arange(D_MODEL) // HEAD)[None, :], BF16)


def _attn_merge_call(os_, ls_, tm):
    b, t, d = os_[0].shape
    em = _head_expand_matrix()
    args = list(os_) + list(ls_) + [em]
    return pl.pallas_call(
        _attn_merge_kernel,
        out_shape=jax.ShapeDtypeStruct((b, t, d), F32),
        grid=(b, t // tm),
        in_specs=[_row_spec(a, tm) for a in args[:-1]] + [_const_spec(em)],
        out_specs=pl.BlockSpec((1, tm, d), lambda bb, i: (bb, i, 0)),
        compiler_params=_cparams(("parallel", "parallel")),
        name="attn_merge",
    )(*args)


def _attn_sample_kernel(q_ref, kn_ref, vn_ref, k1_ref, k4_ref, k16_ref, v1_ref, v4_ref, v16_ref,
                        bk_ref, b0_ref, seg_ref, e_ref, o_ref):
    seg = seg_ref[...]
    em = e_ref[...]
    q = q_ref[0]
    kn = kn_ref[0]
    vn = vn_ref[0]
    k_refs = (k1_ref, k4_ref, k16_ref)
    v_refs = (v1_ref, v4_ref, v16_ref)
    ms, ss, nums = [], [], []
    for g in range(len(DILATIONS)):
        qg = q[:, D_MODEL * g:D_MODEL * (g + 1)]
        lk = _dot3(k_refs[g][0] * qg, seg) * SCALE_B + bk_ref[g]
        l0 = _dot3(jnp.broadcast_to(kn * qg, (8, D_MODEL)), seg)[0:1] * SCALE_B + b0_ref[g]
        m = jnp.maximum(jnp.max(lk, axis=0, keepdims=True), l0)
        pk = jnp.exp(lk - m)
        p0 = jnp.exp(l0 - m)
        ss.append(jnp.sum(pk, axis=0, keepdims=True) + p0)
        num = jnp.sum(_dot3(pk, em) * v_refs[g][0], axis=0, keepdims=True)
        num = num + _dot3(jnp.broadcast_to(p0, (8, LANES)), em)[0:1] * vn
        nums.append(num)
        ms.append(m)
    mx = jnp.maximum(jnp.maximum(ms[0], ms[1]), ms[2])
    cs = [jnp.exp(m - mx) for m in ms]
    inv = 1.0 / (cs[0] * ss[0] + cs[1] * ss[1] + cs[2] * ss[2])
    out = jnp.zeros((1, D_MODEL), F32)
    for g in range(len(DILATIONS)):
        out = out + _dot3(jnp.broadcast_to(cs[g] * inv, (8, LANES)), em)[0:1] * nums[g]
    o_ref[0] = out


def _attn_sample_call(q, kn, vn, cache_k, cache_v, bias_k, bias_0):
    nb, n_buf = cache_k.shape[:2]
    d = D_MODEL
    assert n_buf == BAND * max(DILATIONS)

    def views(cache):
        out = []
        for dil in DILATIONS:
            L = n_buf // dil
            out.append(cache.reshape(nb, L, dil * d))
        return out

    def cspec(dil):
        blk = n_buf // dil // BAND - 1
        return pl.BlockSpec((1, BAND, d), lambda bb, _blk=blk: (bb, _blk, 0))

    seg = jnp.asarray((np.arange(d) // HEAD)[:, None] == np.arange(LANES)[None, :], BF16)
    em = _head_expand_matrix()
    one = lambda n: pl.BlockSpec((1, 1, n), lambda bb: (bb, 0, 0))
    full = lambda a: pl.BlockSpec(a.shape, lambda bb, _n=a.ndim: (0,) * _n)
    return pl.pallas_call(
        _attn_sample_kernel,
        out_shape=jax.ShapeDtypeStruct((nb, 1, d), F32),
        grid=(nb,),
        in_specs=[one(3 * d), one(d), one(d)] + [cspec(dl) for dl in DILATIONS] * 2
                 + [full(bias_k), full(bias_0), full(seg), full(em)],
        out_specs=one(d),
        compiler_params=_cparams(("parallel",)),
        name="attn_sample",
    )(q, kn, vn, *views(cache_k), *views(cache_v), bias_k, bias_0, seg, em)


def _prompt_bias(rel_bias, gi, dil):
    delta = np.arange(QB)[:, None] + QB - np.arange(2 * QB)[None, :]
    in_band = (delta >= 0) & (delta <= BAND)
    table = rel_bias[:, gi * N_HEADS:(gi + 1) * N_HEADS]
    bias = jnp.transpose(table[_t5_bucket(np.clip(delta, 0, BAND) * dil)], (2, 0, 1)).astype(F32)
    return jnp.where(in_band[None], bias, NEG_INF)


def _sample_bias(rel_bias):
    bks, b0s = [], []
    for gi, dil in enumerate(DILATIONS):
        table = rel_bias[:, gi * N_HEADS:(gi + 1) * N_HEADS].astype(F32)
        tb = table[_t5_bucket(dil * np.arange(BAND + 1))]
        tb = jnp.pad(tb, ((0, 0), (0, LANES - N_HEADS)))
        bks.append(tb[1:][::-1])
        b0s.append(tb[0:1])
    return jnp.stack(bks), jnp.stack(b0s)


def _trunk(x, mods, wkv0, shift0, caches, p, tm, tm_moe):
    b, t, d = x.shape
    m0, m1, mkv = mods["l0"], mods["l1"], mods["kv"]

    sh_m, sc_m, gt_m, sh_c, sc_c, gt_c = m0
    h = _norm_mod_call(x, p["g_norm"][0, 0], sh_m, sc_m, tm)
    if shift0 is None:
        hp = jnp.concatenate([jnp.zeros((b, 1, d), F32), h[:, :-1]], axis=1)
    else:
        hp = shift0
    r, lw, k, v, a, bb, g, bonus = _rwkv_proj_call(h, hp, p["rw_mu"], p["rw_vecs"], p["rw_w_rkv"], p["rw_w1"],
                                                   p["rw_w2"], p["rw_a1"], p["rw_a2"], p["rw_g1"], p["rw_g2"],
                                                   min(tm, 128))
    if wkv0 is None:
        y, wkv = _rwkv_scan_call(r, lw, k, v, a, bb)
        shift = h[:, -1]
    else:
        y, wkv = _rwkv_step_call(wkv0, r, lw, k, v, a, bb)
        shift = h[0]
    x = _rwkv_out_call(y, bonus, g, x, gt_m, p["rw_ln"], p["rw_w_o"], tm)
    hf, wall = _router_call(x, p["g_norm"][0, 1], sh_c, sc_c, p["moe_wr"][0], p["moe_br"][0], tm)
    x = _moe_dense_call(hf, wall, p["moe_w_gu"][0], p["moe_w_down"][0], x, gt_c, tm_moe)

    sh_k, sc_k = mkv
    kvp = _linear_call(x, p["w_kv"], tm, norm=(p["g_kv"], sh_k, sc_k), name="kv_proj")
    k_sh, v_sh = kvp[..., :d], kvp[..., d:]

    sh_m, sc_m, gt_m, sh_c, sc_c, gt_c = m1
    q = _linear_call(x, p["at_w_q"], tm, norm=(p["g_norm"][1, 0], sh_m, sc_m), name="q_proj")
    if caches is None:
        os_, ls_ = [], []
        for gi, dil in enumerate(DILATIONS):
            o, lse = _attn_prompt_call(q, k_sh, v_sh, _prompt_bias(p["rel_bias"], gi, dil), gi, dil)
            os_.append(o)
            ls_.append(lse)
        o = _attn_merge_call(os_, ls_, tm)
    else:
        cache_k, cache_v = caches
        bias_k, bias_0 = _sample_bias(p["rel_bias"])
        nb = cache_k.shape[0]
        o = _attn_sample_call(q.reshape(nb, 1, 3 * d), k_sh.reshape(nb, 1, d), v_sh.reshape(nb, 1, d),
                              cache_k, cache_v, bias_k, bias_0).reshape(b, t, d)
    x = _linear_call(o, p["at_w_o"], tm, resid=(x, gt_m), name="attn_out")
    hf, wall = _router_call(x, p["g_norm"][1, 1], sh_c, sc_c, p["moe_wr"][1], p["moe_br"][1], tm)
    x = _moe_dense_call(hf, wall, p["moe_w_gu"][1], p["moe_w_down"][1], x, gt_c, tm_moe)
    y = _final_norm_call(x, p["g_final"], tm)
    return y, wkv, shift, k_sh, v_sh


def _prepare_params(w_ada, b_ada, g_norm, rw_mu, rw_w_rkv, rw_w0, rw_w1, rw_w2, rw_a0, rw_a1, rw_a2, rw_g1, rw_g2,
                    rw_k_k, rw_k_a, rw_r_k, rw_ln_w, rw_ln_b, rw_w_o, moe_w_rg, moe_b_rg, moe_w_re, moe_b_re,
                    moe_w_gu, moe_w_down, w_ada_kv, b_ada_kv, g_kv, w_kv, at_w_q, at_w_o, rel_bias, g_final):
    d = D_MODEL
    zeros = jnp.zeros((3, d), F32)
    pad = LANES - N_GROUPS_E - N_EXPERTS
    depth = moe_w_rg.shape[0]
    return dict(
        g_norm=g_norm, g_kv=g_kv, g_final=g_final, rel_bias=rel_bias,
        rw_mu=rw_mu[0],
        rw_vecs=jnp.concatenate([rw_w0[0][None], rw_a0[0][None], rw_k_k[0][None], rw_k_a[0][None],
                                 rw_r_k[0].reshape(1, d), zeros], axis=0),
        rw_w_rkv=rw_w_rkv[0].astype(BF16), rw_w1=rw_w1[0], rw_w2=rw_w2[0], rw_a1=rw_a1[0], rw_a2=rw_a2[0],
        rw_g1=rw_g1[0], rw_g2=rw_g2[0],
        rw_ln=jnp.concatenate([rw_ln_w[0][None], rw_ln_b[0][None], zeros, zeros], axis=0),
        rw_w_o=rw_w_o[0].astype(BF16),
        moe_wr=jnp.pad(jnp.concatenate([moe_w_rg, moe_w_re], axis=-1), ((0, 0), (0, 0), (0, pad))),
        moe_br=jnp.pad(jnp.concatenate([moe_b_rg, moe_b_re], axis=-1), ((0, 0), (0, pad)))[:, None, :],
        moe_w_gu=moe_w_gu.reshape(depth, N_EXPERTS, d, 2 * D_EXPERT).astype(BF16),
        moe_w_down=moe_w_down.reshape(depth, N_EXPERTS, D_EXPERT, d).astype(BF16),
        w_kv=w_kv.astype(BF16), at_w_q=at_w_q[0].astype(BF16), at_w_o=at_w_o[0].astype(BF16),
    )


def kernel(x_prompt, x_sample, state_wkv, state_shift, cache_k, cache_v, c_prompt, c_sample, w_ada, b_ada, g_norm, rw_mu, rw_w_rkv, rw_w0, rw_w1, rw_w2, rw_a0, rw_a1, rw_a2, rw_g1, rw_g2, rw_k_k, rw_k_a, rw_r_k, rw_ln_w, rw_ln_b, rw_w_o, moe_w_rg, moe_b_rg, moe_w_re, moe_b_re, moe_w_gu, moe_w_down, w_ada_kv, b_ada_kv, g_kv, w_kv, at_w_q, at_w_o, rel_bias, g_final):
    d = D_MODEL
    bp, tp = x_prompt.shape[:2]
    nb = x_sample.shape[0]
    p = _prepare_params(w_ada, b_ada, g_norm, rw_mu, rw_w_rkv, rw_w0, rw_w1, rw_w2, rw_a0, rw_a1, rw_a2, rw_g1,
                        rw_g2, rw_k_k, rw_k_a, rw_r_k, rw_ln_w, rw_ln_b, rw_w_o, moe_w_rg, moe_b_rg, moe_w_re,
                        moe_b_re, moe_w_gu, moe_w_down, w_ada_kv, b_ada_kv, g_kv, w_kv, at_w_q, at_w_o, rel_bias,
                        g_final)

    n_c = bp + nb
    n_pad = -n_c % 8
    c_all = jnp.concatenate([c_prompt, c_sample, jnp.zeros((n_pad, d), F32)], axis=0)
    mod_l = [_ada_linear(c_all, w_ada[l], b_ada[l]) for l in range(2)]
    mod_kv = _ada_linear(c_all, w_ada_kv, b_ada_kv)

    def split(m, n, lo, hi, per_batch):
        parts = [m[lo:hi, d * j:d * (j + 1)] for j in range(n)]
        return [q[:, None, :] if per_batch else q[None] for q in parts]

    mods_p = dict(l0=split(mod_l[0], 6, 0, bp, True), l1=split(mod_l[1], 6, 0, bp, True),
                  kv=split(mod_kv, 2, 0, bp, True))
    mods_s = dict(l0=split(mod_l[0], 6, bp, n_c, False), l1=split(mod_l[1], 6, bp, n_c, False),
                  kv=split(mod_kv, 2, bp, n_c, False))

    y_p, wkv_p, shift_p, k_p, v_p = _trunk(x_prompt, mods_p, None, None, None, p, tm=256, tm_moe=1024)
    keep = min(BAND * max(DILATIONS), tp)
    k_prompt = k_p[:, tp - keep:].reshape(bp, keep, N_HEADS, HEAD)
    v_prompt = v_p[:, tp - keep:].reshape(bp, keep, N_HEADS, HEAD)

    xs = x_sample.reshape(1, nb, d)
    y_s, wkv_s, shift_s, k_s, v_s = _trunk(xs, mods_s, state_wkv[0], state_shift[0][None], (cache_k, cache_v), p,
                                           tm=nb, tm_moe=nb)
    return (y_p, y_s.reshape(nb, 1, d), wkv_p[None], shift_p[None], k_prompt, v_prompt,
            wkv_s[None], shift_s[None], k_s.reshape(nb, 1, N_HEADS, HEAD), v_s.reshape(nb, 1, N_HEADS, HEAD))
```

```python
import functools

import numpy as np
import jax
import jax.numpy as jnp
from jax import lax
from jax.experimental import pallas as pl
from jax.experimental.pallas import tpu as pltpu

F32 = jnp.float32
BF16 = jnp.bfloat16

D_MODEL = 1024
N_HEADS = 16
HEAD = 64
LANES = 128
N_PAIRS = D_MODEL // LANES
N_MIX = 6
GN_EPS = 64e-5
RMS_EPS = 1e-6
NEG_INF = -1e30
DILATIONS = (1, 4, 16)
BAND = 128
QB = 128
SCALE_B = HEAD ** -0.5
NUM_BUCKETS = 32
MAX_DISTANCE = 2048
N_GROUPS_E = 4
N_EXP_PER_GROUP = 8
N_EXPERTS = N_GROUPS_E * N_EXP_PER_GROUP
D_EXPERT = 512
ROUTER_LANE0 = N_GROUPS_E
SCAN_CHUNK = 64

VMEM_LIMIT = 56 * 1024 * 1024


def _cparams(sem):
    return pltpu.CompilerParams(dimension_semantics=sem, vmem_limit_bytes=VMEM_LIMIT)


def _dot(a, b):
    return jnp.dot(a.astype(BF16), b.astype(BF16), preferred_element_type=F32)


def _dot_nt(a, b):
    return lax.dot_general(a.astype(BF16), b.astype(BF16), (((1,), (1,)), ((), ())),
                           preferred_element_type=F32)


def _split3(a):
    hi = a.astype(BF16)
    r1 = a - hi.astype(F32)
    mid = r1.astype(BF16)
    lo = (r1 - mid.astype(F32)).astype(BF16)
    return hi, mid, lo


def _dot3(a, b01):
    hi, mid, lo = _split3(a)
    b = b01.astype(BF16)
    return (jnp.dot(hi, b, preferred_element_type=F32) + jnp.dot(mid, b, preferred_element_type=F32)
            + jnp.dot(lo, b, preferred_element_type=F32))


def _dot3_left(a01, b):
    hi, mid, lo = _split3(b)
    a = a01.astype(BF16)
    return (jnp.dot(a, hi, preferred_element_type=F32) + jnp.dot(a, mid, preferred_element_type=F32)
            + jnp.dot(a, lo, preferred_element_type=F32))


def _sigmoid(x):
    return 1.0 / (1.0 + jnp.exp(-x))


def _silu(x):
    return x * _sigmoid(x)


def _softplus(z):
    return jnp.maximum(z, 0.0) + jnp.log(1.0 + jnp.exp(-jnp.abs(z)))


def _norm_mod(x, g, sh, sc):
    ms = jnp.mean(x * x, axis=-1, keepdims=True)
    return (x * lax.rsqrt(ms + RMS_EPS) * g) * (1.0 + sc) + sh


def _head_sum_matrix():
    r = lax.broadcasted_iota(jnp.int32, (LANES, LANES), 0) >= HEAD
    c = lax.broadcasted_iota(jnp.int32, (LANES, LANES), 1) >= HEAD
    return jnp.where(r == c, 1.0, 0.0).astype(BF16)


def _head_sum(z, m128):
    parts = [_dot3(z[:, LANES * p:LANES * (p + 1)], m128) for p in range(N_PAIRS)]
    return jnp.concatenate(parts, axis=1)


def _row_spec(arr, tm):
    n = arr.shape[-1]
    if arr.shape[1] == 1:
        return pl.BlockSpec((1, 1, n), lambda b, i: (b, 0, 0))
    return pl.BlockSpec((1, tm, n), lambda b, i: (b, i, 0))


def _const_spec(arr):
    nd = arr.ndim
    return pl.BlockSpec(arr.shape, lambda b, i, _nd=nd: (0,) * _nd)


def _ada_kernel(c_ref, w_ref, b_ref, o_ref):
    o_ref[...] = _dot(_silu(c_ref[...]), w_ref[...]) + b_ref[...]


def _ada_linear(c_all, w, b):
    m, k = c_all.shape
    n = w.shape[1]
    tn = 1024
    return pl.pallas_call(
        _ada_kernel,
        out_shape=jax.ShapeDtypeStruct((m, n), F32),
        grid=(n // tn,),
        in_specs=[pl.BlockSpec((m, k), lambda j: (0, 0)),
                  pl.BlockSpec((k, tn), lambda j: (0, j)),
                  pl.BlockSpec((1, tn), lambda j: (0, j))],
        out_specs=pl.BlockSpec((m, tn), lambda j: (0, j)),
        compiler_params=_cparams(("parallel",)),
        name="ada_linear",
    )(c_all, w, b.reshape(1, n))


def _norm_mod_kernel(x_ref, g_ref, sh_ref, sc_ref, o_ref):
    o_ref[0] = _norm_mod(x_ref[0], g_ref[...], sh_ref[0], sc_ref[0])


def _norm_mod_call(x, g, sh, sc, tm):
    b, t, d = x.shape
    g2 = g.reshape(1, d)
    return pl.pallas_call(
        _norm_mod_kernel,
        out_shape=jax.ShapeDtypeStruct(x.shape, F32),
        grid=(b, t // tm),
        in_specs=[_row_spec(x, tm), _const_spec(g2), _row_spec(sh, tm), _row_spec(sc, tm)],
        out_specs=pl.BlockSpec((1, tm, d), lambda bb, i: (bb, i, 0)),
        compiler_params=_cparams(("parallel", "parallel")),
        name="norm_mod",
    )(x, g2, sh, sc)


def _final_norm_kernel(x_ref, g_ref, o_ref):
    x = x_ref[0]
    ms = jnp.mean(x * x, axis=-1, keepdims=True)
    o_ref[0] = x * lax.rsqrt(ms + RMS_EPS) * g_ref[...]


def _final_norm_call(x, g, tm):
    b, t, d = x.shape
    g2 = g.reshape(1, d)
    return pl.pallas_call(
        _final_norm_kernel,
        out_shape=jax.ShapeDtypeStruct(x.shape, F32),
        grid=(b, t // tm),
        in_specs=[_row_spec(x, tm), _const_spec(g2)],
        out_specs=pl.BlockSpec((1, tm, d), lambda bb, i: (bb, i, 0)),
        compiler_params=_cparams(("parallel", "parallel")),
        name="final_norm",
    )(x, g2)


def _linear_kernel(*refs, has_norm, has_resid):
    it = iter(refs)
    x_ref = next(it)
    w_ref = next(it)
    if has_norm:
        g_ref, sh_ref, sc_ref = next(it), next(it), next(it)
    if has_resid:
        xr_ref, gt_ref = next(it), next(it)
    o_ref = next(it)
    x = x_ref[0]
    if has_norm:
        x = _norm_mod(x, g_ref[...], sh_ref[0], sc_ref[0])
    out = _dot(x, w_ref[...])
    if has_resid:
        out = xr_ref[0] + gt_ref[0] * out
    o_ref[0] = out


def _linear_call(x, w, tm, norm=None, resid=None, name="linear"):
    b, t, k = x.shape
    n = w.shape[1]
    args = [x, w]
    specs = [_row_spec(x, tm), _const_spec(w)]
    if norm is not None:
        g, sh, sc = norm
        g2 = g.reshape(1, k)
        args += [g2, sh, sc]
        specs += [_const_spec(g2), _row_spec(sh, tm), _row_spec(sc, tm)]
    if resid is not None:
        xr, gt = resid
        args += [xr, gt]
        specs += [_row_spec(xr, tm), _row_spec(gt, tm)]
    return pl.pallas_call(
        functools.partial(_linear_kernel, has_norm=norm is not None, has_resid=resid is not None),
        out_shape=jax.ShapeDtypeStruct((b, t, n), F32),
        grid=(b, t // tm),
        in_specs=specs,
        out_specs=pl.BlockSpec((1, tm, n), lambda bb, i: (bb, i, 0)),
        compiler_params=_cparams(("parallel", "parallel")),
        name=name,
    )(*args)


def _rwkv_proj_kernel(h_ref, hp_ref, mu_ref, vec_ref, wrkv_ref, w1_ref, w2_ref, a1_ref, a2_ref, g1_ref, g2_ref,
                      r_ref, lw_ref, k_ref, v_ref, a_ref, b_ref, g_ref, bonus_ref):
    h = h_ref[0]
    xx = hp_ref[0] - h

    def mix(i):
        return h + xx * mu_ref[i:i + 1, :]

    w0, a0, k_k, k_a, r_k = (vec_ref[i:i + 1, :] for i in range(5))
    m128 = _head_sum_matrix()
    r = _dot(mix(0), wrkv_ref[0])
    k = _dot(mix(1), wrkv_ref[1])
    v = _dot(mix(2), wrkv_ref[2])
    wl = w0 + _dot(jnp.tanh(_dot(mix(3), w1_ref[...])), w2_ref[...])
    w_log = -_softplus(-wl) - 0.5
    a_sig = _sigmoid(a0 + _dot(_dot(mix(4), a1_ref[...]), a2_ref[...]))
    g = _dot(_sigmoid(_dot(mix(5), g1_ref[...])), g2_ref[...])
    kk = k * k_k
    kk = kk / jnp.maximum(jnp.sqrt(_head_sum(kk * kk, m128)), 1e-12)
    k2 = k * (1.0 + (a_sig - 1.0) * k_a)
    r_ref[0] = r
    lw_ref[0] = -jnp.exp(w_log)
    k_ref[0] = k2
    v_ref[0] = v
    a_ref[0] = -kk
    b_ref[0] = kk * a_sig
    g_ref[0] = g
    bonus_ref[0] = _head_sum(r * k2 * r_k, m128) * v


def _rwkv_proj_call(h, hp, mu, vecs, wrkv, w1, w2, a1, a2, g1, g2, tm):
    b, t, d = h.shape
    consts = [mu, vecs, wrkv, w1, w2, a1, a2, g1, g2]
    out = jax.ShapeDtypeStruct((b, t, d), F32)
    ospec = pl.BlockSpec((1, tm, d), lambda bb, i: (bb, i, 0))
    return pl.pallas_call(
        _rwkv_proj_kernel,
        out_shape=[out] * 8,
        grid=(b, t // tm),
        in_specs=[_row_spec(h, tm), _row_spec(hp, tm)] + [_const_spec(c) for c in consts],
        out_specs=[ospec] * 8,
        compiler_params=_cparams(("parallel", "parallel")),
        name="rwkv_proj",
    )(h, hp, *consts)


def _rwkv_scan_kernel(r_ref, lw_ref, k_ref, v_ref, a_ref, b_ref, y_ref, hout_ref, h_scr):
    c = pl.program_id(1)
    nc = pl.num_programs(1)
    C = SCAN_CHUNK

    @pl.when(c == 0)
    def _():
        h_scr[...] = jnp.zeros_like(h_scr)

    lw = lw_ref[0]
    ri = lax.broadcasted_iota(jnp.int32, (C, C), 0)
    ci = lax.broadcasted_iota(jnp.int32, (C, C), 1)
    tril = jnp.where(ri >= ci, 1.0, 0.0).astype(BF16)
    cum = _dot3_left(tril, lw)
    gam = jnp.exp(cum)
    ginv = jnp.exp(-cum)
    g_last = gam[C - 1:C, :]
    rt = r_ref[0] * gam
    at = a_ref[0] * jnp.exp(cum - lw)
    bt = b_ref[0] * ginv
    kt = k_ref[0] * ginv
    bh = bt * g_last
    kh = kt * g_last
    v = v_ref[0]

    head0 = lax.broadcasted_iota(jnp.int32, (C, LANES), 1) < HEAD
    tt = lax.broadcasted_iota(jnp.int32, (2 * C, 2 * C), 0) & (C - 1)
    ss = lax.broadcasted_iota(jnp.int32, (2 * C, 2 * C), 1) & (C - 1)
    strict = tt > ss
    incl = tt >= ss
    n_double = C.bit_length() - 1

    ys = []
    for p in range(N_PAIRS):
        sl = slice(LANES * p, LANES * (p + 1))

        def two(x):
            xs = x[:, sl]
            return jnp.concatenate([jnp.where(head0, xs, 0.0), jnp.where(head0, 0.0, xs)], axis=0)

        a2, r2, b2, k2, v2, bh2, kh2 = (two(z) for z in (at, rt, bt, kt, v, bh, kh))
        ar = jnp.concatenate([a2, r2], axis=0).astype(BF16)
        bk = jnp.concatenate([b2, k2], axis=0).astype(BF16)
        s = _dot_nt(ar, bk)
        n2 = 2 * C
        l_ab = jnp.where(strict, s[:n2, :n2], 0.0)
        l_ak = jnp.where(strict, s[:n2, n2:], 0.0)
        t_rb = jnp.where(incl, s[n2:, :n2], 0.0)
        t_rk = jnp.where(incl, s[n2:, n2:], 0.0)
        hp = h_scr[p]
        arh = _dot(ar, hp)
        lv = _dot(jnp.concatenate([l_ak, t_rk], axis=0), v2)
        u = arh[:n2] + lv[:n2]
        pw = l_ab
        for it in range(n_double):
            u = u + _dot(pw, u)
            if it + 1 < n_double:
                pw = _dot(pw, pw)
        y2 = arh[n2:] + lv[n2:] + _dot(t_rb, u)
        ys.append(y2[:C] + y2[C:])
        gcol = jnp.transpose(jnp.broadcast_to(g_last[:, sl], (LANES, LANES)))
        w2t = jnp.transpose(jnp.concatenate([bh2, kh2], axis=0))
        z2 = jnp.concatenate([u, v2], axis=0)
        h_scr[p] = gcol * hp + _dot(w2t, z2)
    y_ref[0] = jnp.concatenate(ys, axis=1)

    @pl.when(c == nc - 1)
    def _():
        hout_ref[0] = h_scr[...]


def _rwkv_scan_call(r, lw, k, v, a, bb):
    b, t, d = r.shape
    C = SCAN_CHUNK
    spec = pl.BlockSpec((1, C, d), lambda bi, c: (bi, c, 0))
    y, hout = pl.pallas_call(
        _rwkv_scan_kernel,
        out_shape=[jax.ShapeDtypeStruct((b, t, d), F32),
                   jax.ShapeDtypeStruct((b, N_PAIRS, LANES, LANES), F32)],
        grid=(b, t // C),
        in_specs=[spec] * 6,
        out_specs=[spec, pl.BlockSpec((1, N_PAIRS, LANES, LANES), lambda bi, c: (bi, 0, 0, 0))],
        scratch_shapes=[pltpu.VMEM((N_PAIRS, LANES, LANES), F32)],
        compiler_params=_cparams(("parallel", "arbitrary")),
        name="rwkv_scan",
    )(r, lw, k, v, a, bb)
    hh = hout.reshape(b, N_PAIRS, 2, HEAD, 2, HEAD)
    hh = jnp.stack([hh[:, :, 0, :, 0, :], hh[:, :, 1, :, 1, :]], axis=2).reshape(b, N_HEADS, HEAD, HEAD)
    return y, jnp.swapaxes(hh, -1, -2)


def _rwkv_step_kernel(s_ref, z_ref, repj_ref, repi_ref, segi_ref, so_ref, y_ref):
    s = s_ref[...]
    z = z_ref[...]
    repj, repi, segi = repj_ref[...], repi_ref[...], segi_ref[...]
    r, lw, k, v, a, b = (z[:, HEAD * i:HEAD * (i + 1)] for i in range(6))
    at = _dot3(a, repj)
    sa = _dot3(s * at, segi)
    sn = (s * jnp.exp(_dot3(lw, repj)) + _dot3(sa, repi) * _dot3(b, repj)
          + _dot3(v, repi) * _dot3(k, repj))
    so_ref[...] = sn
    y_ref[...] = _dot3(sn * _dot3(r, repj), segi)


def _rwkv_step_call(state, r, lw, k, v, a, bb):
    nb = state.shape[0]
    rows = nb * N_HEADS
    flat = HEAD * HEAD
    s2 = state.reshape(rows, flat)
    z = jnp.concatenate([q.reshape(rows, HEAD) for q in (r, lw, k, v, a, bb)], axis=1)
    lane = np.arange(flat)
    repj = jnp.asarray(np.arange(HEAD)[:, None] == (lane % HEAD)[None, :], BF16)
    repi = jnp.asarray(np.arange(HEAD)[:, None] == (lane // HEAD)[None, :], BF16)
    segi = jnp.asarray((lane // HEAD)[:, None] == np.arange(HEAD)[None, :], BF16)
    tmr = 128
    so, y = pl.pallas_call(
        _rwkv_step_kernel,
        out_shape=[jax.ShapeDtypeStruct((rows, flat), F32), jax.ShapeDtypeStruct((rows, HEAD), F32)],
        grid=(rows // tmr,),
        in_specs=[pl.BlockSpec((tmr, flat), lambda i: (i, 0)),
                  pl.BlockSpec((tmr, 6 * HEAD), lambda i: (i, 0)),
                  pl.BlockSpec((HEAD, flat), lambda i: (0, 0)),
                  pl.BlockSpec((HEAD, flat), lambda i: (0, 0)),
                  pl.BlockSpec((flat, HEAD), lambda i: (0, 0))],
        out_specs=[pl.BlockSpec((tmr, flat), lambda i: (i, 0)), pl.BlockSpec((tmr, HEAD), lambda i: (i, 0))],
        compiler_params=_cparams(("parallel",)),
        name="rwkv_step",
    )(s2, z, repj, repi, segi)
    return y.reshape(1, nb, D_MODEL), so.reshape(nb, N_HEADS, HEAD, HEAD)


def _rwkv_out_kernel(y_ref, bonus_ref, g_ref, x_ref, gt_ref, ln_ref, wo_ref, o_ref):
    y = y_ref[0]
    m128 = _head_sum_matrix()
    mean = _head_sum(y, m128) * (1.0 / HEAD)
    dlt = y - mean
    var = _head_sum(dlt * dlt, m128) * (1.0 / HEAD)
    yn = dlt * lax.rsqrt(var + GN_EPS) * ln_ref[0:1, :] + ln_ref[1:2, :]
    out = _dot((yn + bonus_ref[0]) * g_ref[0], wo_ref[...])
    o_ref[0] = x_ref[0] + gt_ref[0] * out


def _rwkv_out_call(y, bonus, g, x, gt, ln, wo, tm):
    b, t, d = y.shape
    return pl.pallas_call(
        _rwkv_out_kernel,
        out_shape=jax.ShapeDtypeStruct((b, t, d), F32),
        grid=(b, t // tm),
        in_specs=[_row_spec(y, tm), _row_spec(bonus, tm), _row_spec(g, tm), _row_spec(x, tm), _row_spec(gt, tm),
                  _const_spec(ln), _const_spec(wo)],
        out_specs=pl.BlockSpec((1, tm, d), lambda bb, i: (bb, i, 0)),
        compiler_params=_cparams(("parallel", "parallel")),
        name="rwkv_out",
    )(y, bonus, g, x, gt, ln, wo)


ROUTE_E, ROUTE_W, ROUTE_RANK = 0, 2, 4


def _router_kernel(x_ref, g_ref, sh_ref, sc_ref, wr_ref, br_ref, hf_ref, route_ref, cnt_ref, carry):
    first = jnp.logical_and(pl.program_id(0) == 0, pl.program_id(1) == 0)

    @pl.when(first)
    def _():
        carry[...] = jnp.zeros_like(carry)

    hf = _norm_mod(x_ref[0], g_ref[...], sh_ref[0], sc_ref[0])
    hf_ref[0] = hf
    logits = jnp.dot(hf, wr_ref[...], preferred_element_type=F32, precision=lax.Precision.HIGHEST) + br_ref[...]
    lane = lax.broadcasted_iota(jnp.int32, logits.shape, 1).astype(F32)
    big = float(1 << 20)
    gl = jnp.where(lane < N_GROUPS_E, logits, NEG_INF)
    gmax = jnp.max(gl, axis=-1, keepdims=True)
    p_top = 1.0 / jnp.sum(jnp.exp(gl - gmax), axis=-1, keepdims=True)
    g_sel = jnp.min(jnp.where(gl == gmax, lane, big), axis=-1, keepdims=True)
    lo = ROUTER_LANE0 + N_EXP_PER_GROUP * g_sel
    el = jnp.where((lane >= lo) & (lane < lo + N_EXP_PER_GROUP), logits, NEG_INF)
    v1 = jnp.max(el, axis=-1, keepdims=True)
    i1 = jnp.min(jnp.where(el == v1, lane, big), axis=-1, keepdims=True)
    el2 = jnp.where(lane == i1, NEG_INF, el)
    v2 = jnp.max(el2, axis=-1, keepdims=True)
    i2 = jnp.min(jnp.where(el2 == v2, lane, big), axis=-1, keepdims=True)
    e21 = jnp.exp(v2 - v1)
    w1 = p_top / (1.0 + e21)
    w2 = p_top * e21 / (1.0 + e21)
    tm = logits.shape[0]
    onehot = jnp.where(lane == i1, 1.0, 0.0) + jnp.where(lane == i2, 1.0, 0.0)
    rr = lax.broadcasted_iota(jnp.int32, (tm, tm), 0)
    cc = lax.broadcasted_iota(jnp.int32, (tm, tm), 1)
    earlier = jnp.where(rr > cc, 1.0, 0.0).astype(BF16)
    before = jnp.dot(earlier, onehot.astype(BF16), preferred_element_type=F32) + carry[0:1, :]
    rank1 = jnp.sum(jnp.where(lane == i1, before, 0.0), axis=-1, keepdims=True)
    rank2 = jnp.sum(jnp.where(lane == i2, before, 0.0), axis=-1, keepdims=True)
    carry[...] = carry[...] + jnp.sum(onehot, axis=0, keepdims=True)
    cnt_ref[...] = carry[...]
    route = jnp.zeros_like(logits)
    for ln, val in ((ROUTE_E, i1 - ROUTER_LANE0), (ROUTE_E + 1, i2 - ROUTER_LANE0), (ROUTE_W, w1), (ROUTE_W + 1, w2),
                    (ROUTE_RANK, rank1), (ROUTE_RANK + 1, rank2)):
        route = jnp.where(lane == ln, val, route)
    route_ref[0] = route


def _router_call(x, g, sh, sc, wr, br, tm):
    b, t, d = x.shape
    g2 = g.reshape(1, d)
    return pl.pallas_call(
        _router_kernel,
        out_shape=[jax.ShapeDtypeStruct((b, t, d), F32), jax.ShapeDtypeStruct((b, t, LANES), F32),
                   jax.ShapeDtypeStruct((8, LANES), F32)],
        grid=(b, t // tm),
        in_specs=[_row_spec(x, tm), _const_spec(g2), _row_spec(sh, tm), _row_spec(sc, tm),
                  _const_spec(wr), _const_spec(br)],
        out_specs=[pl.BlockSpec((1, tm, d), lambda bb, i: (bb, i, 0)),
                   pl.BlockSpec((1, tm, LANES), lambda bb, i: (bb, i, 0)),
                   pl.BlockSpec((8, LANES), lambda bb, i: (0, 0))],
        scratch_shapes=[pltpu.VMEM((8, LANES), F32)],
        compiler_params=_cparams(("arbitrary", "arbitrary")),
        name="moe_router",
    )(x, g2, sh, sc, wr, br)


def _route_plan(route, counts, tile):
    n = route.shape[0] * route.shape[1]
    r2 = route.reshape(n, LANES)
    e = r2[:, ROUTE_E:ROUTE_E + 2].astype(jnp.int32)
    rank = r2[:, ROUTE_RANK:ROUTE_RANK + 2].astype(jnp.int32)
    cnt = counts[0, ROUTER_LANE0:ROUTER_LANE0 + N_EXPERTS].astype(jnp.int32)
    nt = (cnt + tile - 1) // tile
    tend = jnp.cumsum(nt)
    tstart = tend - nt
    onehot = e[:, :, None] == jnp.arange(N_EXPERTS, dtype=jnp.int32)[None, None, :]
    dest = jnp.sum(jnp.where(onehot, tstart[None, None, :], 0), axis=-1) * tile + rank
    n_tiles = (2 * n) // tile + N_EXPERTS
    tid = jnp.arange(n_tiles, dtype=jnp.int32)
    te = jnp.minimum(jnp.sum((tid[:, None] >= tend[None, :]).astype(jnp.int32), axis=1), N_EXPERTS - 1)
    active = (tid < tend[-1]).astype(jnp.int32)
    return dest.reshape(-1), te, active, n_tiles


def _dispatch_kernel(dest_ref, hf_ref, xs_in_ref, xs_ref, sem, *, tm):
    del xs_in_ref
    base = pl.program_id(0) * (2 * tm)

    def body(t, carry):
        row = hf_ref.at[pl.ds(t, 1), :]
        for c in range(2):
            dst = dest_ref[base + 2 * t + c]
            pltpu.make_async_copy(row, xs_ref.at[pl.ds(dst, 1), :], sem).start()
        return carry

    lax.fori_loop(0, tm, body, 0, unroll=8)
    done = xs_ref.at[pl.ds(0, 2 * tm), :]
    pltpu.make_async_copy(done, done, sem).wait()


def _dispatch_call(hf2, dest, n_rows, tm):
    n, d = hf2.shape
    xs0 = jnp.zeros((n_rows, d), F32)
    return pl.pallas_call(
        functools.partial(_dispatch_kernel, tm=tm),
        out_shape=jax.ShapeDtypeStruct((n_rows, d), F32),
        grid_spec=pltpu.PrefetchScalarGridSpec(
            num_scalar_prefetch=1,
            grid=(n // tm,),
            in_specs=[pl.BlockSpec((tm, d), lambda i, dest: (i, 0)),
                      pl.BlockSpec(memory_space=pl.ANY)],
            out_specs=pl.BlockSpec(memory_space=pl.ANY),
            scratch_shapes=[pltpu.SemaphoreType.DMA(())],
        ),
        input_output_aliases={2: 0},
        compiler_params=pltpu.CompilerParams(dimension_semantics=("arbitrary",), vmem_limit_bytes=VMEM_LIMIT,
                                             has_side_effects=True),
        name="moe_dispatch",
    )(dest, hf2, xs0)


def _expert_kernel(te_ref, act_ref, xs_ref, wgu_ref, wd_ref, o_ref, wgu_lp, wd_lp):
    i = pl.program_id(0)
    changed = jnp.logical_or(i == 0, te_ref[i] != te_ref[jnp.maximum(i - 1, 0)])

    @pl.when(changed)
    def _():
        wgu_lp[...] = wgu_ref[0].astype(wgu_lp.dtype)
        wd_lp[...] = wd_ref[0].astype(wd_lp.dtype)

    @pl.when(act_ref[i] == 1)
    def _():
        gu = _dot(xs_ref[...], wgu_lp[...])
        act = _silu(gu[:, :D_EXPERT]) * gu[:, D_EXPERT:]
        o_ref[...] = _dot(act, wd_lp[...])

    @pl.when(act_ref[i] == 0)
    def _():
        o_ref[...] = jnp.zeros_like(o_ref)


def _expert_call(xs, te, active, wgu, wd, tile):
    n_rows, d = xs.shape
    return pl.pallas_call(
        _expert_kernel,
        out_shape=jax.ShapeDtypeStruct((n_rows, d), F32),
        grid_spec=pltpu.PrefetchScalarGridSpec(
            num_scalar_prefetch=2,
            grid=(n_rows // tile,),
            in_specs=[pl.BlockSpec((tile, d), lambda i, te, act: (i, 0)),
                      pl.BlockSpec((1, d, 2 * D_EXPERT), lambda i, te, act: (te[i], 0, 0)),
                      pl.BlockSpec((1, D_EXPERT, d), lambda i, te, act: (te[i], 0, 0))],
            out_specs=pl.BlockSpec((tile, d), lambda i, te, act: (i, 0)),
            scratch_shapes=[pltpu.VMEM((d, 2 * D_EXPERT), BF16), pltpu.VMEM((D_EXPERT, d), BF16)],
        ),
        compiler_params=_cparams(("arbitrary",)),
        name="moe_experts",
    )(te, active, xs, wgu, wd)


def _combine_kernel(dest_ref, ys_ref, route_ref, x_ref, gt_ref, o_ref, buf, sem, *, tm):
    step = pl.program_id(0) * pl.num_programs(1) + pl.program_id(1)
    base = step * (2 * tm)

    def body(t, carry):
        for c in range(2):
            src = dest_ref[base + 2 * t + c]
            pltpu.make_async_copy(ys_ref.at[pl.ds(src, 1), :], buf.at[c, pl.ds(t, 1), :], sem).start()
        return carry

    lax.fori_loop(0, tm, body, 0, unroll=8)
    pltpu.make_async_copy(buf, buf, sem).wait()
    route = route_ref[0]
    lane = lax.broadcasted_iota(jnp.int32, route.shape, 1)
    w1 = jnp.sum(jnp.where(lane == ROUTE_W, route, 0.0), axis=-1, keepdims=True)
    w2 = jnp.sum(jnp.where(lane == ROUTE_W + 1, route, 0.0), axis=-1, keepdims=True)
    o_ref[0] = x_ref[0] + gt_ref[0] * (w1 * buf[0] + w2 * buf[1])


def _combine_call(ys, dest, route, x, gt, tm):
    b, t, d = x.shape

    def rs(arr):
        n = arr.shape[-1]
        if arr.shape[1] == 1:
            return pl.BlockSpec((1, 1, n), lambda bb, i, dest: (bb, 0, 0))
        return pl.BlockSpec((1, tm, n), lambda bb, i, dest: (bb, i, 0))

    return pl.pallas_call(
        functools.partial(_combine_kernel, tm=tm),
        out_shape=jax.ShapeDtypeStruct((b, t, d), F32),
        grid_spec=pltpu.PrefetchScalarGridSpec(
            num_scalar_prefetch=1,
            grid=(b, t // tm),
            in_specs=[pl.BlockSpec(memory_space=pl.ANY), rs(route), rs(x), rs(gt)],
            out_specs=pl.BlockSpec((1, tm, d), lambda bb, i, dest: (bb, i, 0)),
            scratch_shapes=[pltpu.VMEM((2, tm, d), F32), pltpu.SemaphoreType.DMA(())],
        ),
        compiler_params=_cparams(("arbitrary", "arbitrary")),
        name="moe_combine",
    )(dest, ys, route, x, gt)


def _moe_call(x, g, sh, sc, gt, wr, br, wgu, wd, tm, tile):
    b, t, d = x.shape
    hf, route, counts = _router_call(x, g, sh, sc, wr, br, tm)
    dest, te, active, n_tiles = _route_plan(route, counts, tile)
    xs = _dispatch_call(hf.reshape(b * t, d), dest, n_tiles * tile, tm)
    ys = _expert_call(xs, te, active, wgu, wd, tile)
    return _combine_call(ys, dest, route, x, gt, tm)


def _t5_bucket(dist):
    dist = np.asarray(dist)
    max_exact = NUM_BUCKETS // 2
    log_ratio = np.log(np.maximum(dist, 1) / max_exact) / np.log(MAX_DISTANCE / max_exact)
    large = np.minimum(max_exact + (log_ratio * (NUM_BUCKETS - max_exact)).astype(np.int32), NUM_BUCKETS - 1)
    return np.where(dist < max_exact, dist, large).astype(np.int32)


def _attn_prompt_kernel(q_ref, kp_ref, kc_ref, vp_ref, vc_ref, bias_ref, o_ref, lse_ref):
    i = pl.program_id(2)
    q = q_ref[0].astype(BF16)
    kcat = jnp.concatenate([kp_ref[0], kc_ref[0]], axis=0).astype(BF16)
    vcat = jnp.concatenate([vp_ref[0], vc_ref[0]], axis=0).astype(BF16)
    head0 = lax.broadcasted_iota(jnp.int32, (QB, LANES), 1) < HEAD
    col = lax.broadcasted_iota(jnp.int32, (QB, 2 * QB), 1)
    key_ok = col >= jnp.where(i > 0, 0, QB)
    lane = lax.broadcasted_iota(jnp.int32, (QB, LANES), 1)
    lse_tile = jnp.zeros((QB, LANES), F32)
    outs = []
    zero = jnp.zeros((), BF16)
    for p in range(N_PAIRS):
        sl = slice(LANES * p, LANES * (p + 1))
        qp, kp, vp = q[:, sl], kcat[:, sl], vcat[:, sl]
        o_pair = []
        for hh in range(2):
            h = 2 * p + hh
            qm = jnp.where(head0, qp, zero) if hh == 0 else jnp.where(head0, zero, qp)
            logits = _dot_nt(qm, kp) * SCALE_B + bias_ref[h]
            logits = jnp.where(key_ok, logits, NEG_INF)
            m = jnp.max(logits, axis=-1, keepdims=True)
            pr = jnp.exp(logits - m)
            ssum = jnp.sum(pr, axis=-1, keepdims=True)
            o_pair.append(_dot(pr, vp) / ssum)
            lse_tile = lse_tile + jnp.where(lane == h, m + jnp.log(ssum), 0.0)
        outs.append(jnp.where(head0, o_pair[0], o_pair[1]))
    o_ref[0] = jnp.concatenate(outs, axis=1)
    lse_ref[0] = lse_tile


def _attn_prompt_call(q, k, v, bias, gi, dil):
    b, t, _ = k.shape
    L = t // dil
    nblk = L // QB
    d = D_MODEL
    qv = q.reshape(b, L, dil * 3 * d)
    kv = k.reshape(b, L, dil * d)
    vv = v.reshape(b, L, dil * d)
    cur = pl.BlockSpec((1, QB, d), lambda bb, r, i: (bb, i, r))
    prev = pl.BlockSpec((1, QB, d), lambda bb, r, i: (bb, jnp.maximum(i - 1, 0), r))
    o, lse = pl.pallas_call(
        _attn_prompt_kernel,
        out_shape=[jax.ShapeDtypeStruct((b, L, dil * d), F32), jax.ShapeDtypeStruct((b, L, dil * LANES), F32)],
        grid=(b, dil, nblk),
        in_specs=[pl.BlockSpec((1, QB, d), lambda bb, r, i: (bb, i, r * 3 + gi)),
                  prev, cur, prev, cur,
                  pl.BlockSpec(bias.shape, lambda bb, r, i: (0, 0, 0))],
        out_specs=[cur, pl.BlockSpec((1, QB, LANES), lambda bb, r, i: (bb, i, r))],
        compiler_params=_cparams(("parallel", "parallel", "parallel")),
        name=f"attn_prompt_d{dil}",
    )(qv, kv, kv, vv, vv, bias)
    return o.reshape(b, t, d), lse.reshape(b, t, LANES)


def _attn_merge_kernel(o0_ref, o1_ref, o2_ref, l0_ref, l1_ref, l2_ref, e_ref, o_ref):
    l0, l1, l2 = l0_ref[0], l1_ref[0], l2_ref[0]
    mx = jnp.maximum(jnp.maximum(l0, l1), l2)
    e0, e1, e2 = jnp.exp(l0 - mx), jnp.exp(l1 - mx), jnp.exp(l2 - mx)
    inv = 1.0 / (e0 + e1 + e2)
    em = e_ref[...]
    o_ref[0] = (_dot3(e0 * inv, em) * o0_ref[0] + _dot3(e1 * inv, em) * o1_ref[0]
                + _dot3(e2 * inv, em) * o2_ref[0])


def _head_expand_matrix():
    return jnp.asarray(np.arange(LANES)[:, None] == (np.arange(D_MODEL) // HEAD)[None, :], BF16)


def _attn_merge_call(os_, ls_, tm):
    b, t, d = os_[0].shape
    em = _head_expand_matrix()
    args = list(os_) + list(ls_) + [em]
    return pl.pallas_call(
        _attn_merge_kernel,
        out_shape=jax.ShapeDtypeStruct((b, t, d), F32),
        grid=(b, t // tm),
        in_specs=[_row_spec(a, tm) for a in args[:-1]] + [_const_spec(em)],
        out_specs=pl.BlockSpec((1, tm, d), lambda bb, i: (bb, i, 0)),
        compiler_params=_cparams(("parallel", "parallel")),
        name="attn_merge",
    )(*args)


def _attn_sample_kernel(q_ref, kn_ref, vn_ref, k1_ref, k4_ref, k16_ref, v1_ref, v4_ref, v16_ref,
                        bk_ref, b0_ref, o_ref):
    ones = jnp.ones((HEAD, LANES), BF16)
    kn = kn_ref[0, 0]
    vn = vn_ref[0, 0]
    k_refs = (k1_ref, k4_ref, k16_ref)
    v_refs = (v1_ref, v4_ref, v16_ref)
    ms, ss, nums = [], [], []
    for g in range(len(DILATIONS)):
        qg = q_ref[0, g] * SCALE_B
        kb = k_refs[g][0]
        lk = _dot((kb * qg[None]).reshape(BAND * N_HEADS, HEAD), ones).reshape(BAND, N_HEADS, LANES) + bk_ref[g]
        l0 = _dot(kn * qg, ones) + b0_ref[g]
        m = jnp.maximum(jnp.max(lk, axis=0), l0)
        pk = jnp.exp(lk - m[None])
        p0 = jnp.exp(l0 - m)
        ss.append(jnp.sum(pk, axis=0) + p0)
        nums.append(jnp.sum(pk[:, :, :HEAD] * v_refs[g][0], axis=0) + p0[:, :HEAD] * vn)
        ms.append(m)
    mx = jnp.maximum(jnp.maximum(ms[0], ms[1]), ms[2])
    cs = [jnp.exp(m - mx) for m in ms]
    inv = 1.0 / (cs[0] * ss[0] + cs[1] * ss[1] + cs[2] * ss[2])
    out = jnp.zeros((N_HEADS, HEAD), F32)
    for g in range(len(DILATIONS)):
        out = out + (cs[g] * inv)[:, :HEAD] * nums[g]
    o_ref[0, 0] = out


def _attn_sample_call(q, kn, vn, cache_k, cache_v, bias_k, bias_0):
    nb, n_buf = cache_k.shape[:2]
    assert n_buf == BAND * max(DILATIONS)

    def view(cache, dil):
        return cache.reshape(nb, n_buf // dil, dil, N_HEADS, HEAD)

    def cspec(dil):
        blk = n_buf // dil // BAND - 1
        return pl.BlockSpec((1, BAND, None, N_HEADS, HEAD), lambda bb, _blk=blk: (bb, _blk, 0, 0, 0))

    one = lambda n: pl.BlockSpec((1, n, N_HEADS, HEAD), lambda bb: (bb, 0, 0, 0))
    full = lambda a: pl.BlockSpec(a.shape, lambda bb, _n=a.ndim: (0,) * _n)
    return pl.pallas_call(
        _attn_sample_kernel,
        out_shape=jax.ShapeDtypeStruct((nb, 1, N_HEADS, HEAD), F32),
        grid=(nb,),
        in_specs=[one(3), one(1), one(1)] + [cspec(dl) for dl in DILATIONS] * 2 + [full(bias_k), full(bias_0)],
        out_specs=one(1),
        compiler_params=_cparams(("parallel",)),
        name="attn_sample",
    )(q, kn, vn, *[view(cache_k, dl) for dl in DILATIONS], *[view(cache_v, dl) for dl in DILATIONS], bias_k, bias_0)


def _prompt_bias(rel_bias, gi, dil):
    table = rel_bias[:, gi * N_HEADS:(gi + 1) * N_HEADS].astype(F32)
    per_dist = table[_t5_bucket(np.arange(BAND + 1) * dil)]
    delta = (lax.broadcasted_iota(jnp.int32, (QB, 2 * QB), 0) + QB
             - lax.broadcasted_iota(jnp.int32, (QB, 2 * QB), 1))
    onehot = (delta[:, :, None] == jnp.arange(BAND + 1, dtype=jnp.int32)[None, None, :]).astype(F32)
    bias = jnp.einsum("qkj,jh->hqk", onehot, per_dist, precision=lax.Precision.HIGHEST)
    in_band = (delta >= 0) & (delta <= BAND)
    return jnp.where(in_band[None], bias, NEG_INF)


def _sample_bias(rel_bias):
    bks, b0s = [], []
    for gi, dil in enumerate(DILATIONS):
        table = rel_bias[:, gi * N_HEADS:(gi + 1) * N_HEADS].astype(F32)
        tb = table[_t5_bucket(dil * np.arange(BAND + 1))]
        bks.append(jnp.broadcast_to(tb[1:][::-1][:, :, None], (BAND, N_HEADS, LANES)))
        b0s.append(jnp.broadcast_to(tb[0][:, None], (N_HEADS, LANES)))
    return jnp.stack(bks), jnp.stack(b0s)


def _trunk(x, mods, wkv0, shift0, caches, p, tm, tm_moe):
    b, t, d = x.shape
    m0, m1, mkv = mods["l0"], mods["l1"], mods["kv"]

    sh_m, sc_m, gt_m, sh_c, sc_c, gt_c = m0
    h = _norm_mod_call(x, p["g_norm"][0, 0], sh_m, sc_m, tm)
    if shift0 is None:
        hp = jnp.concatenate([jnp.zeros((b, 1, d), F32), h[:, :-1]], axis=1)
    else:
        hp = shift0
    r, lw, k, v, a, bb, g, bonus = _rwkv_proj_call(h, hp, p["rw_mu"], p["rw_vecs"], p["rw_w_rkv"], p["rw_w1"],
                                                   p["rw_w2"], p["rw_a1"], p["rw_a2"], p["rw_g1"], p["rw_g2"],
                                                   min(tm, 128))
    if wkv0 is None:
        y, wkv = _rwkv_scan_call(r, lw, k, v, a, bb)
        shift = h[:, -1]
    else:
        y, wkv = _rwkv_step_call(wkv0, r, lw, k, v, a, bb)
        shift = h[0]
    x = _rwkv_out_call(y, bonus, g, x, gt_m, p["rw_ln"], p["rw_w_o"], tm)
    x = _moe_call(x, p["g_norm"][0, 1], sh_c, sc_c, gt_c, p["moe_wr"][0], p["moe_br"][0],
                  p["moe_w_gu"][0], p["moe_w_down"][0], tm, tm_moe)

    sh_k, sc_k = mkv
    kvp = _linear_call(x, p["w_kv"], tm, norm=(p["g_kv"], sh_k, sc_k), name="kv_proj")
    k_sh, v_sh = kvp[..., :d], kvp[..., d:]

    sh_m, sc_m, gt_m, sh_c, sc_c, gt_c = m1
    q = _linear_call(x, p["at_w_q"], tm, norm=(p["g_norm"][1, 0], sh_m, sc_m), name="q_proj")
    if caches is None:
        os_, ls_ = [], []
        for gi, dil in enumerate(DILATIONS):
            o, lse = _attn_prompt_call(q, k_sh, v_sh, _prompt_bias(p["rel_bias"], gi, dil), gi, dil)
            os_.append(o)
            ls_.append(lse)
        o = _attn_merge_call(os_, ls_, tm)
    else:
        cache_k, cache_v = caches
        bias_k, bias_0 = _sample_bias(p["rel_bias"])
        nb = cache_k.shape[0]
        o = _attn_sample_call(q.reshape(nb, len(DILATIONS), N_HEADS, HEAD), k_sh.reshape(nb, 1, N_HEADS, HEAD),
                              v_sh.reshape(nb, 1, N_HEADS, HEAD), cache_k, cache_v, bias_k, bias_0).reshape(b, t, d)
    x = _linear_call(o, p["at_w_o"], tm, resid=(x, gt_m), name="attn_out")
    x = _moe_call(x, p["g_norm"][1, 1], sh_c, sc_c, gt_c, p["moe_wr"][1], p["moe_br"][1],
                  p["moe_w_gu"][1], p["moe_w_down"][1], tm, tm_moe)
    y = _final_norm_call(x, p["g_final"], tm)
    return y, wkv, shift, k_sh, v_sh


def _prepare_params(w_ada, b_ada, g_norm, rw_mu, rw_w_rkv, rw_w0, rw_w1, rw_w2, rw_a0, rw_a1, rw_a2, rw_g1, rw_g2,
                    rw_k_k, rw_k_a, rw_r_k, rw_ln_w, rw_ln_b, rw_w_o, moe_w_rg, moe_b_rg, moe_w_re, moe_b_re,
                    moe_w_gu, moe_w_down, w_ada_kv, b_ada_kv, g_kv, w_kv, at_w_q, at_w_o, rel_bias, g_final):
    d = D_MODEL
    zeros = jnp.zeros((3, d), F32)
    pad = LANES - N_GROUPS_E - N_EXPERTS
    depth = moe_w_rg.shape[0]
    return dict(
        g_norm=g_norm, g_kv=g_kv, g_final=g_final, rel_bias=rel_bias,
        rw_mu=rw_mu[0],
        rw_vecs=jnp.concatenate([rw_w0[0][None], rw_a0[0][None], rw_k_k[0][None], rw_k_a[0][None],
                                 rw_r_k[0].reshape(1, d), zeros], axis=0),
        rw_w_rkv=rw_w_rkv[0].astype(BF16), rw_w1=rw_w1[0], rw_w2=rw_w2[0], rw_a1=rw_a1[0], rw_a2=rw_a2[0],
        rw_g1=rw_g1[0], rw_g2=rw_g2[0],
        rw_ln=jnp.concatenate([rw_ln_w[0][None], rw_ln_b[0][None], zeros, zeros], axis=0),
        rw_w_o=rw_w_o[0].astype(BF16),
        moe_wr=jnp.pad(jnp.concatenate([moe_w_rg, moe_w_re], axis=-1), ((0, 0), (0, 0), (0, pad))),
        moe_br=jnp.pad(jnp.concatenate([moe_b_rg, moe_b_re], axis=-1), ((0, 0), (0, pad)))[:, None, :],
        moe_w_gu=moe_w_gu.reshape(depth, N_EXPERTS, d, 2 * D_EXPERT),
        moe_w_down=moe_w_down.reshape(depth, N_EXPERTS, D_EXPERT, d),
        w_kv=w_kv.astype(BF16), at_w_q=at_w_q[0].astype(BF16), at_w_o=at_w_o[0].astype(BF16),
    )


def kernel(x_prompt, x_sample, state_wkv, state_shift, cache_k, cache_v, c_prompt, c_sample, w_ada, b_ada, g_norm, rw_mu, rw_w_rkv, rw_w0, rw_w1, rw_w2, rw_a0, rw_a1, rw_a2, rw_g1, rw_g2, rw_k_k, rw_k_a, rw_r_k, rw_ln_w, rw_ln_b, rw_w_o, moe_w_rg, moe_b_rg, moe_w_re, moe_b_re, moe_w_gu, moe_w_down, w_ada_kv, b_ada_kv, g_kv, w_kv, at_w_q, at_w_o, rel_bias, g_final):
    d = D_MODEL
    bp, tp = x_prompt.shape[:2]
    nb = x_sample.shape[0]
    p = _prepare_params(w_ada, b_ada, g_norm, rw_mu, rw_w_rkv, rw_w0, rw_w1, rw_w2, rw_a0, rw_a1, rw_a2, rw_g1,
                        rw_g2, rw_k_k, rw_k_a, rw_r_k, rw_ln_w, rw_ln_b, rw_w_o, moe_w_rg, moe_b_rg, moe_w_re,
                        moe_b_re, moe_w_gu, moe_w_down, w_ada_kv, b_ada_kv, g_kv, w_kv, at_w_q, at_w_o, rel_bias,
                        g_final)

    n_c = bp + nb
    n_pad = -n_c % 8
    c_all = jnp.concatenate([c_prompt, c_sample, jnp.zeros((n_pad, d), F32)], axis=0)
    mod_l = [_ada_linear(c_all, w_ada[l], b_ada[l]) for l in range(2)]
    mod_kv = _ada_linear(c_all, w_ada_kv, b_ada_kv)

    def split(m, n, lo, hi, per_batch):
        parts = [m[lo:hi, d * j:d * (j + 1)] for j in range(n)]
        return [q[:, None, :] if per_batch else q[None] for q in parts]

    mods_p = dict(l0=split(mod_l[0], 6, 0, bp, True), l1=split(mod_l[1], 6, 0, bp, True),
                  kv=split(mod_kv, 2, 0, bp, True))
    mods_s = dict(l0=split(mod_l[0], 6, bp, n_c, False), l1=split(mod_l[1], 6, bp, n_c, False),
                  kv=split(mod_kv, 2, bp, n_c, False))

    y_p, wkv_p, shift_p, k_p, v_p = _trunk(x_prompt, mods_p, None, None, None, p, tm=256, tm_moe=256)
    keep = min(BAND * max(DILATIONS), tp)
    k_prompt = k_p[:, tp - keep:].reshape(bp, keep, N_HEADS, HEAD)
    v_prompt = v_p[:, tp - keep:].reshape(bp, keep, N_HEADS, HEAD)

    xs = x_sample.reshape(1, nb, d)
    y_s, wkv_s, shift_s, k_s, v_s = _trunk(xs, mods_s, state_wkv[0], state_shift[0][None], (cache_k, cache_v), p,
                                           tm=nb, tm_moe=nb)
    return (y_p, y_s.reshape(nb, 1, d), wkv_p[None], shift_p[None], k_prompt, v_prompt,
            wkv_s[None], shift_s[None], k_s.reshape(nb, 1, N_HEADS, HEAD), v_s.reshape(nb, 1, N_HEADS, HEAD))
```

```python
import functools

import numpy as np
import jax
import jax.numpy as jnp
from jax import lax
from jax.experimental import pallas as pl
from jax.experimental.pallas import tpu as pltpu

F32 = jnp.float32
BF16 = jnp.bfloat16

D_MODEL = 1024
N_HEADS = 16
HEAD = 64
LANES = 128
N_PAIRS = D_MODEL // LANES
N_MIX = 6
GN_EPS = 64e-5
RMS_EPS = 1e-6
NEG_INF = -1e30
DILATIONS = (1, 4, 16)
BAND = 128
QB = 128
SCALE_B = HEAD ** -0.5
NUM_BUCKETS = 32
MAX_DISTANCE = 2048
N_GROUPS_E = 4
N_EXP_PER_GROUP = 8
N_EXPERTS = N_GROUPS_E * N_EXP_PER_GROUP
D_EXPERT = 512
ROUTER_LANE0 = N_GROUPS_E
SCAN_CHUNK = 64

VMEM_LIMIT = 56 * 1024 * 1024


def _cparams(sem):
    return pltpu.CompilerParams(dimension_semantics=sem, vmem_limit_bytes=VMEM_LIMIT)


def _dot(a, b):
    return jnp.dot(a.astype(BF16), b.astype(BF16), preferred_element_type=F32)


def _dot_nt(a, b):
    return lax.dot_general(a.astype(BF16), b.astype(BF16), (((1,), (1,)), ((), ())),
                           preferred_element_type=F32)


def _split3(a):
    hi = a.astype(BF16)
    r1 = a - hi.astype(F32)
    mid = r1.astype(BF16)
    lo = (r1 - mid.astype(F32)).astype(BF16)
    return hi, mid, lo


def _dot3(a, b01):
    hi, mid, lo = _split3(a)
    b = b01.astype(BF16)
    return (jnp.dot(hi, b, preferred_element_type=F32) + jnp.dot(mid, b, preferred_element_type=F32)
            + jnp.dot(lo, b, preferred_element_type=F32))


def _dot3_left(a01, b):
    hi, mid, lo = _split3(b)
    a = a01.astype(BF16)
    return (jnp.dot(a, hi, preferred_element_type=F32) + jnp.dot(a, mid, preferred_element_type=F32)
            + jnp.dot(a, lo, preferred_element_type=F32))


def _sigmoid(x):
    return 1.0 / (1.0 + jnp.exp(-x))


def _silu(x):
    return x * _sigmoid(x)


def _softplus(z):
    return jnp.maximum(z, 0.0) + jnp.log(1.0 + jnp.exp(-jnp.abs(z)))


def _norm_mod(x, g, sh, sc):
    ms = jnp.mean(x * x, axis=-1, keepdims=True)
    return (x * lax.rsqrt(ms + RMS_EPS) * g) * (1.0 + sc) + sh


def _head_sum_matrix():
    r = lax.broadcasted_iota(jnp.int32, (LANES, LANES), 0) >= HEAD
    c = lax.broadcasted_iota(jnp.int32, (LANES, LANES), 1) >= HEAD
    return jnp.where(r == c, 1.0, 0.0).astype(BF16)


def _head_sum(z, m128):
    parts = [_dot3(z[:, LANES * p:LANES * (p + 1)], m128) for p in range(N_PAIRS)]
    return jnp.concatenate(parts, axis=1)


def _row_spec(arr, tm):
    n = arr.shape[-1]
    if arr.shape[1] == 1:
        return pl.BlockSpec((1, 1, n), lambda b, i: (b, 0, 0))
    return pl.BlockSpec((1, tm, n), lambda b, i: (b, i, 0))


def _const_spec(arr):
    nd = arr.ndim
    return pl.BlockSpec(arr.shape, lambda b, i, _nd=nd: (0,) * _nd)


def _ada_kernel(c_ref, w_ref, b_ref, o_ref):
    o_ref[...] = _dot(_silu(c_ref[...]), w_ref[...]) + b_ref[...]


def _ada_linear(c_all, w, b):
    m, k = c_all.shape
    n = w.shape[1]
    tn = 1024
    return pl.pallas_call(
        _ada_kernel,
        out_shape=jax.ShapeDtypeStruct((m, n), F32),
        grid=(n // tn,),
        in_specs=[pl.BlockSpec((m, k), lambda j: (0, 0)),
                  pl.BlockSpec((k, tn), lambda j: (0, j)),
                  pl.BlockSpec((1, tn), lambda j: (0, j))],
        out_specs=pl.BlockSpec((m, tn), lambda j: (0, j)),
        compiler_params=_cparams(("parallel",)),
        name="ada_linear",
    )(c_all, w, b.reshape(1, n))


def _norm_mod_kernel(x_ref, g_ref, sh_ref, sc_ref, o_ref):
    o_ref[0] = _norm_mod(x_ref[0], g_ref[...], sh_ref[0], sc_ref[0])


def _norm_mod_call(x, g, sh, sc, tm):
    b, t, d = x.shape
    g2 = g.reshape(1, d)
    return pl.pallas_call(
        _norm_mod_kernel,
        out_shape=jax.ShapeDtypeStruct(x.shape, F32),
        grid=(b, t // tm),
        in_specs=[_row_spec(x, tm), _const_spec(g2), _row_spec(sh, tm), _row_spec(sc, tm)],
        out_specs=pl.BlockSpec((1, tm, d), lambda bb, i: (bb, i, 0)),
        compiler_params=_cparams(("parallel", "parallel")),
        name="norm_mod",
    )(x, g2, sh, sc)


def _final_norm_kernel(x_ref, g_ref, o_ref):
    x = x_ref[0]
    ms = jnp.mean(x * x, axis=-1, keepdims=True)
    o_ref[0] = x * lax.rsqrt(ms + RMS_EPS) * g_ref[...]


def _final_norm_call(x, g, tm):
    b, t, d = x.shape
    g2 = g.reshape(1, d)
    return pl.pallas_call(
        _final_norm_kernel,
        out_shape=jax.ShapeDtypeStruct(x.shape, F32),
        grid=(b, t // tm),
        in_specs=[_row_spec(x, tm), _const_spec(g2)],
        out_specs=pl.BlockSpec((1, tm, d), lambda bb, i: (bb, i, 0)),
        compiler_params=_cparams(("parallel", "parallel")),
        name="final_norm",
    )(x, g2)


def _linear_kernel(*refs, has_norm, has_resid):
    it = iter(refs)
    x_ref = next(it)
    w_ref = next(it)
    if has_norm:
        g_ref, sh_ref, sc_ref = next(it), next(it), next(it)
    if has_resid:
        xr_ref, gt_ref = next(it), next(it)
    o_ref = next(it)
    x = x_ref[0]
    if has_norm:
        x = _norm_mod(x, g_ref[...], sh_ref[0], sc_ref[0])
    out = _dot(x, w_ref[...])
    if has_resid:
        out = xr_ref[0] + gt_ref[0] * out
    o_ref[0] = out


def _linear_call(x, w, tm, norm=None, resid=None, name="linear"):
    b, t, k = x.shape
    n = w.shape[1]
    args = [x, w]
    specs = [_row_spec(x, tm), _const_spec(w)]
    if norm is not None:
        g, sh, sc = norm
        g2 = g.reshape(1, k)
        args += [g2, sh, sc]
        specs += [_const_spec(g2), _row_spec(sh, tm), _row_spec(sc, tm)]
    if resid is not None:
        xr, gt = resid
        args += [xr, gt]
        specs += [_row_spec(xr, tm), _row_spec(gt, tm)]
    return pl.pallas_call(
        functools.partial(_linear_kernel, has_norm=norm is not None, has_resid=resid is not None),
        out_shape=jax.ShapeDtypeStruct((b, t, n), F32),
        grid=(b, t // tm),
        in_specs=specs,
        out_specs=pl.BlockSpec((1, tm, n), lambda bb, i: (bb, i, 0)),
        compiler_params=_cparams(("parallel", "parallel")),
        name=name,
    )(*args)


def _rwkv_proj_kernel(h_ref, hp_ref, mu_ref, vec_ref, wrkv_ref, w1_ref, w2_ref, a1_ref, a2_ref, g1_ref, g2_ref,
                      r_ref, lw_ref, k_ref, v_ref, a_ref, b_ref, g_ref, bonus_ref):
    h = h_ref[0]
    xx = hp_ref[0] - h

    def mix(i):
        return h + xx * mu_ref[i:i + 1, :]

    w0, a0, k_k, k_a, r_k = (vec_ref[i:i + 1, :] for i in range(5))
    m128 = _head_sum_matrix()
    r = _dot(mix(0), wrkv_ref[0])
    k = _dot(mix(1), wrkv_ref[1])
    v = _dot(mix(2), wrkv_ref[2])
    wl = w0 + _dot(jnp.tanh(_dot(mix(3), w1_ref[...])), w2_ref[...])
    w_log = -_softplus(-wl) - 0.5
    a_sig = _sigmoid(a0 + _dot(_dot(mix(4), a1_ref[...]), a2_ref[...]))
    g = _dot(_sigmoid(_dot(mix(5), g1_ref[...])), g2_ref[...])
    kk = k * k_k
    kk = kk / jnp.maximum(jnp.sqrt(_head_sum(kk * kk, m128)), 1e-12)
    k2 = k * (1.0 + (a_sig - 1.0) * k_a)
    r_ref[0] = r
    lw_ref[0] = -jnp.exp(w_log)
    k_ref[0] = k2
    v_ref[0] = v
    a_ref[0] = -kk
    b_ref[0] = kk * a_sig
    g_ref[0] = g
    bonus_ref[0] = _head_sum(r * k2 * r_k, m128) * v


def _rwkv_proj_call(h, hp, mu, vecs, wrkv, w1, w2, a1, a2, g1, g2, tm):
    b, t, d = h.shape
    consts = [mu, vecs, wrkv, w1, w2, a1, a2, g1, g2]
    out = jax.ShapeDtypeStruct((b, t, d), F32)
    ospec = pl.BlockSpec((1, tm, d), lambda bb, i: (bb, i, 0))
    return pl.pallas_call(
        _rwkv_proj_kernel,
        out_shape=[out] * 8,
        grid=(b, t // tm),
        in_specs=[_row_spec(h, tm), _row_spec(hp, tm)] + [_const_spec(c) for c in consts],
        out_specs=[ospec] * 8,
        compiler_params=_cparams(("parallel", "parallel")),
        name="rwkv_proj",
    )(h, hp, *consts)


def _rwkv_scan_kernel(r_ref, lw_ref, k_ref, v_ref, a_ref, b_ref, y_ref, hout_ref, h_scr):
    c = pl.program_id(0)
    nc = pl.num_programs(0)
    nb = r_ref.shape[0]
    C = SCAN_CHUNK
    n2 = 2 * C

    @pl.when(c == 0)
    def _():
        h_scr[...] = jnp.zeros_like(h_scr)

    ri = lax.broadcasted_iota(jnp.int32, (C, C), 0)
    ci = lax.broadcasted_iota(jnp.int32, (C, C), 1)
    tril = jnp.where(ri >= ci, 1.0, 0.0).astype(BF16)
    head0 = lax.broadcasted_iota(jnp.int32, (C, LANES), 1) < HEAD
    tt = lax.broadcasted_iota(jnp.int32, (n2, n2), 0) & (C - 1)
    ss = lax.broadcasted_iota(jnp.int32, (n2, n2), 1) & (C - 1)
    strict = tt > ss
    incl = tt >= ss
    n_double = C.bit_length() - 1

    def two(xs):
        return jnp.concatenate([jnp.where(head0, xs, 0.0), jnp.where(head0, 0.0, xs)], axis=0)

    ar, bk, v2, w2t, gcol = [], [], [], [], []
    for bi in range(nb):
        lw = lw_ref[bi]
        cum = _dot3_left(tril, lw)
        gam = jnp.exp(cum)
        ginv = jnp.exp(-cum)
        g_last = gam[C - 1:C, :]
        rt = r_ref[bi] * gam
        at = a_ref[bi] * jnp.exp(cum - lw)
        bt = b_ref[bi] * ginv
        kt = k_ref[bi] * ginv
        bh = bt * g_last
        kh = kt * g_last
        v = v_ref[bi]
        for p in range(N_PAIRS):
            sl = slice(LANES * p, LANES * (p + 1))
            ar.append(jnp.concatenate([two(at[:, sl]), two(rt[:, sl])], axis=0).astype(BF16))
            bk.append(jnp.concatenate([two(bt[:, sl]), two(kt[:, sl])], axis=0).astype(BF16))
            v2.append(two(v[:, sl]).astype(BF16))
            w2t.append(jnp.transpose(jnp.concatenate([two(bh[:, sl]), two(kh[:, sl])], axis=0)).astype(BF16))
            gcol.append(jnp.transpose(jnp.broadcast_to(g_last[:, sl], (LANES, LANES))))
    n = len(ar)
    hp = [h_scr[i] for i in range(n)]
    s = [_dot_nt(ar[i], bk[i]) for i in range(n)]
    arh = [_dot(ar[i], hp[i]) for i in range(n)]
    pw = [jnp.where(strict, s[i][:n2, :n2], 0.0).astype(BF16) for i in range(n)]
    lt = [jnp.concatenate([jnp.where(strict, s[i][:n2, n2:], 0.0), jnp.where(incl, s[i][n2:, n2:], 0.0)],
                          axis=0) for i in range(n)]
    t_rb = [jnp.where(incl, s[i][n2:, :n2], 0.0) for i in range(n)]
    lv = [_dot(lt[i], v2[i]) for i in range(n)]
    u = [arh[i][:n2] + lv[i][:n2] for i in range(n)]
    for it in range(n_double):
        pu = [_dot(pw[i], u[i]) for i in range(n)]
        if it + 1 < n_double:
            pw = [_dot(pw[i], pw[i]).astype(BF16) for i in range(n)]
        u = [u[i] + pu[i] for i in range(n)]
    tu = [_dot(t_rb[i], u[i]) for i in range(n)]
    hn = [_dot(w2t[i], jnp.concatenate([u[i].astype(BF16), v2[i]], axis=0)) for i in range(n)]
    for i in range(n):
        h_scr[i] = gcol[i] * hp[i] + hn[i]
    for bi in range(nb):
        ys = []
        for p in range(N_PAIRS):
            i = bi * N_PAIRS + p
            y2 = arh[i][n2:] + lv[i][n2:] + tu[i]
            ys.append(y2[:C] + y2[C:])
        y_ref[bi] = jnp.concatenate(ys, axis=1)

    @pl.when(c == nc - 1)
    def _():
        hout_ref[...] = h_scr[...]


def _rwkv_scan_call(r, lw, k, v, a, bb):
    b, t, d = r.shape
    C = SCAN_CHUNK
    spec = pl.BlockSpec((b, C, d), lambda c: (0, c, 0))
    y, hout = pl.pallas_call(
        _rwkv_scan_kernel,
        out_shape=[jax.ShapeDtypeStruct((b, t, d), F32),
                   jax.ShapeDtypeStruct((b * N_PAIRS, LANES, LANES), F32)],
        grid=(t // C,),
        in_specs=[spec] * 6,
        out_specs=[spec, pl.BlockSpec((b * N_PAIRS, LANES, LANES), lambda c: (0, 0, 0))],
        scratch_shapes=[pltpu.VMEM((b * N_PAIRS, LANES, LANES), F32)],
        compiler_params=_cparams(("arbitrary",)),
        name="rwkv_scan",
    )(r, lw, k, v, a, bb)
    hh = hout.reshape(b, N_PAIRS, 2, HEAD, 2, HEAD)
    hh = jnp.stack([hh[:, :, 0, :, 0, :], hh[:, :, 1, :, 1, :]], axis=2).reshape(b, N_HEADS, HEAD, HEAD)
    return y, jnp.swapaxes(hh, -1, -2)


def _rwkv_step_kernel(s_ref, z_ref, so_ref, y_ref):
    r, lw, k, v, a, b = (z_ref[c, 0] for c in range(6))
    w = jnp.exp(lw)
    ys = []
    for i in range(HEAD):
        s = s_ref[0, i]
        sa = jnp.sum(s * a, axis=0, keepdims=True)
        sn = s * w + sa * b + v[i:i + 1, :] * k
        so_ref[0, i] = sn
        ys.append(jnp.sum(sn * r, axis=0, keepdims=True))
    y_ref[0] = jnp.concatenate(ys, axis=0)


def _rwkv_step_call(state, r, lw, k, v, a, bb):
    nb = state.shape[0]
    st = jnp.transpose(state, (1, 2, 3, 0))
    z = jnp.stack([jnp.transpose(q[0]).reshape(N_HEADS, HEAD, nb) for q in (r, lw, k, v, a, bb)])
    so, y = pl.pallas_call(
        _rwkv_step_kernel,
        out_shape=[jax.ShapeDtypeStruct((N_HEADS, HEAD, HEAD, nb), F32),
                   jax.ShapeDtypeStruct((N_HEADS, HEAD, nb), F32)],
        grid=(N_HEADS,),
        in_specs=[pl.BlockSpec((1, HEAD, HEAD, nb), lambda h: (h, 0, 0, 0)),
                  pl.BlockSpec((6, 1, HEAD, nb), lambda h: (0, h, 0, 0))],
        out_specs=[pl.BlockSpec((1, HEAD, HEAD, nb), lambda h: (h, 0, 0, 0)),
                   pl.BlockSpec((1, HEAD, nb), lambda h: (h, 0, 0))],
        compiler_params=_cparams(("parallel",)),
        name="rwkv_step",
    )(st, z)
    return jnp.transpose(y.reshape(D_MODEL, nb))[None], jnp.transpose(so, (3, 0, 1, 2))


def _rwkv_out_kernel(y_ref, bonus_ref, g_ref, x_ref, gt_ref, ln_ref, wo_ref, o_ref):
    y = y_ref[0]
    m128 = _head_sum_matrix()
    mean = _head_sum(y, m128) * (1.0 / HEAD)
    dlt = y - mean
    var = _head_sum(dlt * dlt, m128) * (1.0 / HEAD)
    yn = dlt * lax.rsqrt(var + GN_EPS) * ln_ref[0:1, :] + ln_ref[1:2, :]
    out = _dot((yn + bonus_ref[0]) * g_ref[0], wo_ref[...])
    o_ref[0] = x_ref[0] + gt_ref[0] * out


def _rwkv_out_call(y, bonus, g, x, gt, ln, wo, tm):
    b, t, d = y.shape
    return pl.pallas_call(
        _rwkv_out_kernel,
        out_shape=jax.ShapeDtypeStruct((b, t, d), F32),
        grid=(b, t // tm),
        in_specs=[_row_spec(y, tm), _row_spec(bonus, tm), _row_spec(g, tm), _row_spec(x, tm), _row_spec(gt, tm),
                  _const_spec(ln), _const_spec(wo)],
        out_specs=pl.BlockSpec((1, tm, d), lambda bb, i: (bb, i, 0)),
        compiler_params=_cparams(("parallel", "parallel")),
        name="rwkv_out",
    )(y, bonus, g, x, gt, ln, wo)


ROUTE_E, ROUTE_W, ROUTE_RANK = 0, 2, 4


def _router_kernel(x_ref, g_ref, sh_ref, sc_ref, wr_ref, br_ref, hf_ref, route_ref, cnt_ref, carry):
    first = jnp.logical_and(pl.program_id(0) == 0, pl.program_id(1) == 0)

    @pl.when(first)
    def _():
        carry[...] = jnp.zeros_like(carry)

    hf = _norm_mod(x_ref[0], g_ref[...], sh_ref[0], sc_ref[0])
    hf_ref[0] = hf
    logits = jnp.dot(hf, wr_ref[...], preferred_element_type=F32, precision=lax.Precision.HIGHEST) + br_ref[...]
    lane = lax.broadcasted_iota(jnp.int32, logits.shape, 1).astype(F32)
    big = float(1 << 20)
    gl = jnp.where(lane < N_GROUPS_E, logits, NEG_INF)
    gmax = jnp.max(gl, axis=-1, keepdims=True)
    p_top = 1.0 / jnp.sum(jnp.exp(gl - gmax), axis=-1, keepdims=True)
    g_sel = jnp.min(jnp.where(gl == gmax, lane, big), axis=-1, keepdims=True)
    lo = ROUTER_LANE0 + N_EXP_PER_GROUP * g_sel
    el = jnp.where((lane >= lo) & (lane < lo + N_EXP_PER_GROUP), logits, NEG_INF)
    v1 = jnp.max(el, axis=-1, keepdims=True)
    i1 = jnp.min(jnp.where(el == v1, lane, big), axis=-1, keepdims=True)
    el2 = jnp.where(lane == i1, NEG_INF, el)
    v2 = jnp.max(el2, axis=-1, keepdims=True)
    i2 = jnp.min(jnp.where(el2 == v2, lane, big), axis=-1, keepdims=True)
    e21 = jnp.exp(v2 - v1)
    w1 = p_top / (1.0 + e21)
    w2 = p_top * e21 / (1.0 + e21)
    tm = logits.shape[0]
    onehot = jnp.where(lane == i1, 1.0, 0.0) + jnp.where(lane == i2, 1.0, 0.0)
    rr = lax.broadcasted_iota(jnp.int32, (tm, tm), 0)
    cc = lax.broadcasted_iota(jnp.int32, (tm, tm), 1)
    earlier = jnp.where(rr > cc, 1.0, 0.0).astype(BF16)
    before = jnp.dot(earlier, onehot.astype(BF16), preferred_element_type=F32) + carry[0:1, :]
    rank1 = jnp.sum(jnp.where(lane == i1, before, 0.0), axis=-1, keepdims=True)
    rank2 = jnp.sum(jnp.where(lane == i2, before, 0.0), axis=-1, keepdims=True)
    carry[...] = carry[...] + jnp.sum(onehot, axis=0, keepdims=True)
    cnt_ref[...] = carry[...]
    route = jnp.zeros_like(logits)
    for ln, val in ((ROUTE_E, i1 - ROUTER_LANE0), (ROUTE_E + 1, i2 - ROUTER_LANE0), (ROUTE_W, w1), (ROUTE_W + 1, w2),
                    (ROUTE_RANK, rank1), (ROUTE_RANK + 1, rank2)):
        route = jnp.where(lane == ln, val, route)
    route_ref[0] = route


def _router_call(x, g, sh, sc, wr, br, tm):
    b, t, d = x.shape
    g2 = g.reshape(1, d)
    return pl.pallas_call(
        _router_kernel,
        out_shape=[jax.ShapeDtypeStruct((b, t, d), F32), jax.ShapeDtypeStruct((b, t, LANES), F32),
                   jax.ShapeDtypeStruct((8, LANES), F32)],
        grid=(b, t // tm),
        in_specs=[_row_spec(x, tm), _const_spec(g2), _row_spec(sh, tm), _row_spec(sc, tm),
                  _const_spec(wr), _const_spec(br)],
        out_specs=[pl.BlockSpec((1, tm, d), lambda bb, i: (bb, i, 0)),
                   pl.BlockSpec((1, tm, LANES), lambda bb, i: (bb, i, 0)),
                   pl.BlockSpec((8, LANES), lambda bb, i: (0, 0))],
        scratch_shapes=[pltpu.VMEM((8, LANES), F32)],
        compiler_params=_cparams(("arbitrary", "arbitrary")),
        name="moe_router",
    )(x, g2, sh, sc, wr, br)


def _route_plan(route, counts, tile):
    n = route.shape[0] * route.shape[1]
    r2 = route.reshape(n, LANES)
    e = r2[:, ROUTE_E:ROUTE_E + 2].astype(jnp.int32)
    rank = r2[:, ROUTE_RANK:ROUTE_RANK + 2].astype(jnp.int32)
    cnt = counts[0, ROUTER_LANE0:ROUTER_LANE0 + N_EXPERTS].astype(jnp.int32)
    nt = (cnt + tile - 1) // tile
    tend = jnp.cumsum(nt)
    tstart = tend - nt
    onehot = e[:, :, None] == jnp.arange(N_EXPERTS, dtype=jnp.int32)[None, None, :]
    dest = jnp.sum(jnp.where(onehot, tstart[None, None, :], 0), axis=-1) * tile + rank
    n_tiles = (2 * n) // tile + N_EXPERTS
    tid = jnp.arange(n_tiles, dtype=jnp.int32)
    te = jnp.minimum(jnp.sum((tid[:, None] >= tend[None, :]).astype(jnp.int32), axis=1), N_EXPERTS - 1)
    active = (tid < tend[-1]).astype(jnp.int32)
    return dest.reshape(-1), te, active, n_tiles


def _dispatch_kernel(dest_ref, hf_ref, xs_in_ref, xs_ref, sem, *, tm):
    del xs_in_ref
    base = pl.program_id(0) * (2 * tm)

    def body(t, carry):
        row = hf_ref.at[pl.ds(t, 1), :]
        for c in range(2):
            dst = dest_ref[base + 2 * t + c]
            pltpu.make_async_copy(row, xs_ref.at[pl.ds(dst, 1), :], sem).start()
        return carry

    lax.fori_loop(0, tm, body, 0, unroll=8)
    done = xs_ref.at[pl.ds(0, 2 * tm), :]
    pltpu.make_async_copy(done, done, sem).wait()


def _dispatch_call(hf2, dest, n_rows, tm):
    n, d = hf2.shape
    xs0 = jnp.zeros((n_rows, d), F32)
    return pl.pallas_call(
        functools.partial(_dispatch_kernel, tm=tm),
        out_shape=jax.ShapeDtypeStruct((n_rows, d), F32),
        grid_spec=pltpu.PrefetchScalarGridSpec(
            num_scalar_prefetch=1,
            grid=(n // tm,),
            in_specs=[pl.BlockSpec((tm, d), lambda i, dest: (i, 0)),
                      pl.BlockSpec(memory_space=pl.ANY)],
            out_specs=pl.BlockSpec(memory_space=pl.ANY),
            scratch_shapes=[pltpu.SemaphoreType.DMA(())],
        ),
        input_output_aliases={2: 0},
        compiler_params=pltpu.CompilerParams(dimension_semantics=("arbitrary",), vmem_limit_bytes=VMEM_LIMIT,
                                             has_side_effects=True),
        name="moe_dispatch",
    )(dest, hf2, xs0)


def _expert_kernel(te_ref, act_ref, xs_ref, wgu_ref, wd_ref, o_ref, wgu_lp, wd_lp):
    i = pl.program_id(0)
    changed = jnp.logical_or(i == 0, te_ref[i] != te_ref[jnp.maximum(i - 1, 0)])

    @pl.when(changed)
    def _():
        wgu_lp[...] = wgu_ref[0].astype(wgu_lp.dtype)
        wd_lp[...] = wd_ref[0].astype(wd_lp.dtype)

    @pl.when(act_ref[i] == 1)
    def _():
        gu = _dot(xs_ref[...], wgu_lp[...])
        act = _silu(gu[:, :D_EXPERT]) * gu[:, D_EXPERT:]
        o_ref[...] = _dot(act, wd_lp[...])

    @pl.when(act_ref[i] == 0)
    def _():
        o_ref[...] = jnp.zeros_like(o_ref)


def _expert_call(xs, te, active, wgu, wd, tile):
    n_rows, d = xs.shape
    return pl.pallas_call(
        _expert_kernel,
        out_shape=jax.ShapeDtypeStruct((n_rows, d), F32),
        grid_spec=pltpu.PrefetchScalarGridSpec(
            num_scalar_prefetch=2,
            grid=(n_rows // tile,),
            in_specs=[pl.BlockSpec((tile, d), lambda i, te, act: (i, 0)),
                      pl.BlockSpec((1, d, 2 * D_EXPERT), lambda i, te, act: (te[i], 0, 0)),
                      pl.BlockSpec((1, D_EXPERT, d), lambda i, te, act: (te[i], 0, 0))],
            out_specs=pl.BlockSpec((tile, d), lambda i, te, act: (i, 0)),
            scratch_shapes=[pltpu.VMEM((d, 2 * D_EXPERT), BF16), pltpu.VMEM((D_EXPERT, d), BF16)],
        ),
        compiler_params=_cparams(("arbitrary",)),
        name="moe_experts",
    )(te, active, xs, wgu, wd)


def _combine_kernel(dest_ref, ys_ref, route_ref, x_ref, gt_ref, o_ref, buf, sem, *, tm):
    step = pl.program_id(0) * pl.num_programs(1) + pl.program_id(1)
    base = step * (2 * tm)

    def body(t, carry):
        for c in range(2):
            src = dest_ref[base + 2 * t + c]
            pltpu.make_async_copy(ys_ref.at[pl.ds(src, 1), :], buf.at[c, pl.ds(t, 1), :], sem).start()
        return carry

    lax.fori_loop(0, tm, body, 0, unroll=8)
    pltpu.make_async_copy(buf, buf, sem).wait()
    route = route_ref[0]
    lane = lax.broadcasted_iota(jnp.int32, route.shape, 1)
    w1 = jnp.sum(jnp.where(lane == ROUTE_W, route, 0.0), axis=-1, keepdims=True)
    w2 = jnp.sum(jnp.where(lane == ROUTE_W + 1, route, 0.0), axis=-1, keepdims=True)
    o_ref[0] = x_ref[0] + gt_ref[0] * (w1 * buf[0] + w2 * buf[1])


def _combine_call(ys, dest, route, x, gt, tm):
    b, t, d = x.shape

    def rs(arr):
        n = arr.shape[-1]
        if arr.shape[1] == 1:
            return pl.BlockSpec((1, 1, n), lambda bb, i, dest: (bb, 0, 0))
        return pl.BlockSpec((1, tm, n), lambda bb, i, dest: (bb, i, 0))

    return pl.pallas_call(
        functools.partial(_combine_kernel, tm=tm),
        out_shape=jax.ShapeDtypeStruct((b, t, d), F32),
        grid_spec=pltpu.PrefetchScalarGridSpec(
            num_scalar_prefetch=1,
            grid=(b, t // tm),
            in_specs=[pl.BlockSpec(memory_space=pl.ANY), rs(route), rs(x), rs(gt)],
            out_specs=pl.BlockSpec((1, tm, d), lambda bb, i, dest: (bb, i, 0)),
            scratch_shapes=[pltpu.VMEM((2, tm, d), F32), pltpu.SemaphoreType.DMA(())],
        ),
        compiler_params=_cparams(("arbitrary", "arbitrary")),
        name="moe_combine",
    )(dest, ys, route, x, gt)


def _moe_call(x, g, sh, sc, gt, wr, br, wgu, wd, tm, tile):
    b, t, d = x.shape
    hf, route, counts = _router_call(x, g, sh, sc, wr, br, tm)
    dest, te, active, n_tiles = _route_plan(route, counts, tile)
    xs = _dispatch_call(hf.reshape(b * t, d), dest, n_tiles * tile, tm)
    ys = _expert_call(xs, te, active, wgu, wd, tile)
    return _combine_call(ys, dest, route, x, gt, tm)


def _t5_bucket(dist):
    dist = np.asarray(dist)
    max_exact = NUM_BUCKETS // 2
    log_ratio = np.log(np.maximum(dist, 1) / max_exact) / np.log(MAX_DISTANCE / max_exact)
    large = np.minimum(max_exact + (log_ratio * (NUM_BUCKETS - max_exact)).astype(np.int32), NUM_BUCKETS - 1)
    return np.where(dist < max_exact, dist, large).astype(np.int32)


def _attn_prompt_kernel(q_ref, kp_ref, kc_ref, vp_ref, vc_ref, bias_ref, o_ref, lse_ref):
    i = pl.program_id(2)
    q = q_ref[0].astype(BF16)
    kcat = jnp.concatenate([kp_ref[0], kc_ref[0]], axis=0).astype(BF16)
    vcat = jnp.concatenate([vp_ref[0], vc_ref[0]], axis=0).astype(BF16)
    head0 = lax.broadcasted_iota(jnp.int32, (QB, LANES), 1) < HEAD
    col = lax.broadcasted_iota(jnp.int32, (QB, 2 * QB), 1)
    key_ok = col >= jnp.where(i > 0, 0, QB)
    lane = lax.broadcasted_iota(jnp.int32, (QB, LANES), 1)
    lse_tile = jnp.zeros((QB, LANES), F32)
    outs = []
    zero = jnp.zeros((), BF16)
    for p in range(N_PAIRS):
        sl = slice(LANES * p, LANES * (p + 1))
        qp, kp, vp = q[:, sl], kcat[:, sl], vcat[:, sl]
        o_pair = []
        for hh in range(2):
            h = 2 * p + hh
            qm = jnp.where(head0, qp, zero) if hh == 0 else jnp.where(head0, zero, qp)
            logits = _dot_nt(qm, kp) * SCALE_B + bias_ref[h]
            logits = jnp.where(key_ok, logits, NEG_INF)
            m = jnp.max(logits, axis=-1, keepdims=True)
            pr = jnp.exp(logits - m)
            ssum = jnp.sum(pr, axis=-1, keepdims=True)
            o_pair.append(_dot(pr, vp) / ssum)
            lse_tile = lse_tile + jnp.where(lane == h, m + jnp.log(ssum), 0.0)
        outs.append(jnp.where(head0, o_pair[0], o_pair[1]))
    o_ref[0] = jnp.concatenate(outs, axis=1)
    lse_ref[0] = lse_tile


def _attn_prompt_call(q, k, v, bias, gi, dil):
    b, t, _ = k.shape
    L = t // dil
    nblk = L // QB
    d = D_MODEL
    qv = q.reshape(b, L, dil * 3 * d)
    kv = k.reshape(b, L, dil * d)
    vv = v.reshape(b, L, dil * d)
    cur = pl.BlockSpec((1, QB, d), lambda bb, r, i: (bb, i, r))
    prev = pl.BlockSpec((1, QB, d), lambda bb, r, i: (bb, jnp.maximum(i - 1, 0), r))
    o, lse = pl.pallas_call(
        _attn_prompt_kernel,
        out_shape=[jax.ShapeDtypeStruct((b, L, dil * d), F32), jax.ShapeDtypeStruct((b, L, dil * LANES), F32)],
        grid=(b, dil, nblk),
        in_specs=[pl.BlockSpec((1, QB, d), lambda bb, r, i: (bb, i, r * 3 + gi)),
                  prev, cur, prev, cur,
                  pl.BlockSpec(bias.shape, lambda bb, r, i: (0, 0, 0))],
        out_specs=[cur, pl.BlockSpec((1, QB, LANES), lambda bb, r, i: (bb, i, r))],
        compiler_params=_cparams(("parallel", "parallel", "parallel")),
        name=f"attn_prompt_d{dil}",
    )(qv, kv, kv, vv, vv, bias)
    return o.reshape(b, t, d), lse.reshape(b, t, LANES)


def _attn_merge_kernel(o0_ref, o1_ref, o2_ref, l0_ref, l1_ref, l2_ref, e_ref, o_ref):
    l0, l1, l2 = l0_ref[0], l1_ref[0], l2_ref[0]
    mx = jnp.maximum(jnp.maximum(l0, l1), l2)
    e0, e1, e2 = jnp.exp(l0 - mx), jnp.exp(l1 - mx), jnp.exp(l2 - mx)
    inv = 1.0 / (e0 + e1 + e2)
    em = e_ref[...]
    o_ref[0] = (_dot3(e0 * inv, em) * o0_ref[0] + _dot3(e1 * inv, em) * o1_ref[0]
                + _dot3(e2 * inv, em) * o2_ref[0])


def _head_expand_matrix():
    return jnp.asarray(np.arange(LANES)[:, None] == (np.arange(D_MODEL) // HEAD)[None, :], BF16)


def _attn_merge_call(os_, ls_, tm):
    b, t, d = os_[0].shape
    em = _head_expand_matrix()
    args = list(os_) + list(ls_) + [em]
    return pl.pallas_call(
        _attn_merge_kernel,
        out_shape=jax.ShapeDtypeStruct((b, t, d), F32),
        grid=(b, t // tm),
        in_specs=[_row_spec(a, tm) for a in args[:-1]] + [_const_spec(em)],
        out_specs=pl.BlockSpec((1, tm, d), lambda bb, i: (bb, i, 0)),
        compiler_params=_cparams(("parallel", "parallel")),
        name="attn_merge",
    )(*args)


SAMPLE_HEADS_PER_STEP = 8


def _attn_sample_kernel(q_ref, kn_ref, vn_ref, k_ref, v_ref, bias_ref, b0_ref, o_ref):
    n_buf = k_ref.shape[-1]
    hb = SAMPLE_HEADS_PER_STEP
    lane = lax.broadcasted_iota(jnp.int32, (HEAD, LANES), 1)
    o_tile = jnp.zeros((HEAD, LANES), F32)
    for hh in range(hb):
        kn = kn_ref[0, 0][:, hh:hh + 1]
        vn = vn_ref[0, 0][:, hh:hh + 1]
        ms, ss, nums = [], [], []
        for g, dil in enumerate(DILATIONS):
            lo = n_buf - BAND * dil
            q = q_ref[0, g, 0][:, hh:hh + 1] * SCALE_B
            lk = jnp.sum(k_ref[0, hh, :, lo:] * q, axis=0, keepdims=True) + bias_ref[g, hh:hh + 1, lo:]
            l0 = jnp.sum(kn * q, axis=0, keepdims=True) + b0_ref[g, 0][0:1, hh:hh + 1]
            m = jnp.maximum(jnp.max(lk, axis=1, keepdims=True), l0)
            pk = jnp.exp(lk - m)
            p0 = jnp.exp(l0 - m)
            ss.append(jnp.sum(pk, axis=1, keepdims=True) + p0)
            nums.append(jnp.sum(v_ref[0, hh, :, lo:] * pk, axis=1, keepdims=True) + p0 * vn)
            ms.append(m)
        mx = jnp.maximum(jnp.maximum(ms[0], ms[1]), ms[2])
        cs = [jnp.exp(m - mx) for m in ms]
        inv = 1.0 / (cs[0] * ss[0] + cs[1] * ss[1] + cs[2] * ss[2])
        o_col = (cs[0] * inv) * nums[0] + (cs[1] * inv) * nums[1] + (cs[2] * inv) * nums[2]
        o_tile = jnp.where(lane == hh, o_col, o_tile)
    o_ref[0, 0] = o_tile


def _attn_sample_call(q, kn, vn, cache_k, cache_v, bias_p, bias_0):
    nb, n_buf = cache_k.shape[:2]
    assert n_buf == BAND * max(DILATIONS)
    hb = SAMPLE_HEADS_PER_STEP
    nhb = N_HEADS // hb
    ng = len(DILATIONS)
    kt = jnp.transpose(cache_k, (0, 2, 3, 1))
    vt = jnp.transpose(cache_v, (0, 2, 3, 1))
    qt = jnp.transpose(q.reshape(nb, ng, nhb, hb, HEAD), (0, 1, 2, 4, 3))
    knt = jnp.transpose(kn.reshape(nb, nhb, hb, HEAD), (0, 1, 3, 2))
    vnt = jnp.transpose(vn.reshape(nb, nhb, hb, HEAD), (0, 1, 3, 2))
    o = pl.pallas_call(
        _attn_sample_kernel,
        out_shape=jax.ShapeDtypeStruct((nb, nhb, HEAD, LANES), F32),
        grid=(nb, nhb),
        in_specs=[pl.BlockSpec((1, ng, 1, HEAD, hb), lambda b, j: (b, 0, j, 0, 0)),
                  pl.BlockSpec((1, 1, HEAD, hb), lambda b, j: (b, j, 0, 0)),
                  pl.BlockSpec((1, 1, HEAD, hb), lambda b, j: (b, j, 0, 0)),
                  pl.BlockSpec((1, hb, HEAD, n_buf), lambda b, j: (b, j, 0, 0)),
                  pl.BlockSpec((1, hb, HEAD, n_buf), lambda b, j: (b, j, 0, 0)),
                  pl.BlockSpec((ng, hb, n_buf), lambda b, j: (0, j, 0)),
                  pl.BlockSpec((ng, 1, 8, hb), lambda b, j: (0, j, 0, 0))],
        out_specs=pl.BlockSpec((1, 1, HEAD, LANES), lambda b, j: (b, j, 0, 0)),
        compiler_params=_cparams(("parallel", "parallel")),
        name="attn_sample",
    )(qt, knt, vnt, kt, vt, bias_p, bias_0)
    return jnp.transpose(o[..., :hb], (0, 1, 3, 2)).reshape(nb, N_HEADS * HEAD)


def _prompt_bias(rel_bias, gi, dil):
    table = rel_bias[:, gi * N_HEADS:(gi + 1) * N_HEADS].astype(F32)
    per_dist = table[_t5_bucket(np.arange(BAND + 1) * dil)]
    delta = (lax.broadcasted_iota(jnp.int32, (QB, 2 * QB), 0) + QB
             - lax.broadcasted_iota(jnp.int32, (QB, 2 * QB), 1))
    onehot = (delta[:, :, None] == jnp.arange(BAND + 1, dtype=jnp.int32)[None, None, :]).astype(F32)
    bias = jnp.einsum("qkj,jh->hqk", onehot, per_dist, precision=lax.Precision.HIGHEST)
    in_band = (delta >= 0) & (delta <= BAND)
    return jnp.where(in_band[None], bias, NEG_INF)


def _sample_bias(rel_bias, n_buf):
    hb = SAMPLE_HEADS_PER_STEP
    dist = n_buf - np.arange(n_buf)
    bps, b0s = [], []
    for gi, dil in enumerate(DILATIONS):
        table = rel_bias[:, gi * N_HEADS:(gi + 1) * N_HEADS].astype(F32)
        in_window = (dist % dil == 0) & (dist <= BAND * dil)
        bps.append(jnp.where(in_window[None, :], jnp.transpose(table[_t5_bucket(dist)]), NEG_INF))
        b0s.append(jnp.broadcast_to(table[0].reshape(N_HEADS // hb, 1, hb), (N_HEADS // hb, 8, hb)))
    return jnp.stack(bps), jnp.stack(b0s)


def _trunk(x, mods, wkv0, shift0, caches, p, tm, tm_moe):
    b, t, d = x.shape
    m0, m1, mkv = mods["l0"], mods["l1"], mods["kv"]

    sh_m, sc_m, gt_m, sh_c, sc_c, gt_c = m0
    h = _norm_mod_call(x, p["g_norm"][0, 0], sh_m, sc_m, tm)
    if shift0 is None:
        hp = jnp.concatenate([jnp.zeros((b, 1, d), F32), h[:, :-1]], axis=1)
    else:
        hp = shift0
    r, lw, k, v, a, bb, g, bonus = _rwkv_proj_call(h, hp, p["rw_mu"], p["rw_vecs"], p["rw_w_rkv"], p["rw_w1"],
                                                   p["rw_w2"], p["rw_a1"], p["rw_a2"], p["rw_g1"], p["rw_g2"],
                                                   min(tm, 128))
    if wkv0 is None:
        y, wkv = _rwkv_scan_call(r, lw, k, v, a, bb)
        shift = h[:, -1]
    else:
        y, wkv = _rwkv_step_call(wkv0, r, lw, k, v, a, bb)
        shift = h[0]
    x = _rwkv_out_call(y, bonus, g, x, gt_m, p["rw_ln"], p["rw_w_o"], tm)
    x = _moe_call(x, p["g_norm"][0, 1], sh_c, sc_c, gt_c, p["moe_wr"][0], p["moe_br"][0],
                  p["moe_w_gu"][0], p["moe_w_down"][0], tm, tm_moe)

    sh_k, sc_k = mkv
    kvp = _linear_call(x, p["w_kv"], tm, norm=(p["g_kv"], sh_k, sc_k), name="kv_proj")
    k_sh, v_sh = kvp[..., :d], kvp[..., d:]

    sh_m, sc_m, gt_m, sh_c, sc_c, gt_c = m1
    q = _linear_call(x, p["at_w_q"], tm, norm=(p["g_norm"][1, 0], sh_m, sc_m), name="q_proj")
    if caches is None:
        os_, ls_ = [], []
        for gi, dil in enumerate(DILATIONS):
            o, lse = _attn_prompt_call(q, k_sh, v_sh, _prompt_bias(p["rel_bias"], gi, dil), gi, dil)
            os_.append(o)
            ls_.append(lse)
        o = _attn_merge_call(os_, ls_, tm)
    else:
        cache_k, cache_v = caches
        nb, n_buf = cache_k.shape[:2]
        bias_p, bias_0 = _sample_bias(p["rel_bias"], n_buf)
        o = _attn_sample_call(q.reshape(nb, len(DILATIONS), N_HEADS, HEAD), k_sh.reshape(nb, N_HEADS, HEAD),
                              v_sh.reshape(nb, N_HEADS, HEAD), cache_k, cache_v, bias_p, bias_0).reshape(b, t, d)
    x = _linear_call(o, p["at_w_o"], tm, resid=(x, gt_m), name="attn_out")
    x = _moe_call(x, p["g_norm"][1, 1], sh_c, sc_c, gt_c, p["moe_wr"][1], p["moe_br"][1],
                  p["moe_w_gu"][1], p["moe_w_down"][1], tm, tm_moe)
    y = _final_norm_call(x, p["g_final"], tm)
    return y, wkv, shift, k_sh, v_sh


def _prepare_params(w_ada, b_ada, g_norm, rw_mu, rw_w_rkv, rw_w0, rw_w1, rw_w2, rw_a0, rw_a1, rw_a2, rw_g1, rw_g2,
                    rw_k_k, rw_k_a, rw_r_k, rw_ln_w, rw_ln_b, rw_w_o, moe_w_rg, moe_b_rg, moe_w_re, moe_b_re,
                    moe_w_gu, moe_w_down, w_ada_kv, b_ada_kv, g_kv, w_kv, at_w_q, at_w_o, rel_bias, g_final):
    d = D_MODEL
    zeros = jnp.zeros((3, d), F32)
    pad = LANES - N_GROUPS_E - N_EXPERTS
    depth = moe_w_rg.shape[0]
    return dict(
        g_norm=g_norm, g_kv=g_kv, g_final=g_final, rel_bias=rel_bias,
        rw_mu=rw_mu[0],
        rw_vecs=jnp.concatenate([rw_w0[0][None], rw_a0[0][None], rw_k_k[0][None], rw_k_a[0][None],
                                 rw_r_k[0].reshape(1, d), zeros], axis=0),
        rw_w_rkv=rw_w_rkv[0].astype(BF16), rw_w1=rw_w1[0], rw_w2=rw_w2[0], rw_a1=rw_a1[0], rw_a2=rw_a2[0],
        rw_g1=rw_g1[0], rw_g2=rw_g2[0],
        rw_ln=jnp.concatenate([rw_ln_w[0][None], rw_ln_b[0][None], zeros, zeros], axis=0),
        rw_w_o=rw_w_o[0].astype(BF16),
        moe_wr=jnp.pad(jnp.concatenate([moe_w_rg, moe_w_re], axis=-1), ((0, 0), (0, 0), (0, pad))),
        moe_br=jnp.pad(jnp.concatenate([moe_b_rg, moe_b_re], axis=-1), ((0, 0), (0, pad)))[:, None, :],
        moe_w_gu=moe_w_gu.reshape(depth, N_EXPERTS, d, 2 * D_EXPERT),
        moe_w_down=moe_w_down.reshape(depth, N_EXPERTS, D_EXPERT, d),
        w_kv=w_kv.astype(BF16), at_w_q=at_w_q[0].astype(BF16), at_w_o=at_w_o[0].astype(BF16),
    )


def kernel(x_prompt, x_sample, state_wkv, state_shift, cache_k, cache_v, c_prompt, c_sample, w_ada, b_ada, g_norm, rw_mu, rw_w_rkv, rw_w0, rw_w1, rw_w2, rw_a0, rw_a1, rw_a2, rw_g1, rw_g2, rw_k_k, rw_k_a, rw_r_k, rw_ln_w, rw_ln_b, rw_w_o, moe_w_rg, moe_b_rg, moe_w_re, moe_b_re, moe_w_gu, moe_w_down, w_ada_kv, b_ada_kv, g_kv, w_kv, at_w_q, at_w_o, rel_bias, g_final):
    d = D_MODEL
    bp, tp = x_prompt.shape[:2]
    nb = x_sample.shape[0]
    p = _prepare_params(w_ada, b_ada, g_norm, rw_mu, rw_w_rkv, rw_w0, rw_w1, rw_w2, rw_a0, rw_a1, rw_a2, rw_g1,
                        rw_g2, rw_k_k, rw_k_a, rw_r_k, rw_ln_w, rw_ln_b, rw_w_o, moe_w_rg, moe_b_rg, moe_w_re,
                        moe_b_re, moe_w_gu, moe_w_down, w_ada_kv, b_ada_kv, g_kv, w_kv, at_w_q, at_w_o, rel_bias,
                        g_final)

    n_c = bp + nb
    n_pad = -n_c % 8
    c_all = jnp.concatenate([c_prompt, c_sample, jnp.zeros((n_pad, d), F32)], axis=0)
    mod_l = [_ada_linear(c_all, w_ada[l], b_ada[l]) for l in range(2)]
    mod_kv = _ada_linear(c_all, w_ada_kv, b_ada_kv)

    def split(m, n, lo, hi, per_batch):
        parts = [m[lo:hi, d * j:d * (j + 1)] for j in range(n)]
        return [q[:, None, :] if per_batch else q[None] for q in parts]

    mods_p = dict(l0=split(mod_l[0], 6, 0, bp, True), l1=split(mod_l[1], 6, 0, bp, True),
                  kv=split(mod_kv, 2, 0, bp, True))
    mods_s = dict(l0=split(mod_l[0], 6, bp, n_c, False), l1=split(mod_l[1], 6, bp, n_c, False),
                  kv=split(mod_kv, 2, bp, n_c, False))

    y_p, wkv_p, shift_p, k_p, v_p = _trunk(x_prompt, mods_p, None, None, None, p, tm=256, tm_moe=256)
    keep = min(BAND * max(DILATIONS), tp)
    k_prompt = k_p[:, tp - keep:].reshape(bp, keep, N_HEADS, HEAD)
    v_prompt = v_p[:, tp - keep:].reshape(bp, keep, N_HEADS, HEAD)

    xs = x_sample.reshape(1, nb, d)
    y_s, wkv_s, shift_s, k_s, v_s = _trunk(xs, mods_s, state_wkv[0], state_shift[0][None], (cache_k, cache_v), p,
                                           tm=nb, tm_moe=nb)
    return (y_p, y_s.reshape(nb, 1, d), wkv_p[None], shift_p[None], k_prompt, v_prompt,
            wkv_s[None], shift_s[None], k_s.reshape(nb, 1, N_HEADS, HEAD), v_s.reshape(nb, 1, N_HEADS, HEAD))
```

```python
import functools

import numpy as np
import jax
import jax.numpy as jnp
from jax import lax
from jax.experimental import pallas as pl
from jax.experimental.pallas import tpu as pltpu

F32 = jnp.float32
BF16 = jnp.bfloat16

D_MODEL = 1024
N_HEADS = 16
HEAD = 64
LANES = 128
N_PAIRS = D_MODEL // LANES
N_MIX = 6
GN_EPS = 64e-5
RMS_EPS = 1e-6
NEG_INF = -1e30
DILATIONS = (1, 4, 16)
BAND = 128
QB = 128
SCALE_B = HEAD ** -0.5
NUM_BUCKETS = 32
MAX_DISTANCE = 2048
N_GROUPS_E = 4
N_EXP_PER_GROUP = 8
N_EXPERTS = N_GROUPS_E * N_EXP_PER_GROUP
D_EXPERT = 512
ROUTER_LANE0 = N_GROUPS_E
SCAN_CHUNK = 64

VMEM_LIMIT = 56 * 1024 * 1024


def _cparams(sem):
    return pltpu.CompilerParams(dimension_semantics=sem, vmem_limit_bytes=VMEM_LIMIT)


def _dot(a, b):
    return jnp.dot(a.astype(BF16), b.astype(BF16), preferred_element_type=F32)


def _dot_nt(a, b):
    return lax.dot_general(a.astype(BF16), b.astype(BF16), (((1,), (1,)), ((), ())),
                           preferred_element_type=F32)


def _split3(a):
    hi = a.astype(BF16)
    r1 = a - hi.astype(F32)
    mid = r1.astype(BF16)
    lo = (r1 - mid.astype(F32)).astype(BF16)
    return hi, mid, lo


def _dot3(a, b01):
    hi, mid, lo = _split3(a)
    b = b01.astype(BF16)
    return (jnp.dot(hi, b, preferred_element_type=F32) + jnp.dot(mid, b, preferred_element_type=F32)
            + jnp.dot(lo, b, preferred_element_type=F32))


def _dot3_left(a01, b):
    hi, mid, lo = _split3(b)
    a = a01.astype(BF16)
    return (jnp.dot(a, hi, preferred_element_type=F32) + jnp.dot(a, mid, preferred_element_type=F32)
            + jnp.dot(a, lo, preferred_element_type=F32))


def _sigmoid(x):
    return 1.0 / (1.0 + jnp.exp(-x))


def _silu(x):
    return x * _sigmoid(x)


def _softplus(z):
    return jnp.maximum(z, 0.0) + jnp.log(1.0 + jnp.exp(-jnp.abs(z)))


def _norm_mod(x, g, sh, sc):
    ms = jnp.mean(x * x, axis=-1, keepdims=True)
    return (x * lax.rsqrt(ms + RMS_EPS) * g) * (1.0 + sc) + sh


def _head_sum_matrix():
    r = lax.broadcasted_iota(jnp.int32, (LANES, LANES), 0) >= HEAD
    c = lax.broadcasted_iota(jnp.int32, (LANES, LANES), 1) >= HEAD
    return jnp.where(r == c, 1.0, 0.0).astype(BF16)


def _head_sum(z, m128):
    parts = [_dot3(z[:, LANES * p:LANES * (p + 1)], m128) for p in range(N_PAIRS)]
    return jnp.concatenate(parts, axis=1)


def _row_spec(arr, tm):
    n = arr.shape[-1]
    if arr.shape[1] == 1:
        return pl.BlockSpec((1, 1, n), lambda b, i: (b, 0, 0))
    return pl.BlockSpec((1, tm, n), lambda b, i: (b, i, 0))


def _const_spec(arr):
    nd = arr.ndim
    return pl.BlockSpec(arr.shape, lambda b, i, _nd=nd: (0,) * _nd)


def _ada_kernel(c_ref, w_ref, b_ref, o_ref):
    o_ref[...] = _dot(_silu(c_ref[...]), w_ref[...]) + b_ref[...]


def _ada_linear(c_all, w, b):
    m, k = c_all.shape
    n = w.shape[1]
    tn = 1024
    return pl.pallas_call(
        _ada_kernel,
        out_shape=jax.ShapeDtypeStruct((m, n), F32),
        grid=(n // tn,),
        in_specs=[pl.BlockSpec((m, k), lambda j: (0, 0)),
                  pl.BlockSpec((k, tn), lambda j: (0, j)),
                  pl.BlockSpec((1, tn), lambda j: (0, j))],
        out_specs=pl.BlockSpec((m, tn), lambda j: (0, j)),
        compiler_params=_cparams(("parallel",)),
        name="ada_linear",
    )(c_all, w, b.reshape(1, n))


def _norm_mod_kernel(x_ref, g_ref, sh_ref, sc_ref, o_ref):
    o_ref[0] = _norm_mod(x_ref[0], g_ref[...], sh_ref[0], sc_ref[0])


def _norm_mod_call(x, g, sh, sc, tm):
    b, t, d = x.shape
    g2 = g.reshape(1, d)
    return pl.pallas_call(
        _norm_mod_kernel,
        out_shape=jax.ShapeDtypeStruct(x.shape, F32),
        grid=(b, t // tm),
        in_specs=[_row_spec(x, tm), _const_spec(g2), _row_spec(sh, tm), _row_spec(sc, tm)],
        out_specs=pl.BlockSpec((1, tm, d), lambda bb, i: (bb, i, 0)),
        compiler_params=_cparams(("parallel", "parallel")),
        name="norm_mod",
    )(x, g2, sh, sc)


def _final_norm_kernel(x_ref, g_ref, o_ref):
    x = x_ref[0]
    ms = jnp.mean(x * x, axis=-1, keepdims=True)
    o_ref[0] = x * lax.rsqrt(ms + RMS_EPS) * g_ref[...]


def _final_norm_call(x, g, tm):
    b, t, d = x.shape
    g2 = g.reshape(1, d)
    return pl.pallas_call(
        _final_norm_kernel,
        out_shape=jax.ShapeDtypeStruct(x.shape, F32),
        grid=(b, t // tm),
        in_specs=[_row_spec(x, tm), _const_spec(g2)],
        out_specs=pl.BlockSpec((1, tm, d), lambda bb, i: (bb, i, 0)),
        compiler_params=_cparams(("parallel", "parallel")),
        name="final_norm",
    )(x, g2)


def _linear_kernel(*refs, has_norm, has_resid):
    it = iter(refs)
    x_ref = next(it)
    w_ref = next(it)
    if has_norm:
        g_ref, sh_ref, sc_ref = next(it), next(it), next(it)
    if has_resid:
        xr_ref, gt_ref = next(it), next(it)
    o_ref = next(it)
    x = x_ref[0]
    if has_norm:
        x = _norm_mod(x, g_ref[...], sh_ref[0], sc_ref[0])
    out = _dot(x, w_ref[...])
    if has_resid:
        out = xr_ref[0] + gt_ref[0] * out
    o_ref[0] = out


def _linear_call(x, w, tm, norm=None, resid=None, name="linear"):
    b, t, k = x.shape
    n = w.shape[1]
    args = [x, w]
    specs = [_row_spec(x, tm), _const_spec(w)]
    if norm is not None:
        g, sh, sc = norm
        g2 = g.reshape(1, k)
        args += [g2, sh, sc]
        specs += [_const_spec(g2), _row_spec(sh, tm), _row_spec(sc, tm)]
    if resid is not None:
        xr, gt = resid
        args += [xr, gt]
        specs += [_row_spec(xr, tm), _row_spec(gt, tm)]
    return pl.pallas_call(
        functools.partial(_linear_kernel, has_norm=norm is not None, has_resid=resid is not None),
        out_shape=jax.ShapeDtypeStruct((b, t, n), F32),
        grid=(b, t // tm),
        in_specs=specs,
        out_specs=pl.BlockSpec((1, tm, n), lambda bb, i: (bb, i, 0)),
        compiler_params=_cparams(("parallel", "parallel")),
        name=name,
    )(*args)


def _fill_tiles(scr, val):
    for c in range(val.shape[1] // LANES):
        scr[c] = val[:, LANES * c:LANES * (c + 1)]


def _store_dilated(scr, out_ref, dil):
    n_tiles, tm, _ = scr.shape
    n = n_tiles * LANES
    for r in range(dil):
        rows = [scr[c, pl.ds(r, tm // dil, stride=dil), :] for c in range(n_tiles)]
        out_ref[0, :, n * r:n * (r + 1)] = jnp.concatenate(rows, axis=1).astype(out_ref.dtype)


def _load_dilated(in_ref, scr, dil):
    n_tiles, tm, _ = scr.shape
    n = n_tiles * LANES
    for r in range(dil):
        blk = in_ref[0, :, n * r:n * (r + 1)].astype(F32)
        for c in range(n_tiles):
            scr[c, pl.ds(r, tm // dil, stride=dil), :] = blk[:, LANES * c:LANES * (c + 1)]
    return jnp.concatenate([scr[c] for c in range(n_tiles)], axis=1)


def _qproj_kernel(x_ref, g_ref, sh_ref, sc_ref, w_ref, q0_ref, q1_ref, q2_ref, s1, s2):
    d = D_MODEL
    q = _dot(_norm_mod(x_ref[0], g_ref[...], sh_ref[0], sc_ref[0]), w_ref[...])
    q0_ref[0] = q[:, :d].astype(q0_ref.dtype)
    _fill_tiles(s1, q[:, d:2 * d])
    _store_dilated(s1, q1_ref, DILATIONS[1])
    _fill_tiles(s2, q[:, 2 * d:])
    _store_dilated(s2, q2_ref, DILATIONS[2])


def _dilated_shape(b, t, n, dil, dtype):
    return jax.ShapeDtypeStruct((b, t // dil, dil * n), dtype)


def _dilated_spec(tm, n, dil):
    return pl.BlockSpec((1, tm // dil, dil * n), lambda bb, i: (bb, i, 0))


def _qproj_call(x, w, g, sh, sc, tm):
    b, t, d = x.shape
    g2 = g.reshape(1, d)
    tile = pltpu.VMEM((N_PAIRS, tm, LANES), F32)
    return pl.pallas_call(
        _qproj_kernel,
        out_shape=[_dilated_shape(b, t, d, dil, BF16) for dil in DILATIONS],
        grid=(b, t // tm),
        in_specs=[_row_spec(x, tm), _const_spec(g2), _row_spec(sh, tm), _row_spec(sc, tm), _const_spec(w)],
        out_specs=[_dilated_spec(tm, d, dil) for dil in DILATIONS],
        scratch_shapes=[tile, tile],
        compiler_params=_cparams(("parallel", "parallel")),
        name="q_proj",
    )(x, g2, sh, sc, w)


def _kvproj_kernel(x_ref, g_ref, sh_ref, sc_ref, w_ref, k_ref, v_ref, k1_ref, v1_ref, k2_ref, v2_ref, sk, sv):
    d = D_MODEL
    kv = _dot(_norm_mod(x_ref[0], g_ref[...], sh_ref[0], sc_ref[0]), w_ref[...])
    k_ref[0] = kv[:, :d]
    v_ref[0] = kv[:, d:]
    _fill_tiles(sk, kv[:, :d])
    _fill_tiles(sv, kv[:, d:])
    _store_dilated(sk, k1_ref, DILATIONS[1])
    _store_dilated(sv, v1_ref, DILATIONS[1])
    _store_dilated(sk, k2_ref, DILATIONS[2])
    _store_dilated(sv, v2_ref, DILATIONS[2])


def _kvproj_call(x, w, g, sh, sc, tm):
    b, t, d = x.shape
    g2 = g.reshape(1, d)
    tile = pltpu.VMEM((N_PAIRS, tm, LANES), F32)
    plain = jax.ShapeDtypeStruct((b, t, d), F32)
    return pl.pallas_call(
        _kvproj_kernel,
        out_shape=[plain, plain] + [_dilated_shape(b, t, d, dil, BF16) for dil in DILATIONS[1:] for _ in range(2)],
        grid=(b, t // tm),
        in_specs=[_row_spec(x, tm), _const_spec(g2), _row_spec(sh, tm), _row_spec(sc, tm), _const_spec(w)],
        out_specs=[_dilated_spec(tm, d, 1)] * 2 + [_dilated_spec(tm, d, dil) for dil in DILATIONS[1:] for _ in range(2)],
        scratch_shapes=[tile, tile],
        compiler_params=_cparams(("parallel", "parallel")),
        name="kv_proj",
    )(x, g2, sh, sc, w)


PREV_ROWS = 8


def _rwkv_proj_kernel(x_ref, aux_ref, gn_ref, sh_ref, sc_ref, mu_ref, vec_ref, wrkv_ref, w1_ref, w2_ref, a1_ref, a2_ref,
                      g1_ref, g2_ref, r_ref, lw_ref, k_ref, v_ref, a_ref, b_ref, g_ref, bonus_ref, *, sequence):
    h = _norm_mod(x_ref[0], gn_ref[...], sh_ref[0], sc_ref[0])
    if sequence:
        prev = _norm_mod(aux_ref[0], gn_ref[...], sh_ref[0], sc_ref[0])[PREV_ROWS - 1:PREV_ROWS, :]
        prev = jnp.where(pl.program_id(1) > 0, prev, 0.0)
        row = lax.broadcasted_iota(jnp.int32, h.shape, 0)
        hp = jnp.where(row == 0, prev, pltpu.roll(h, 1, axis=0))
    else:
        hp = aux_ref[0]
    xx = hp - h

    def mix(i):
        return h + xx * mu_ref[i:i + 1, :]

    w0, a0, k_k, k_a, r_k = (vec_ref[i:i + 1, :] for i in range(5))
    m128 = _head_sum_matrix()
    r = _dot(mix(0), wrkv_ref[0])
    k = _dot(mix(1), wrkv_ref[1])
    v = _dot(mix(2), wrkv_ref[2])
    wl = w0 + _dot(jnp.tanh(_dot(mix(3), w1_ref[...])), w2_ref[...])
    w_log = -_softplus(-wl) - 0.5
    a_sig = _sigmoid(a0 + _dot(_dot(mix(4), a1_ref[...]), a2_ref[...]))
    g = _dot(_sigmoid(_dot(mix(5), g1_ref[...])), g2_ref[...])
    kk = k * k_k
    kk = kk / jnp.maximum(jnp.sqrt(_head_sum(kk * kk, m128)), 1e-12)
    k2 = k * (1.0 + (a_sig - 1.0) * k_a)
    r_ref[0] = r
    lw_ref[0] = -jnp.exp(w_log)
    k_ref[0] = k2
    v_ref[0] = v
    a_ref[0] = -kk
    b_ref[0] = kk * a_sig
    g_ref[0] = g
    bonus_ref[0] = _head_sum(r * k2 * r_k, m128) * v


def _rwkv_proj_call(x, shift, gn, sh, sc, mu, vecs, wrkv, w1, w2, a1, a2, g1, g2, tm):
    b, t, d = x.shape
    gn2 = gn.reshape(1, d)
    consts = [mu, vecs, wrkv, w1, w2, a1, a2, g1, g2]
    out = jax.ShapeDtypeStruct((b, t, d), F32)
    ospec = pl.BlockSpec((1, tm, d), lambda bb, i: (bb, i, 0))
    if shift is None:
        per = tm // PREV_ROWS
        aux, aux_spec = x, pl.BlockSpec((1, PREV_ROWS, d), lambda bb, i: (bb, jnp.maximum(i * per - 1, 0), 0))
    else:
        aux, aux_spec = shift, _row_spec(shift, tm)
    return pl.pallas_call(
        functools.partial(_rwkv_proj_kernel, sequence=shift is None),
        out_shape=[out] * 8,
        grid=(b, t // tm),
        in_specs=[_row_spec(x, tm), aux_spec, _const_spec(gn2), _row_spec(sh, tm), _row_spec(sc, tm)]
                 + [_const_spec(c) for c in consts],
        out_specs=[ospec] * 8,
        compiler_params=_cparams(("parallel", "parallel")),
        name="rwkv_proj",
    )(x, aux, gn2, sh, sc, *consts)


def _rwkv_scan_kernel(r_ref, lw_ref, k_ref, v_ref, a_ref, b_ref, y_ref, hout_ref, h_scr):
    c = pl.program_id(0)
    nc = pl.num_programs(0)
    nb = r_ref.shape[0]
    C = SCAN_CHUNK
    n2 = 2 * C

    @pl.when(c == 0)
    def _():
        h_scr[...] = jnp.zeros_like(h_scr)

    ri = lax.broadcasted_iota(jnp.int32, (C, C), 0)
    ci = lax.broadcasted_iota(jnp.int32, (C, C), 1)
    tril = jnp.where(ri >= ci, 1.0, 0.0).astype(BF16)
    head0 = lax.broadcasted_iota(jnp.int32, (C, LANES), 1) < HEAD
    tt = lax.broadcasted_iota(jnp.int32, (n2, n2), 0) & (C - 1)
    ss = lax.broadcasted_iota(jnp.int32, (n2, n2), 1) & (C - 1)
    strict = tt > ss
    incl = tt >= ss
    n_double = C.bit_length() - 1

    def two(xs):
        return jnp.concatenate([jnp.where(head0, xs, 0.0), jnp.where(head0, 0.0, xs)], axis=0)

    ar, bk, v2, w2t, gcol = [], [], [], [], []
    for bi in range(nb):
        lw = lw_ref[bi]
        cum = _dot3_left(tril, lw)
        gam = jnp.exp(cum)
        ginv = jnp.exp(-cum)
        g_last = gam[C - 1:C, :]
        rt = r_ref[bi] * gam
        at = a_ref[bi] * jnp.exp(cum - lw)
        bt = b_ref[bi] * ginv
        kt = k_ref[bi] * ginv
        bh = bt * g_last
        kh = kt * g_last
        v = v_ref[bi]
        for p in range(N_PAIRS):
            sl = slice(LANES * p, LANES * (p + 1))
            ar.append(jnp.concatenate([two(at[:, sl]), two(rt[:, sl])], axis=0).astype(BF16))
            bk.append(jnp.concatenate([two(bt[:, sl]), two(kt[:, sl])], axis=0).astype(BF16))
            v2.append(two(v[:, sl]).astype(BF16))
            w2t.append(jnp.transpose(jnp.concatenate([two(bh[:, sl]), two(kh[:, sl])], axis=0)).astype(BF16))
            gcol.append(jnp.transpose(jnp.broadcast_to(g_last[:, sl], (LANES, LANES))))
    n = len(ar)
    hp = [h_scr[i] for i in range(n)]
    s = [_dot_nt(ar[i], bk[i]) for i in range(n)]
    arh = [_dot(ar[i], hp[i]) for i in range(n)]
    pw = [jnp.where(strict, s[i][:n2, :n2], 0.0).astype(BF16) for i in range(n)]
    lt = [jnp.concatenate([jnp.where(strict, s[i][:n2, n2:], 0.0), jnp.where(incl, s[i][n2:, n2:], 0.0)],
                          axis=0) for i in range(n)]
    t_rb = [jnp.where(incl, s[i][n2:, :n2], 0.0) for i in range(n)]
    lv = [_dot(lt[i], v2[i]) for i in range(n)]
    u = [arh[i][:n2] + lv[i][:n2] for i in range(n)]
    for it in range(n_double):
        pu = [_dot(pw[i], u[i]) for i in range(n)]
        if it + 1 < n_double:
            pw = [_dot(pw[i], pw[i]).astype(BF16) for i in range(n)]
        u = [u[i] + pu[i] for i in range(n)]
    tu = [_dot(t_rb[i], u[i]) for i in range(n)]
    hn = [_dot(w2t[i], jnp.concatenate([u[i].astype(BF16), v2[i]], axis=0)) for i in range(n)]
    for i in range(n):
        h_scr[i] = gcol[i] * hp[i] + hn[i]
    for bi in range(nb):
        ys = []
        for p in range(N_PAIRS):
            i = bi * N_PAIRS + p
            y2 = arh[i][n2:] + lv[i][n2:] + tu[i]
            ys.append(y2[:C] + y2[C:])
        y_ref[bi] = jnp.concatenate(ys, axis=1)

    @pl.when(c == nc - 1)
    def _():
        hout_ref[...] = h_scr[...]


def _rwkv_scan_call(r, lw, k, v, a, bb):
    b, t, d = r.shape
    C = SCAN_CHUNK
    spec = pl.BlockSpec((b, C, d), lambda c: (0, c, 0))
    y, hout = pl.pallas_call(
        _rwkv_scan_kernel,
        out_shape=[jax.ShapeDtypeStruct((b, t, d), F32),
                   jax.ShapeDtypeStruct((b * N_PAIRS, LANES, LANES), F32)],
        grid=(t // C,),
        in_specs=[spec] * 6,
        out_specs=[spec, pl.BlockSpec((b * N_PAIRS, LANES, LANES), lambda c: (0, 0, 0))],
        scratch_shapes=[pltpu.VMEM((b * N_PAIRS, LANES, LANES), F32)],
        compiler_params=_cparams(("arbitrary",)),
        name="rwkv_scan",
    )(r, lw, k, v, a, bb)
    hh = hout.reshape(b, N_PAIRS, 2, HEAD, 2, HEAD)
    hh = jnp.stack([hh[:, :, 0, :, 0, :], hh[:, :, 1, :, 1, :]], axis=2).reshape(b, N_HEADS, HEAD, HEAD)
    return y, jnp.swapaxes(hh, -1, -2)


def _rwkv_step_kernel(s_ref, z_ref, so_ref, y_ref):
    r, lw, k, v, a, b = (z_ref[c, 0] for c in range(6))
    w = jnp.exp(lw)
    ys = []
    for i in range(HEAD):
        s = s_ref[0, i]
        sa = jnp.sum(s * a, axis=0, keepdims=True)
        sn = s * w + sa * b + v[i:i + 1, :] * k
        so_ref[0, i] = sn
        ys.append(jnp.sum(sn * r, axis=0, keepdims=True))
    y_ref[0] = jnp.concatenate(ys, axis=0)


def _rwkv_step_call(state, r, lw, k, v, a, bb):
    nb = state.shape[0]
    st = jnp.transpose(state, (1, 2, 3, 0))
    z = jnp.stack([jnp.transpose(q[0]).reshape(N_HEADS, HEAD, nb) for q in (r, lw, k, v, a, bb)])
    so, y = pl.pallas_call(
        _rwkv_step_kernel,
        out_shape=[jax.ShapeDtypeStruct((N_HEADS, HEAD, HEAD, nb), F32),
                   jax.ShapeDtypeStruct((N_HEADS, HEAD, nb), F32)],
        grid=(N_HEADS,),
        in_specs=[pl.BlockSpec((1, HEAD, HEAD, nb), lambda h: (h, 0, 0, 0)),
                  pl.BlockSpec((6, 1, HEAD, nb), lambda h: (0, h, 0, 0))],
        out_specs=[pl.BlockSpec((1, HEAD, HEAD, nb), lambda h: (h, 0, 0, 0)),
                   pl.BlockSpec((1, HEAD, nb), lambda h: (h, 0, 0))],
        compiler_params=_cparams(("parallel",)),
        name="rwkv_step",
    )(st, z)
    return jnp.transpose(y.reshape(D_MODEL, nb))[None], jnp.transpose(so, (3, 0, 1, 2))


def _rwkv_out_kernel(y_ref, bonus_ref, g_ref, x_ref, gt_ref, ln_ref, wo_ref, o_ref):
    y = y_ref[0]
    m128 = _head_sum_matrix()
    mean = _head_sum(y, m128) * (1.0 / HEAD)
    dlt = y - mean
    var = _head_sum(dlt * dlt, m128) * (1.0 / HEAD)
    yn = dlt * lax.rsqrt(var + GN_EPS) * ln_ref[0:1, :] + ln_ref[1:2, :]
    out = _dot((yn + bonus_ref[0]) * g_ref[0], wo_ref[...])
    o_ref[0] = x_ref[0] + gt_ref[0] * out


def _rwkv_out_call(y, bonus, g, x, gt, ln, wo, tm):
    b, t, d = y.shape
    return pl.pallas_call(
        _rwkv_out_kernel,
        out_shape=jax.ShapeDtypeStruct((b, t, d), F32),
        grid=(b, t // tm),
        in_specs=[_row_spec(y, tm), _row_spec(bonus, tm), _row_spec(g, tm), _row_spec(x, tm), _row_spec(gt, tm),
                  _const_spec(ln), _const_spec(wo)],
        out_specs=pl.BlockSpec((1, tm, d), lambda bb, i: (bb, i, 0)),
        compiler_params=_cparams(("parallel", "parallel")),
        name="rwkv_out",
    )(y, bonus, g, x, gt, ln, wo)


ROUTE_E, ROUTE_W, ROUTE_RANK = 0, 2, 4


def _router_kernel(x_ref, g_ref, sh_ref, sc_ref, wr_ref, br_ref, hf_ref, route_ref, cnt_ref, carry):
    first = jnp.logical_and(pl.program_id(0) == 0, pl.program_id(1) == 0)

    @pl.when(first)
    def _():
        carry[...] = jnp.zeros_like(carry)

    hf = _norm_mod(x_ref[0], g_ref[...], sh_ref[0], sc_ref[0])
    hf_ref[0] = hf
    logits = jnp.dot(hf, wr_ref[...], preferred_element_type=F32, precision=lax.Precision.HIGHEST) + br_ref[...]
    lane = lax.broadcasted_iota(jnp.int32, logits.shape, 1).astype(F32)
    big = float(1 << 20)
    gl = jnp.where(lane < N_GROUPS_E, logits, NEG_INF)
    gmax = jnp.max(gl, axis=-1, keepdims=True)
    p_top = 1.0 / jnp.sum(jnp.exp(gl - gmax), axis=-1, keepdims=True)
    g_sel = jnp.min(jnp.where(gl == gmax, lane, big), axis=-1, keepdims=True)
    lo = ROUTER_LANE0 + N_EXP_PER_GROUP * g_sel
    el = jnp.where((lane >= lo) & (lane < lo + N_EXP_PER_GROUP), logits, NEG_INF)
    v1 = jnp.max(el, axis=-1, keepdims=True)
    i1 = jnp.min(jnp.where(el == v1, lane, big), axis=-1, keepdims=True)
    el2 = jnp.where(lane == i1, NEG_INF, el)
    v2 = jnp.max(el2, axis=-1, keepdims=True)
    i2 = jnp.min(jnp.where(el2 == v2, lane, big), axis=-1, keepdims=True)
    e21 = jnp.exp(v2 - v1)
    w1 = p_top / (1.0 + e21)
    w2 = p_top * e21 / (1.0 + e21)
    tm = logits.shape[0]
    onehot = jnp.where(lane == i1, 1.0, 0.0) + jnp.where(lane == i2, 1.0, 0.0)
    rr = lax.broadcasted_iota(jnp.int32, (tm, tm), 0)
    cc = lax.broadcasted_iota(jnp.int32, (tm, tm), 1)
    earlier = jnp.where(rr > cc, 1.0, 0.0).astype(BF16)
    before = jnp.dot(earlier, onehot.astype(BF16), preferred_element_type=F32) + carry[0:1, :]
    rank1 = jnp.sum(jnp.where(lane == i1, before, 0.0), axis=-1, keepdims=True)
    rank2 = jnp.sum(jnp.where(lane == i2, before, 0.0), axis=-1, keepdims=True)
    carry[...] = carry[...] + jnp.sum(onehot, axis=0, keepdims=True)
    cnt_ref[...] = carry[...]
    route = jnp.zeros_like(logits)
    for ln, val in ((ROUTE_E, i1 - ROUTER_LANE0), (ROUTE_E + 1, i2 - ROUTER_LANE0), (ROUTE_W, w1), (ROUTE_W + 1, w2),
                    (ROUTE_RANK, rank1), (ROUTE_RANK + 1, rank2)):
        route = jnp.where(lane == ln, val, route)
    route_ref[0] = route


def _router_call(x, g, sh, sc, wr, br, tm):
    b, t, d = x.shape
    g2 = g.reshape(1, d)
    return pl.pallas_call(
        _router_kernel,
        out_shape=[jax.ShapeDtypeStruct((b, t, d), F32), jax.ShapeDtypeStruct((b, t, LANES), F32),
                   jax.ShapeDtypeStruct((8, LANES), F32)],
        grid=(b, t // tm),
        in_specs=[_row_spec(x, tm), _const_spec(g2), _row_spec(sh, tm), _row_spec(sc, tm),
                  _const_spec(wr), _const_spec(br)],
        out_specs=[pl.BlockSpec((1, tm, d), lambda bb, i: (bb, i, 0)),
                   pl.BlockSpec((1, tm, LANES), lambda bb, i: (bb, i, 0)),
                   pl.BlockSpec((8, LANES), lambda bb, i: (0, 0))],
        scratch_shapes=[pltpu.VMEM((8, LANES), F32)],
        compiler_params=_cparams(("arbitrary", "arbitrary")),
        name="moe_router",
    )(x, g2, sh, sc, wr, br)


def _route_plan(route, counts, tile):
    n = route.shape[0] * route.shape[1]
    r2 = route.reshape(n, LANES)
    e = r2[:, ROUTE_E:ROUTE_E + 2].astype(jnp.int32)
    rank = r2[:, ROUTE_RANK:ROUTE_RANK + 2].astype(jnp.int32)
    cnt = counts[0, ROUTER_LANE0:ROUTER_LANE0 + N_EXPERTS].astype(jnp.int32)
    nt = (cnt + tile - 1) // tile
    tend = jnp.cumsum(nt)
    tstart = tend - nt
    onehot = e[:, :, None] == jnp.arange(N_EXPERTS, dtype=jnp.int32)[None, None, :]
    dest = jnp.sum(jnp.where(onehot, tstart[None, None, :], 0), axis=-1) * tile + rank
    n_tiles = (2 * n) // tile + N_EXPERTS
    tid = jnp.arange(n_tiles, dtype=jnp.int32)
    te = jnp.minimum(jnp.sum((tid[:, None] >= tend[None, :]).astype(jnp.int32), axis=1), N_EXPERTS - 1)
    active = (tid < tend[-1]).astype(jnp.int32)
    return dest.reshape(-1), te, active, n_tiles


def _dispatch_kernel(dest_ref, hf_ref, xs_in_ref, xs_ref, sem, *, tm):
    del xs_in_ref
    base = pl.program_id(0) * (2 * tm)

    def body(t, carry):
        row = hf_ref.at[pl.ds(t, 1), :]
        for c in range(2):
            dst = dest_ref[base + 2 * t + c]
            pltpu.make_async_copy(row, xs_ref.at[pl.ds(dst, 1), :], sem).start()
        return carry

    lax.fori_loop(0, tm, body, 0, unroll=8)
    done = xs_ref.at[pl.ds(0, 2 * tm), :]
    pltpu.make_async_copy(done, done, sem).wait()


def _dispatch_call(hf2, dest, n_rows, tm):
    n, d = hf2.shape
    xs0 = jnp.zeros((n_rows, d), F32)
    return pl.pallas_call(
        functools.partial(_dispatch_kernel, tm=tm),
        out_shape=jax.ShapeDtypeStruct((n_rows, d), F32),
        grid_spec=pltpu.PrefetchScalarGridSpec(
            num_scalar_prefetch=1,
            grid=(n // tm,),
            in_specs=[pl.BlockSpec((tm, d), lambda i, dest: (i, 0)),
                      pl.BlockSpec(memory_space=pl.ANY)],
            out_specs=pl.BlockSpec(memory_space=pl.ANY),
            scratch_shapes=[pltpu.SemaphoreType.DMA(())],
        ),
        input_output_aliases={2: 0},
        compiler_params=pltpu.CompilerParams(dimension_semantics=("arbitrary",), vmem_limit_bytes=VMEM_LIMIT,
                                             has_side_effects=True),
        name="moe_dispatch",
    )(dest, hf2, xs0)


def _expert_kernel(te_ref, act_ref, xs_ref, wgu_ref, wd_ref, o_ref, wgu_lp, wd_lp):
    i = pl.program_id(0)
    changed = jnp.logical_or(i == 0, te_ref[i] != te_ref[jnp.maximum(i - 1, 0)])

    @pl.when(changed)
    def _():
        wgu_lp[...] = wgu_ref[0].astype(wgu_lp.dtype)
        wd_lp[...] = wd_ref[0].astype(wd_lp.dtype)

    @pl.when(act_ref[i] == 1)
    def _():
        gu = _dot(xs_ref[...], wgu_lp[...])
        act = _silu(gu[:, :D_EXPERT]) * gu[:, D_EXPERT:]
        o_ref[...] = _dot(act, wd_lp[...])

    @pl.when(act_ref[i] == 0)
    def _():
        o_ref[...] = jnp.zeros_like(o_ref)


def _expert_call(xs, te, active, wgu, wd, tile):
    n_rows, d = xs.shape
    return pl.pallas_call(
        _expert_kernel,
        out_shape=jax.ShapeDtypeStruct((n_rows, d), F32),
        grid_spec=pltpu.PrefetchScalarGridSpec(
            num_scalar_prefetch=2,
            grid=(n_rows // tile,),
            in_specs=[pl.BlockSpec((tile, d), lambda i, te, act: (i, 0)),
                      pl.BlockSpec((1, d, 2 * D_EXPERT), lambda i, te, act: (te[i], 0, 0)),
                      pl.BlockSpec((1, D_EXPERT, d), lambda i, te, act: (te[i], 0, 0))],
            out_specs=pl.BlockSpec((tile, d), lambda i, te, act: (i, 0)),
            scratch_shapes=[pltpu.VMEM((d, 2 * D_EXPERT), BF16), pltpu.VMEM((D_EXPERT, d), BF16)],
        ),
        compiler_params=_cparams(("arbitrary",)),
        name="moe_experts",
    )(te, active, xs, wgu, wd)


def _combine_kernel(dest_ref, ys_ref, route_ref, x_ref, gt_ref, *rest, tm, final_norm):
    if final_norm:
        gf_ref, o_ref, buf, sem = rest
    else:
        o_ref, buf, sem = rest
    step = pl.program_id(0) * pl.num_programs(1) + pl.program_id(1)
    base = step * (2 * tm)

    def body(t, carry):
        for c in range(2):
            src = dest_ref[base + 2 * t + c]
            pltpu.make_async_copy(ys_ref.at[pl.ds(src, 1), :], buf.at[c, pl.ds(t, 1), :], sem).start()
        return carry

    lax.fori_loop(0, tm, body, 0, unroll=8)
    pltpu.make_async_copy(buf, buf, sem).wait()
    route = route_ref[0]
    lane = lax.broadcasted_iota(jnp.int32, route.shape, 1)
    w1 = jnp.sum(jnp.where(lane == ROUTE_W, route, 0.0), axis=-1, keepdims=True)
    w2 = jnp.sum(jnp.where(lane == ROUTE_W + 1, route, 0.0), axis=-1, keepdims=True)
    out = x_ref[0] + gt_ref[0] * (w1 * buf[0] + w2 * buf[1])
    if final_norm:
        ms = jnp.mean(out * out, axis=-1, keepdims=True)
        out = out * lax.rsqrt(ms + RMS_EPS) * gf_ref[...]
    o_ref[0] = out


def _combine_call(ys, dest, route, x, gt, tm, final_g=None):
    b, t, d = x.shape
    extra = [] if final_g is None else [final_g.reshape(1, d)]

    def rs(arr):
        n = arr.shape[-1]
        if arr.shape[1] == 1:
            return pl.BlockSpec((1, 1, n), lambda bb, i, dest: (bb, 0, 0))
        return pl.BlockSpec((1, tm, n), lambda bb, i, dest: (bb, i, 0))

    return pl.pallas_call(
        functools.partial(_combine_kernel, tm=tm, final_norm=final_g is not None),
        out_shape=jax.ShapeDtypeStruct((b, t, d), F32),
        grid_spec=pltpu.PrefetchScalarGridSpec(
            num_scalar_prefetch=1,
            grid=(b, t // tm),
            in_specs=[pl.BlockSpec(memory_space=pl.ANY), rs(route), rs(x), rs(gt)]
                     + [pl.BlockSpec((1, d), lambda bb, i, dest: (0, 0)) for _ in extra],
            out_specs=pl.BlockSpec((1, tm, d), lambda bb, i, dest: (bb, i, 0)),
            scratch_shapes=[pltpu.VMEM((2, tm, d), F32), pltpu.SemaphoreType.DMA(())],
        ),
        compiler_params=_cparams(("arbitrary", "arbitrary")),
        name="moe_combine",
    )(dest, ys, route, x, gt, *extra)


def _moe_call(x, g, sh, sc, gt, wr, br, wgu, wd, tm, tile, final_g=None):
    b, t, d = x.shape
    hf, route, counts = _router_call(x, g, sh, sc, wr, br, tm)
    dest, te, active, n_tiles = _route_plan(route, counts, tile)
    xs = _dispatch_call(hf.reshape(b * t, d), dest, n_tiles * tile, tm)
    ys = _expert_call(xs, te, active, wgu, wd, tile)
    return _combine_call(ys, dest, route, x, gt, tm, final_g)


def _t5_bucket(dist):
    dist = np.asarray(dist)
    max_exact = NUM_BUCKETS // 2
    log_ratio = np.log(np.maximum(dist, 1) / max_exact) / np.log(MAX_DISTANCE / max_exact)
    large = np.minimum(max_exact + (log_ratio * (NUM_BUCKETS - max_exact)).astype(np.int32), NUM_BUCKETS - 1)
    return np.where(dist < max_exact, dist, large).astype(np.int32)


def _attn_prompt_kernel(q_ref, kp_ref, kc_ref, vp_ref, vc_ref, bias_ref, o_ref, lse_ref):
    i = pl.program_id(2)
    q = q_ref[0].astype(BF16)
    kcat = jnp.concatenate([kp_ref[0], kc_ref[0]], axis=0).astype(BF16)
    vcat = jnp.concatenate([vp_ref[0], vc_ref[0]], axis=0).astype(BF16)
    head0 = lax.broadcasted_iota(jnp.int32, (QB, LANES), 1) < HEAD
    col = lax.broadcasted_iota(jnp.int32, (QB, 2 * QB), 1)
    key_ok = col >= jnp.where(i > 0, 0, QB)
    lane = lax.broadcasted_iota(jnp.int32, (QB, LANES), 1)
    lse_tile = jnp.zeros((QB, LANES), F32)
    outs = []
    zero = jnp.zeros((), BF16)
    for p in range(N_PAIRS):
        sl = slice(LANES * p, LANES * (p + 1))
        qp, kp, vp = q[:, sl], kcat[:, sl], vcat[:, sl]
        o_pair = []
        for hh in range(2):
            h = 2 * p + hh
            qm = jnp.where(head0, qp, zero) if hh == 0 else jnp.where(head0, zero, qp)
            logits = _dot_nt(qm, kp) * SCALE_B + bias_ref[h]
            logits = jnp.where(key_ok, logits, NEG_INF)
            m = jnp.max(logits, axis=-1, keepdims=True)
            pr = jnp.exp(logits - m)
            ssum = jnp.sum(pr, axis=-1, keepdims=True)
            o_pair.append(_dot(pr, vp) / ssum)
            lse_tile = lse_tile + jnp.where(lane == h, m + jnp.log(ssum), 0.0)
        outs.append(jnp.where(head0, o_pair[0], o_pair[1]))
    o_ref[0] = jnp.concatenate(outs, axis=1)
    lse_ref[0] = lse_tile


def _attn_prompt_call(q, k, v, bias, dil):
    b, L, _ = k.shape
    nblk = L // QB
    d = D_MODEL
    cur = pl.BlockSpec((1, QB, d), lambda bb, r, i: (bb, i, r))
    prev = pl.BlockSpec((1, QB, d), lambda bb, r, i: (bb, jnp.maximum(i - 1, 0), r))
    return pl.pallas_call(
        _attn_prompt_kernel,
        out_shape=[jax.ShapeDtypeStruct((b, L, dil * d), F32), jax.ShapeDtypeStruct((b, L, dil * LANES), F32)],
        grid=(b, dil, nblk),
        in_specs=[cur, prev, cur, prev, cur, pl.BlockSpec(bias.shape, lambda bb, r, i: (0, 0, 0))],
        out_specs=[cur, pl.BlockSpec((1, QB, LANES), lambda bb, r, i: (bb, i, r))],
        compiler_params=_cparams(("parallel", "parallel", "parallel")),
        name=f"attn_prompt_d{dil}",
    )(q, k, k, v, v, bias)


def _attn_merge_kernel(o0_ref, o1_ref, o2_ref, l0_ref, l1_ref, l2_ref, e_ref, wo_ref, x_ref, gt_ref, out_ref,
                       so1, so2, sl1, sl2):
    o1 = _load_dilated(o1_ref, so1, DILATIONS[1])
    o2 = _load_dilated(o2_ref, so2, DILATIONS[2])
    l0 = l0_ref[0]
    l1 = _load_dilated(l1_ref, sl1, DILATIONS[1])
    l2 = _load_dilated(l2_ref, sl2, DILATIONS[2])
    mx = jnp.maximum(jnp.maximum(l0, l1), l2)
    e0, e1, e2 = jnp.exp(l0 - mx), jnp.exp(l1 - mx), jnp.exp(l2 - mx)
    inv = 1.0 / (e0 + e1 + e2)
    em = e_ref[...]
    o = _dot3(e0 * inv, em) * o0_ref[0] + _dot3(e1 * inv, em) * o1 + _dot3(e2 * inv, em) * o2
    out_ref[0] = x_ref[0] + gt_ref[0] * _dot(o, wo_ref[...])


def _head_expand_matrix():
    return jnp.asarray(np.arange(LANES)[:, None] == (np.arange(D_MODEL) // HEAD)[None, :], BF16)


def _attn_merge_call(os_, ls_, wo, x, gt, tm):
    b, t, d = x.shape
    em = _head_expand_matrix()
    o_specs = [_dilated_spec(tm, d, dil) for dil in DILATIONS]
    l_specs = [_dilated_spec(tm, LANES, dil) for dil in DILATIONS]
    o_tile = pltpu.VMEM((N_PAIRS, tm, LANES), F32)
    l_tile = pltpu.VMEM((1, tm, LANES), F32)
    return pl.pallas_call(
        _attn_merge_kernel,
        out_shape=jax.ShapeDtypeStruct((b, t, d), F32),
        grid=(b, t // tm),
        in_specs=o_specs + l_specs + [_const_spec(em), _const_spec(wo), _row_spec(x, tm), _row_spec(gt, tm)],
        out_specs=pl.BlockSpec((1, tm, d), lambda bb, i: (bb, i, 0)),
        scratch_shapes=[o_tile, o_tile, l_tile, l_tile],
        compiler_params=_cparams(("parallel", "parallel")),
        name="attn_merge_out",
    )(*os_, *ls_, em, wo, x, gt)


SAMPLE_HEADS_PER_STEP = 8


def _attn_sample_kernel(q_ref, kn_ref, vn_ref, k_ref, v_ref, bias_ref, b0_ref, o_ref):
    n_buf = k_ref.shape[-1]
    hb = SAMPLE_HEADS_PER_STEP
    ng = len(DILATIONS)
    lane = lax.broadcasted_iota(jnp.int32, (HEAD, LANES), 1)
    los = [n_buf - BAND * dil for dil in DILATIONS]
    chains = [(hh, g) for hh in range(hb) for g in range(ng)]
    q = {(hh, g): q_ref[0, g, 0][:, hh:hh + 1] * SCALE_B for hh, g in chains}
    kn = [kn_ref[0, 0][:, hh:hh + 1] for hh in range(hb)]
    vn = [vn_ref[0, 0][:, hh:hh + 1] for hh in range(hb)]
    lk = {(hh, g): jnp.sum(k_ref[0, hh, :, los[g]:] * q[hh, g], axis=0, keepdims=True)
          + bias_ref[g, hh:hh + 1, los[g]:] for hh, g in chains}
    l0 = {(hh, g): jnp.sum(kn[hh] * q[hh, g], axis=0, keepdims=True) + b0_ref[g, 0][0:1, hh:hh + 1]
          for hh, g in chains}
    m = {c: jnp.maximum(jnp.max(lk[c], axis=1, keepdims=True), l0[c]) for c in chains}
    pk = {c: jnp.exp(lk[c] - m[c]) for c in chains}
    p0 = {c: jnp.exp(l0[c] - m[c]) for c in chains}
    ssum = {c: jnp.sum(pk[c], axis=1, keepdims=True) + p0[c] for c in chains}
    num = {(hh, g): jnp.sum(v_ref[0, hh, :, los[g]:] * pk[hh, g], axis=1, keepdims=True) + p0[hh, g] * vn[hh]
           for hh, g in chains}
    o_tile = jnp.zeros((HEAD, LANES), F32)
    for hh in range(hb):
        mx = jnp.maximum(jnp.maximum(m[hh, 0], m[hh, 1]), m[hh, 2])
        cs = [jnp.exp(m[hh, g] - mx) for g in range(ng)]
        inv = 1.0 / (cs[0] * ssum[hh, 0] + cs[1] * ssum[hh, 1] + cs[2] * ssum[hh, 2])
        o_col = (cs[0] * inv) * num[hh, 0] + (cs[1] * inv) * num[hh, 1] + (cs[2] * inv) * num[hh, 2]
        o_tile = jnp.where(lane == hh, o_col, o_tile)
    o_ref[0, 0] = o_tile


def _attn_sample_call(q, kn, vn, cache_k, cache_v, bias_p, bias_0):
    nb, n_buf = cache_k.shape[:2]
    assert n_buf == BAND * max(DILATIONS)
    hb = SAMPLE_HEADS_PER_STEP
    nhb = N_HEADS // hb
    ng = len(DILATIONS)
    kt = jnp.transpose(cache_k, (0, 2, 3, 1))
    vt = jnp.transpose(cache_v, (0, 2, 3, 1))
    qt = jnp.transpose(q.reshape(nb, ng, nhb, hb, HEAD), (0, 1, 2, 4, 3))
    knt = jnp.transpose(kn.reshape(nb, nhb, hb, HEAD), (0, 1, 3, 2))
    vnt = jnp.transpose(vn.reshape(nb, nhb, hb, HEAD), (0, 1, 3, 2))
    o = pl.pallas_call(
        _attn_sample_kernel,
        out_shape=jax.ShapeDtypeStruct((nb, nhb, HEAD, LANES), F32),
        grid=(nb, nhb),
        in_specs=[pl.BlockSpec((1, ng, 1, HEAD, hb), lambda b, j: (b, 0, j, 0, 0)),
                  pl.BlockSpec((1, 1, HEAD, hb), lambda b, j: (b, j, 0, 0)),
                  pl.BlockSpec((1, 1, HEAD, hb), lambda b, j: (b, j, 0, 0)),
                  pl.BlockSpec((1, hb, HEAD, n_buf), lambda b, j: (b, j, 0, 0)),
                  pl.BlockSpec((1, hb, HEAD, n_buf), lambda b, j: (b, j, 0, 0)),
                  pl.BlockSpec((ng, hb, n_buf), lambda b, j: (0, j, 0)),
                  pl.BlockSpec((ng, 1, 8, hb), lambda b, j: (0, j, 0, 0))],
        out_specs=pl.BlockSpec((1, 1, HEAD, LANES), lambda b, j: (b, j, 0, 0)),
        compiler_params=_cparams(("parallel", "parallel")),
        name="attn_sample",
    )(qt, knt, vnt, kt, vt, bias_p, bias_0)
    return jnp.transpose(o[..., :hb], (0, 1, 3, 2)).reshape(nb, N_HEADS * HEAD)


def _prompt_bias(rel_bias, gi, dil):
    table = rel_bias[:, gi * N_HEADS:(gi + 1) * N_HEADS].astype(F32)
    per_dist = table[_t5_bucket(np.arange(BAND + 1) * dil)]
    delta = (lax.broadcasted_iota(jnp.int32, (QB, 2 * QB), 0) + QB
             - lax.broadcasted_iota(jnp.int32, (QB, 2 * QB), 1))
    onehot = (delta[:, :, None] == jnp.arange(BAND + 1, dtype=jnp.int32)[None, None, :]).astype(F32)
    bias = jnp.einsum("qkj,jh->hqk", onehot, per_dist, precision=lax.Precision.HIGHEST)
    in_band = (delta >= 0) & (delta <= BAND)
    return jnp.where(in_band[None], bias, NEG_INF)


def _sample_bias(rel_bias, n_buf):
    hb = SAMPLE_HEADS_PER_STEP
    dist = n_buf - np.arange(n_buf)
    bps, b0s = [], []
    for gi, dil in enumerate(DILATIONS):
        table = rel_bias[:, gi * N_HEADS:(gi + 1) * N_HEADS].astype(F32)
        in_window = (dist % dil == 0) & (dist <= BAND * dil)
        bps.append(jnp.where(in_window[None, :], jnp.transpose(table[_t5_bucket(dist)]), NEG_INF))
        b0s.append(jnp.broadcast_to(table[0].reshape(N_HEADS // hb, 1, hb), (N_HEADS // hb, 8, hb)))
    return jnp.stack(bps), jnp.stack(b0s)


def _trunk(x, mods, wkv0, shift0, caches, p, tm, tm_moe):
    b, t, d = x.shape
    m0, m1, mkv = mods["l0"], mods["l1"], mods["kv"]

    sh_m, sc_m, gt_m, sh_c, sc_c, gt_c = m0
    r, lw, k, v, a, bb, g, bonus = _rwkv_proj_call(x, shift0, p["g_norm"][0, 0], sh_m, sc_m, p["rw_mu"], p["rw_vecs"],
                                                   p["rw_w_rkv"], p["rw_w1"], p["rw_w2"], p["rw_a1"], p["rw_a2"],
                                                   p["rw_g1"], p["rw_g2"], min(tm, 128))
    if wkv0 is None:
        y, wkv = _rwkv_scan_call(r, lw, k, v, a, bb)
        shift = _norm_mod_call(x[:, t - PREV_ROWS:], p["g_norm"][0, 0], sh_m, sc_m, PREV_ROWS)[:, -1]
    else:
        y, wkv = _rwkv_step_call(wkv0, r, lw, k, v, a, bb)
        shift = _norm_mod_call(x, p["g_norm"][0, 0], sh_m, sc_m, tm)[0]
    x = _rwkv_out_call(y, bonus, g, x, gt_m, p["rw_ln"], p["rw_w_o"], tm)
    x = _moe_call(x, p["g_norm"][0, 1], sh_c, sc_c, gt_c, p["moe_wr"][0], p["moe_br"][0],
                  p["moe_w_gu"][0], p["moe_w_down"][0], tm, tm_moe)

    sh_k, sc_k = mkv
    sh_m, sc_m, gt_m, sh_c, sc_c, gt_c = m1
    if caches is None:
        k_sh, v_sh, k1, v1, k2, v2 = _kvproj_call(x, p["w_kv"], p["g_kv"], sh_k, sc_k, tm)
        qs = _qproj_call(x, p["at_w_q"], p["g_norm"][1, 0], sh_m, sc_m, tm)
        ks, vs = (k_sh, k1, k2), (v_sh, v1, v2)
        os_, ls_ = [], []
        for gi, dil in enumerate(DILATIONS):
            o, lse = _attn_prompt_call(qs[gi], ks[gi], vs[gi], _prompt_bias(p["rel_bias"], gi, dil), dil)
            os_.append(o)
            ls_.append(lse)
        x = _attn_merge_call(os_, ls_, p["at_w_o"], x, gt_m, tm)
    else:
        kvp = _linear_call(x, p["w_kv"], tm, norm=(p["g_kv"], sh_k, sc_k), name="kv_proj")
        k_sh, v_sh = kvp[..., :d], kvp[..., d:]
        q = _linear_call(x, p["at_w_q"], tm, norm=(p["g_norm"][1, 0], sh_m, sc_m), name="q_proj")
        cache_k, cache_v = caches
        nb, n_buf = cache_k.shape[:2]
        bias_p, bias_0 = _sample_bias(p["rel_bias"], n_buf)
        o = _attn_sample_call(q.reshape(nb, len(DILATIONS), N_HEADS, HEAD), k_sh.reshape(nb, N_HEADS, HEAD),
                              v_sh.reshape(nb, N_HEADS, HEAD), cache_k, cache_v, bias_p, bias_0).reshape(b, t, d)
        x = _linear_call(o, p["at_w_o"], tm, resid=(x, gt_m), name="attn_out")
    y = _moe_call(x, p["g_norm"][1, 1], sh_c, sc_c, gt_c, p["moe_wr"][1], p["moe_br"][1],
                  p["moe_w_gu"][1], p["moe_w_down"][1], tm, tm_moe, final_g=p["g_final"])
    return y, wkv, shift, k_sh, v_sh


def _prepare_params(w_ada, b_ada, g_norm, rw_mu, rw_w_rkv, rw_w0, rw_w1, rw_w2, rw_a0, rw_a1, rw_a2, rw_g1, rw_g2,
                    rw_k_k, rw_k_a, rw_r_k, rw_ln_w, rw_ln_b, rw_w_o, moe_w_rg, moe_b_rg, moe_w_re, moe_b_re,
                    moe_w_gu, moe_w_down, w_ada_kv, b_ada_kv, g_kv, w_kv, at_w_q, at_w_o, rel_bias, g_final):
    d = D_MODEL
    zeros = jnp.zeros((3, d), F32)
    pad = LANES - N_GROUPS_E - N_EXPERTS
    depth = moe_w_rg.shape[0]
    return dict(
        g_norm=g_norm, g_kv=g_kv, g_final=g_final, rel_bias=rel_bias,
        rw_mu=rw_mu[0],
        rw_vecs=jnp.concatenate([rw_w0[0][None], rw_a0[0][None], rw_k_k[0][None], rw_k_a[0][None],
                                 rw_r_k[0].reshape(1, d), zeros], axis=0),
        rw_w_rkv=rw_w_rkv[0].astype(BF16), rw_w1=rw_w1[0], rw_w2=rw_w2[0], rw_a1=rw_a1[0], rw_a2=rw_a2[0],
        rw_g1=rw_g1[0], rw_g2=rw_g2[0],
        rw_ln=jnp.concatenate([rw_ln_w[0][None], rw_ln_b[0][None], zeros, zeros], axis=0),
        rw_w_o=rw_w_o[0].astype(BF16),
        moe_wr=jnp.pad(jnp.concatenate([moe_w_rg, moe_w_re], axis=-1), ((0, 0), (0, 0), (0, pad))),
        moe_br=jnp.pad(jnp.concatenate([moe_b_rg, moe_b_re], axis=-1), ((0, 0), (0, pad)))[:, None, :],
        moe_w_gu=moe_w_gu.reshape(depth, N_EXPERTS, d, 2 * D_EXPERT),
        moe_w_down=moe_w_down.reshape(depth, N_EXPERTS, D_EXPERT, d),
        w_kv=w_kv.astype(BF16), at_w_q=at_w_q[0].astype(BF16), at_w_o=at_w_o[0].astype(BF16),
    )


def kernel(x_prompt, x_sample, state_wkv, state_shift, cache_k, cache_v, c_prompt, c_sample, w_ada, b_ada, g_norm, rw_mu, rw_w_rkv, rw_w0, rw_w1, rw_w2, rw_a0, rw_a1, rw_a2, rw_g1, rw_g2, rw_k_k, rw_k_a, rw_r_k, rw_ln_w, rw_ln_b, rw_w_o, moe_w_rg, moe_b_rg, moe_w_re, moe_b_re, moe_w_gu, moe_w_down, w_ada_kv, b_ada_kv, g_kv, w_kv, at_w_q, at_w_o, rel_bias, g_final):
    d = D_MODEL
    bp, tp = x_prompt.shape[:2]
    nb = x_sample.shape[0]
    p = _prepare_params(w_ada, b_ada, g_norm, rw_mu, rw_w_rkv, rw_w0, rw_w1, rw_w2, rw_a0, rw_a1, rw_a2, rw_g1,
                        rw_g2, rw_k_k, rw_k_a, rw_r_k, rw_ln_w, rw_ln_b, rw_w_o, moe_w_rg, moe_b_rg, moe_w_re,
                        moe_b_re, moe_w_gu, moe_w_down, w_ada_kv, b_ada_kv, g_kv, w_kv, at_w_q, at_w_o, rel_bias,
                        g_final)

    n_c = bp + nb
    n_pad = -n_c % 8
    c_all = jnp.concatenate([c_prompt, c_sample, jnp.zeros((n_pad, d), F32)], axis=0)
    mod_l = [_ada_linear(c_all, w_ada[l], b_ada[l]) for l in range(2)]
    mod_kv = _ada_linear(c_all, w_ada_kv, b_ada_kv)

    def split(m, n, lo, hi, per_batch):
        parts = [m[lo:hi, d * j:d * (j + 1)] for j in range(n)]
        return [q[:, None, :] if per_batch else q[None] for q in parts]

    mods_p = dict(l0=split(mod_l[0], 6, 0, bp, True), l1=split(mod_l[1], 6, 0, bp, True),
                  kv=split(mod_kv, 2, 0, bp, True))
    mods_s = dict(l0=split(mod_l[0], 6, bp, n_c, False), l1=split(mod_l[1], 6, bp, n_c, False),
                  kv=split(mod_kv, 2, bp, n_c, False))

    y_p, wkv_p, shift_p, k_p, v_p = _trunk(x_prompt, mods_p, None, None, None, p, tm=256, tm_moe=256)
    keep = min(BAND * max(DILATIONS), tp)
    k_prompt = k_p[:, tp - keep:].reshape(bp, keep, N_HEADS, HEAD)
    v_prompt = v_p[:, tp - keep:].reshape(bp, keep, N_HEADS, HEAD)

    xs = x_sample.reshape(1, nb, d)
    y_s, wkv_s, shift_s, k_s, v_s = _trunk(xs, mods_s, state_wkv[0], state_shift[0][None], (cache_k, cache_v), p,
                                           tm=nb, tm_moe=nb)
    return (y_p, y_s.reshape(nb, 1, d), wkv_p[None], shift_p[None], k_prompt, v_prompt,
            wkv_s[None], shift_s[None], k_s.reshape(nb, 1, N_HEADS, HEAD), v_s.reshape(nb, 1, N_HEADS, HEAD))
```

```python
import functools

import numpy as np
import jax
import jax.numpy as jnp
from jax import lax
from jax.experimental import pallas as pl
from jax.experimental.pallas import tpu as pltpu

F32 = jnp.float32
BF16 = jnp.bfloat16

D_MODEL = 1024
N_HEADS = 16
HEAD = 64
LANES = 128
N_PAIRS = D_MODEL // LANES
N_MIX = 6
GN_EPS = 64e-5
RMS_EPS = 1e-6
NEG_INF = -1e30
DILATIONS = (1, 4, 16)
BAND = 128
QB = 128
SCALE_B = HEAD ** -0.5
NUM_BUCKETS = 32
MAX_DISTANCE = 2048
N_GROUPS_E = 4
N_EXP_PER_GROUP = 8
N_EXPERTS = N_GROUPS_E * N_EXP_PER_GROUP
D_EXPERT = 512
ROUTER_LANE0 = N_GROUPS_E
SCAN_CHUNK = 64

VMEM_LIMIT = 56 * 1024 * 1024


def _cparams(sem):
    return pltpu.CompilerParams(dimension_semantics=sem, vmem_limit_bytes=VMEM_LIMIT)


def _dot(a, b):
    return jnp.dot(a.astype(BF16), b.astype(BF16), preferred_element_type=F32)


def _dot_nt(a, b):
    return lax.dot_general(a.astype(BF16), b.astype(BF16), (((1,), (1,)), ((), ())),
                           preferred_element_type=F32)


def _split3(a):
    hi = a.astype(BF16)
    r1 = a - hi.astype(F32)
    mid = r1.astype(BF16)
    lo = (r1 - mid.astype(F32)).astype(BF16)
    return hi, mid, lo


def _dot3(a, b01):
    hi, mid, lo = _split3(a)
    b = b01.astype(BF16)
    return (jnp.dot(hi, b, preferred_element_type=F32) + jnp.dot(mid, b, preferred_element_type=F32)
            + jnp.dot(lo, b, preferred_element_type=F32))


def _dot3_left(a01, b):
    hi, mid, lo = _split3(b)
    a = a01.astype(BF16)
    return (jnp.dot(a, hi, preferred_element_type=F32) + jnp.dot(a, mid, preferred_element_type=F32)
            + jnp.dot(a, lo, preferred_element_type=F32))


def _sigmoid(x):
    return 1.0 / (1.0 + jnp.exp(-x))


def _silu(x):
    return x * _sigmoid(x)


def _softplus(z):
    return jnp.maximum(z, 0.0) + jnp.log(1.0 + jnp.exp(-jnp.abs(z)))


def _norm_mod(x, g, sh, sc):
    ms = jnp.mean(x * x, axis=-1, keepdims=True)
    return (x * lax.rsqrt(ms + RMS_EPS) * g) * (1.0 + sc) + sh


def _head_sum_matrix():
    r = lax.broadcasted_iota(jnp.int32, (LANES, LANES), 0) >= HEAD
    c = lax.broadcasted_iota(jnp.int32, (LANES, LANES), 1) >= HEAD
    return jnp.where(r == c, 1.0, 0.0).astype(BF16)


def _head_sum(z, m128):
    parts = [_dot3(z[:, LANES * p:LANES * (p + 1)], m128) for p in range(N_PAIRS)]
    return jnp.concatenate(parts, axis=1)


def _row_spec(arr, tm):
    n = arr.shape[-1]
    if arr.shape[1] == 1:
        return pl.BlockSpec((1, 1, n), lambda b, i: (b, 0, 0))
    return pl.BlockSpec((1, tm, n), lambda b, i: (b, i, 0))


def _const_spec(arr):
    nd = arr.ndim
    return pl.BlockSpec(arr.shape, lambda b, i, _nd=nd: (0,) * _nd)


def _ada_kernel(c_ref, w_ref, b_ref, o_ref):
    o_ref[...] = _dot(_silu(c_ref[...]), w_ref[...]) + b_ref[...]


def _ada_linear(c_all, w, b):
    m, k = c_all.shape
    n = w.shape[1]
    tn = 1024
    return pl.pallas_call(
        _ada_kernel,
        out_shape=jax.ShapeDtypeStruct((m, n), F32),
        grid=(n // tn,),
        in_specs=[pl.BlockSpec((m, k), lambda j: (0, 0)),
                  pl.BlockSpec((k, tn), lambda j: (0, j)),
                  pl.BlockSpec((1, tn), lambda j: (0, j))],
        out_specs=pl.BlockSpec((m, tn), lambda j: (0, j)),
        compiler_params=_cparams(("parallel",)),
        name="ada_linear",
    )(c_all, w, b.reshape(1, n))


def _norm_mod_kernel(x_ref, g_ref, sh_ref, sc_ref, o_ref):
    o_ref[0] = _norm_mod(x_ref[0], g_ref[...], sh_ref[0], sc_ref[0])


def _norm_mod_call(x, g, sh, sc, tm):
    b, t, d = x.shape
    g2 = g.reshape(1, d)
    return pl.pallas_call(
        _norm_mod_kernel,
        out_shape=jax.ShapeDtypeStruct(x.shape, F32),
        grid=(b, t // tm),
        in_specs=[_row_spec(x, tm), _const_spec(g2), _row_spec(sh, tm), _row_spec(sc, tm)],
        out_specs=pl.BlockSpec((1, tm, d), lambda bb, i: (bb, i, 0)),
        compiler_params=_cparams(("parallel", "parallel")),
        name="norm_mod",
    )(x, g2, sh, sc)


def _final_norm_kernel(x_ref, g_ref, o_ref):
    x = x_ref[0]
    ms = jnp.mean(x * x, axis=-1, keepdims=True)
    o_ref[0] = x * lax.rsqrt(ms + RMS_EPS) * g_ref[...]


def _final_norm_call(x, g, tm):
    b, t, d = x.shape
    g2 = g.reshape(1, d)
    return pl.pallas_call(
        _final_norm_kernel,
        out_shape=jax.ShapeDtypeStruct(x.shape, F32),
        grid=(b, t // tm),
        in_specs=[_row_spec(x, tm), _const_spec(g2)],
        out_specs=pl.BlockSpec((1, tm, d), lambda bb, i: (bb, i, 0)),
        compiler_params=_cparams(("parallel", "parallel")),
        name="final_norm",
    )(x, g2)


def _linear_kernel(*refs, has_norm, has_resid):
    it = iter(refs)
    x_ref = next(it)
    w_ref = next(it)
    if has_norm:
        g_ref, sh_ref, sc_ref = next(it), next(it), next(it)
    if has_resid:
        xr_ref, gt_ref = next(it), next(it)
    o_ref = next(it)
    x = x_ref[0]
    if has_norm:
        x = _norm_mod(x, g_ref[...], sh_ref[0], sc_ref[0])
    out = _dot(x, w_ref[...])
    if has_resid:
        out = xr_ref[0] + gt_ref[0] * out
    o_ref[0] = out


def _linear_call(x, w, tm, norm=None, resid=None, name="linear"):
    b, t, k = x.shape
    n = w.shape[1]
    args = [x, w]
    specs = [_row_spec(x, tm), _const_spec(w)]
    if norm is not None:
        g, sh, sc = norm
        g2 = g.reshape(1, k)
        args += [g2, sh, sc]
        specs += [_const_spec(g2), _row_spec(sh, tm), _row_spec(sc, tm)]
    if resid is not None:
        xr, gt = resid
        args += [xr, gt]
        specs += [_row_spec(xr, tm), _row_spec(gt, tm)]
    return pl.pallas_call(
        functools.partial(_linear_kernel, has_norm=norm is not None, has_resid=resid is not None),
        out_shape=jax.ShapeDtypeStruct((b, t, n), F32),
        grid=(b, t // tm),
        in_specs=specs,
        out_specs=pl.BlockSpec((1, tm, n), lambda bb, i: (bb, i, 0)),
        compiler_params=_cparams(("parallel", "parallel")),
        name=name,
    )(*args)


def _fill_tiles(scr, val):
    for c in range(val.shape[1] // LANES):
        scr[c] = val[:, LANES * c:LANES * (c + 1)]


def _store_dilated(scr, out_ref, dil):
    n_tiles, tm, _ = scr.shape
    n = n_tiles * LANES
    for r in range(dil):
        rows = [scr[c, pl.ds(r, tm // dil, stride=dil), :] for c in range(n_tiles)]
        out_ref[0, :, n * r:n * (r + 1)] = jnp.concatenate(rows, axis=1).astype(out_ref.dtype)


def _load_dilated(in_ref, scr, dil):
    n_tiles, tm, _ = scr.shape
    n = n_tiles * LANES
    for r in range(dil):
        blk = in_ref[0, :, n * r:n * (r + 1)].astype(F32)
        for c in range(n_tiles):
            scr[c, pl.ds(r, tm // dil, stride=dil), :] = blk[:, LANES * c:LANES * (c + 1)]
    return jnp.concatenate([scr[c] for c in range(n_tiles)], axis=1)


def _qproj_kernel(x_ref, g_ref, sh_ref, sc_ref, w_ref, q0_ref, q1_ref, q2_ref, s1, s2):
    d = D_MODEL
    q = _dot(_norm_mod(x_ref[0], g_ref[...], sh_ref[0], sc_ref[0]), w_ref[...]) * SCALE_B
    q0_ref[0] = q[:, :d].astype(q0_ref.dtype)
    _fill_tiles(s1, q[:, d:2 * d])
    _store_dilated(s1, q1_ref, DILATIONS[1])
    _fill_tiles(s2, q[:, 2 * d:])
    _store_dilated(s2, q2_ref, DILATIONS[2])


def _dilated_shape(b, t, n, dil, dtype):
    return jax.ShapeDtypeStruct((b, t // dil, dil * n), dtype)


def _dilated_spec(tm, n, dil):
    return pl.BlockSpec((1, tm // dil, dil * n), lambda bb, i: (bb, i, 0))


def _qproj_call(x, w, g, sh, sc, tm):
    b, t, d = x.shape
    g2 = g.reshape(1, d)
    tile = pltpu.VMEM((N_PAIRS, tm, LANES), F32)
    return pl.pallas_call(
        _qproj_kernel,
        out_shape=[_dilated_shape(b, t, d, dil, BF16) for dil in DILATIONS],
        grid=(b, t // tm),
        in_specs=[_row_spec(x, tm), _const_spec(g2), _row_spec(sh, tm), _row_spec(sc, tm), _const_spec(w)],
        out_specs=[_dilated_spec(tm, d, dil) for dil in DILATIONS],
        scratch_shapes=[tile, tile],
        compiler_params=_cparams(("parallel", "parallel")),
        name="q_proj",
    )(x, g2, sh, sc, w)


def _kvproj_kernel(x_ref, g_ref, sh_ref, sc_ref, w_ref, k_ref, v_ref, k1_ref, v1_ref, k2_ref, v2_ref, sk, sv):
    d = D_MODEL
    kv = _dot(_norm_mod(x_ref[0], g_ref[...], sh_ref[0], sc_ref[0]), w_ref[...])
    k_ref[0] = kv[:, :d]
    v_ref[0] = kv[:, d:]
    _fill_tiles(sk, kv[:, :d])
    _fill_tiles(sv, kv[:, d:])
    _store_dilated(sk, k1_ref, DILATIONS[1])
    _store_dilated(sv, v1_ref, DILATIONS[1])
    _store_dilated(sk, k2_ref, DILATIONS[2])
    _store_dilated(sv, v2_ref, DILATIONS[2])


def _kvproj_call(x, w, g, sh, sc, tm):
    b, t, d = x.shape
    g2 = g.reshape(1, d)
    tile = pltpu.VMEM((N_PAIRS, tm, LANES), F32)
    plain = jax.ShapeDtypeStruct((b, t, d), F32)
    return pl.pallas_call(
        _kvproj_kernel,
        out_shape=[plain, plain] + [_dilated_shape(b, t, d, dil, BF16) for dil in DILATIONS[1:] for _ in range(2)],
        grid=(b, t // tm),
        in_specs=[_row_spec(x, tm), _const_spec(g2), _row_spec(sh, tm), _row_spec(sc, tm), _const_spec(w)],
        out_specs=[_dilated_spec(tm, d, 1)] * 2 + [_dilated_spec(tm, d, dil) for dil in DILATIONS[1:] for _ in range(2)],
        scratch_shapes=[tile, tile],
        compiler_params=_cparams(("parallel", "parallel")),
        name="kv_proj",
    )(x, g2, sh, sc, w)


PREV_ROWS = 8


def _rwkv_proj_kernel(x_ref, aux_ref, gn_ref, sh_ref, sc_ref, mu_ref, vec_ref, wrkv_ref, w1_ref, w2_ref, a1_ref, a2_ref,
                      g1_ref, g2_ref, r_ref, lw_ref, k_ref, v_ref, a_ref, b_ref, g_ref, bonus_ref, *, sequence):
    h = _norm_mod(x_ref[0], gn_ref[...], sh_ref[0], sc_ref[0])
    if sequence:
        prev = _norm_mod(aux_ref[0], gn_ref[...], sh_ref[0], sc_ref[0])[PREV_ROWS - 1:PREV_ROWS, :]
        prev = jnp.where(pl.program_id(1) > 0, prev, 0.0)
        row = lax.broadcasted_iota(jnp.int32, h.shape, 0)
        hp = jnp.where(row == 0, prev, pltpu.roll(h, 1, axis=0))
    else:
        hp = aux_ref[0]
    xx = hp - h

    def mix(i):
        return h + xx * mu_ref[i:i + 1, :]

    w0, a0, k_k, k_a, r_k = (vec_ref[i:i + 1, :] for i in range(5))
    m128 = _head_sum_matrix()
    r = _dot(mix(0), wrkv_ref[0])
    k = _dot(mix(1), wrkv_ref[1])
    v = _dot(mix(2), wrkv_ref[2])
    wl = w0 + _dot(jnp.tanh(_dot(mix(3), w1_ref[...])), w2_ref[...])
    w_log = -_softplus(-wl) - 0.5
    a_sig = _sigmoid(a0 + _dot(_dot(mix(4), a1_ref[...]), a2_ref[...]))
    g = _dot(_sigmoid(_dot(mix(5), g1_ref[...])), g2_ref[...])
    kk = k * k_k
    kk = kk / jnp.maximum(jnp.sqrt(_head_sum(kk * kk, m128)), 1e-12)
    k2 = k * (1.0 + (a_sig - 1.0) * k_a)
    r_ref[0] = r
    lw_ref[0] = -jnp.exp(w_log)
    k_ref[0] = k2
    v_ref[0] = v
    a_ref[0] = -kk
    b_ref[0] = kk * a_sig
    g_ref[0] = g
    bonus_ref[0] = _head_sum(r * k2 * r_k, m128) * v


def _rwkv_proj_call(x, shift, gn, sh, sc, mu, vecs, wrkv, w1, w2, a1, a2, g1, g2, tm):
    b, t, d = x.shape
    gn2 = gn.reshape(1, d)
    consts = [mu, vecs, wrkv, w1, w2, a1, a2, g1, g2]
    out = jax.ShapeDtypeStruct((b, t, d), F32)
    ospec = pl.BlockSpec((1, tm, d), lambda bb, i: (bb, i, 0))
    if shift is None:
        per = tm // PREV_ROWS
        aux, aux_spec = x, pl.BlockSpec((1, PREV_ROWS, d), lambda bb, i: (bb, jnp.maximum(i * per - 1, 0), 0))
    else:
        aux, aux_spec = shift, _row_spec(shift, tm)
    return pl.pallas_call(
        functools.partial(_rwkv_proj_kernel, sequence=shift is None),
        out_shape=[out] * 8,
        grid=(b, t // tm),
        in_specs=[_row_spec(x, tm), aux_spec, _const_spec(gn2), _row_spec(sh, tm), _row_spec(sc, tm)]
                 + [_const_spec(c) for c in consts],
        out_specs=[ospec] * 8,
        compiler_params=_cparams(("parallel", "parallel")),
        name="rwkv_proj",
    )(x, aux, gn2, sh, sc, *consts)


def _rwkv_scan_kernel(r_ref, lw_ref, k_ref, v_ref, a_ref, b_ref, y_ref, hout_ref, h_scr):
    c = pl.program_id(0)
    nc = pl.num_programs(0)
    nb = r_ref.shape[0]
    C = SCAN_CHUNK
    n2 = 2 * C

    @pl.when(c == 0)
    def _():
        h_scr[...] = jnp.zeros_like(h_scr)

    ri = lax.broadcasted_iota(jnp.int32, (C, C), 0)
    ci = lax.broadcasted_iota(jnp.int32, (C, C), 1)
    tril = jnp.where(ri >= ci, 1.0, 0.0).astype(BF16)
    head0 = lax.broadcasted_iota(jnp.int32, (C, LANES), 1) < HEAD
    tt = lax.broadcasted_iota(jnp.int32, (n2, n2), 0) & (C - 1)
    ss = lax.broadcasted_iota(jnp.int32, (n2, n2), 1) & (C - 1)
    strict = tt > ss
    incl = tt >= ss
    n_double = C.bit_length() - 1

    def two(xs):
        return jnp.concatenate([jnp.where(head0, xs, 0.0), jnp.where(head0, 0.0, xs)], axis=0)

    ar, bk, v2, w2t, gcol = [], [], [], [], []
    for bi in range(nb):
        lw = lw_ref[bi]
        cum = _dot3_left(tril, lw)
        gam = jnp.exp(cum)
        ginv = jnp.exp(-cum)
        g_last = gam[C - 1:C, :]
        rt = r_ref[bi] * gam
        at = a_ref[bi] * jnp.exp(cum - lw)
        bt = b_ref[bi] * ginv
        kt = k_ref[bi] * ginv
        bh = bt * g_last
        kh = kt * g_last
        v = v_ref[bi]
        for p in range(N_PAIRS):
            sl = slice(LANES * p, LANES * (p + 1))
            ar.append(jnp.concatenate([two(at[:, sl]), two(rt[:, sl])], axis=0).astype(BF16))
            bk.append(jnp.concatenate([two(bt[:, sl]), two(kt[:, sl])], axis=0).astype(BF16))
            v2.append(two(v[:, sl]).astype(BF16))
            w2t.append(jnp.transpose(jnp.concatenate([two(bh[:, sl]), two(kh[:, sl])], axis=0)).astype(BF16))
            gcol.append(jnp.transpose(jnp.broadcast_to(g_last[:, sl], (LANES, LANES))))
    n = len(ar)
    hp = [h_scr[i] for i in range(n)]
    s = [_dot_nt(ar[i], bk[i]) for i in range(n)]
    arh = [_dot(ar[i], hp[i]) for i in range(n)]
    pw = [jnp.where(strict, s[i][:n2, :n2], 0.0).astype(BF16) for i in range(n)]
    lt = [jnp.concatenate([jnp.where(strict, s[i][:n2, n2:], 0.0), jnp.where(incl, s[i][n2:, n2:], 0.0)],
                          axis=0) for i in range(n)]
    t_rb = [jnp.where(incl, s[i][n2:, :n2], 0.0) for i in range(n)]
    lv = [_dot(lt[i], v2[i]) for i in range(n)]
    u = [arh[i][:n2] + lv[i][:n2] for i in range(n)]
    for it in range(n_double):
        pu = [_dot(pw[i], u[i]) for i in range(n)]
        if it + 1 < n_double:
            pw = [_dot(pw[i], pw[i]).astype(BF16) for i in range(n)]
        u = [u[i] + pu[i] for i in range(n)]
    tu = [_dot(t_rb[i], u[i]) for i in range(n)]
    hn = [_dot(w2t[i], jnp.concatenate([u[i].astype(BF16), v2[i]], axis=0)) for i in range(n)]
    for i in range(n):
        h_scr[i] = gcol[i] * hp[i] + hn[i]
    for bi in range(nb):
        ys = []
        for p in range(N_PAIRS):
            i = bi * N_PAIRS + p
            y2 = arh[i][n2:] + lv[i][n2:] + tu[i]
            ys.append(y2[:C] + y2[C:])
        y_ref[bi] = jnp.concatenate(ys, axis=1)

    @pl.when(c == nc - 1)
    def _():
        hout_ref[...] = h_scr[...]


def _rwkv_scan_call(r, lw, k, v, a, bb):
    b, t, d = r.shape
    C = SCAN_CHUNK
    spec = pl.BlockSpec((b, C, d), lambda c: (0, c, 0))
    y, hout = pl.pallas_call(
        _rwkv_scan_kernel,
        out_shape=[jax.ShapeDtypeStruct((b, t, d), F32),
                   jax.ShapeDtypeStruct((b * N_PAIRS, LANES, LANES), F32)],
        grid=(t // C,),
        in_specs=[spec] * 6,
        out_specs=[spec, pl.BlockSpec((b * N_PAIRS, LANES, LANES), lambda c: (0, 0, 0))],
        scratch_shapes=[pltpu.VMEM((b * N_PAIRS, LANES, LANES), F32)],
        compiler_params=_cparams(("arbitrary",)),
        name="rwkv_scan",
    )(r, lw, k, v, a, bb)
    hh = hout.reshape(b, N_PAIRS, 2, HEAD, 2, HEAD)
    hh = jnp.stack([hh[:, :, 0, :, 0, :], hh[:, :, 1, :, 1, :]], axis=2).reshape(b, N_HEADS, HEAD, HEAD)
    return y, jnp.swapaxes(hh, -1, -2)


def _rwkv_step_kernel(s_ref, z_ref, so_ref, y_ref):
    r, lw, k, v, a, b = (z_ref[c, 0] for c in range(6))
    w = jnp.exp(lw)
    ys = []
    for i in range(HEAD):
        s = s_ref[0, i]
        sa = jnp.sum(s * a, axis=0, keepdims=True)
        sn = s * w + sa * b + v[i:i + 1, :] * k
        so_ref[0, i] = sn
        ys.append(jnp.sum(sn * r, axis=0, keepdims=True))
    y_ref[0] = jnp.concatenate(ys, axis=0)


def _rwkv_step_call(state, r, lw, k, v, a, bb):
    nb = state.shape[0]
    st = jnp.transpose(state, (1, 2, 3, 0))
    z = jnp.stack([jnp.transpose(q[0]).reshape(N_HEADS, HEAD, nb) for q in (r, lw, k, v, a, bb)])
    so, y = pl.pallas_call(
        _rwkv_step_kernel,
        out_shape=[jax.ShapeDtypeStruct((N_HEADS, HEAD, HEAD, nb), F32),
                   jax.ShapeDtypeStruct((N_HEADS, HEAD, nb), F32)],
        grid=(N_HEADS,),
        in_specs=[pl.BlockSpec((1, HEAD, HEAD, nb), lambda h: (h, 0, 0, 0)),
                  pl.BlockSpec((6, 1, HEAD, nb), lambda h: (0, h, 0, 0))],
        out_specs=[pl.BlockSpec((1, HEAD, HEAD, nb), lambda h: (h, 0, 0, 0)),
                   pl.BlockSpec((1, HEAD, nb), lambda h: (h, 0, 0))],
        compiler_params=_cparams(("parallel",)),
        name="rwkv_step",
    )(st, z)
    return jnp.transpose(y.reshape(D_MODEL, nb))[None], jnp.transpose(so, (3, 0, 1, 2))


def _rwkv_out_kernel(y_ref, bonus_ref, g_ref, x_ref, gt_ref, ln_ref, wo_ref, o_ref):
    y = y_ref[0]
    m128 = _head_sum_matrix()
    mean = _head_sum(y, m128) * (1.0 / HEAD)
    dlt = y - mean
    var = _head_sum(dlt * dlt, m128) * (1.0 / HEAD)
    yn = dlt * lax.rsqrt(var + GN_EPS) * ln_ref[0:1, :] + ln_ref[1:2, :]
    out = _dot((yn + bonus_ref[0]) * g_ref[0], wo_ref[...])
    o_ref[0] = x_ref[0] + gt_ref[0] * out


def _rwkv_out_call(y, bonus, g, x, gt, ln, wo, tm):
    b, t, d = y.shape
    return pl.pallas_call(
        _rwkv_out_kernel,
        out_shape=jax.ShapeDtypeStruct((b, t, d), F32),
        grid=(b, t // tm),
        in_specs=[_row_spec(y, tm), _row_spec(bonus, tm), _row_spec(g, tm), _row_spec(x, tm), _row_spec(gt, tm),
                  _const_spec(ln), _const_spec(wo)],
        out_specs=pl.BlockSpec((1, tm, d), lambda bb, i: (bb, i, 0)),
        compiler_params=_cparams(("parallel", "parallel")),
        name="rwkv_out",
    )(y, bonus, g, x, gt, ln, wo)


ROUTE_E, ROUTE_W, ROUTE_RANK = 0, 2, 4


def _router_kernel(x_ref, g_ref, sh_ref, sc_ref, wr_ref, br_ref, hf_ref, route_ref, route_t_ref, cnt_ref, carry):
    first = jnp.logical_and(pl.program_id(0) == 0, pl.program_id(1) == 0)

    @pl.when(first)
    def _():
        carry[...] = jnp.zeros_like(carry)

    hf = _norm_mod(x_ref[0], g_ref[...], sh_ref[0], sc_ref[0])
    hf_ref[0] = hf
    logits = jnp.dot(hf, wr_ref[...], preferred_element_type=F32, precision=lax.Precision.HIGHEST) + br_ref[...]
    lane = lax.broadcasted_iota(jnp.int32, logits.shape, 1).astype(F32)
    big = float(1 << 20)
    gl = jnp.where(lane < N_GROUPS_E, logits, NEG_INF)
    gmax = jnp.max(gl, axis=-1, keepdims=True)
    p_top = 1.0 / jnp.sum(jnp.exp(gl - gmax), axis=-1, keepdims=True)
    g_sel = jnp.min(jnp.where(gl == gmax, lane, big), axis=-1, keepdims=True)
    lo = ROUTER_LANE0 + N_EXP_PER_GROUP * g_sel
    el = jnp.where((lane >= lo) & (lane < lo + N_EXP_PER_GROUP), logits, NEG_INF)
    v1 = jnp.max(el, axis=-1, keepdims=True)
    i1 = jnp.min(jnp.where(el == v1, lane, big), axis=-1, keepdims=True)
    el2 = jnp.where(lane == i1, NEG_INF, el)
    v2 = jnp.max(el2, axis=-1, keepdims=True)
    i2 = jnp.min(jnp.where(el2 == v2, lane, big), axis=-1, keepdims=True)
    e21 = jnp.exp(v2 - v1)
    w1 = p_top / (1.0 + e21)
    w2 = p_top * e21 / (1.0 + e21)
    tm = logits.shape[0]
    onehot = jnp.where(lane == i1, 1.0, 0.0) + jnp.where(lane == i2, 1.0, 0.0)
    rr = lax.broadcasted_iota(jnp.int32, (tm, tm), 0)
    cc = lax.broadcasted_iota(jnp.int32, (tm, tm), 1)
    earlier = jnp.where(rr > cc, 1.0, 0.0).astype(BF16)
    before = jnp.dot(earlier, onehot.astype(BF16), preferred_element_type=F32) + carry[0:1, :]
    rank1 = jnp.sum(jnp.where(lane == i1, before, 0.0), axis=-1, keepdims=True)
    rank2 = jnp.sum(jnp.where(lane == i2, before, 0.0), axis=-1, keepdims=True)
    carry[...] = carry[...] + jnp.sum(onehot, axis=0, keepdims=True)
    cnt_ref[...] = carry[...]
    route = jnp.zeros_like(logits)
    for ln, val in ((ROUTE_E, i1 - ROUTER_LANE0), (ROUTE_E + 1, i2 - ROUTER_LANE0), (ROUTE_W, w1), (ROUTE_W + 1, w2),
                    (ROUTE_RANK, rank1), (ROUTE_RANK + 1, rank2)):
        route = jnp.where(lane == ln, val, route)
    route_ref[0] = route
    route_t_ref[...] = jnp.transpose(route)[:8, :]


def _router_call(x, g, sh, sc, wr, br, tm):
    b, t, d = x.shape
    g2 = g.reshape(1, d)
    nt = t // tm
    return pl.pallas_call(
        _router_kernel,
        out_shape=[jax.ShapeDtypeStruct((b, t, d), F32), jax.ShapeDtypeStruct((b, t, LANES), F32),
                   jax.ShapeDtypeStruct((8, b * t), F32), jax.ShapeDtypeStruct((8, LANES), F32)],
        grid=(b, nt),
        in_specs=[_row_spec(x, tm), _const_spec(g2), _row_spec(sh, tm), _row_spec(sc, tm),
                  _const_spec(wr), _const_spec(br)],
        out_specs=[pl.BlockSpec((1, tm, d), lambda bb, i: (bb, i, 0)),
                   pl.BlockSpec((1, tm, LANES), lambda bb, i: (bb, i, 0)),
                   pl.BlockSpec((8, tm), lambda bb, i: (0, bb * nt + i)),
                   pl.BlockSpec((8, LANES), lambda bb, i: (0, 0))],
        scratch_shapes=[pltpu.VMEM((8, LANES), F32)],
        compiler_params=_cparams(("arbitrary", "arbitrary")),
        name="moe_router",
    )(x, g2, sh, sc, wr, br)


def _route_plan(route_t, counts, tile):
    n = route_t.shape[1]
    e = route_t[ROUTE_E:ROUTE_E + 2].astype(jnp.int32)
    rank = route_t[ROUTE_RANK:ROUTE_RANK + 2].astype(jnp.int32)
    cnt = counts[0, ROUTER_LANE0:ROUTER_LANE0 + N_EXPERTS].astype(jnp.int32)
    nt = (cnt + tile - 1) // tile
    tend = jnp.cumsum(nt)
    tstart = tend - nt
    start = jnp.zeros_like(e)
    for ex in range(N_EXPERTS):
        start = jnp.where(e == ex, tstart[ex], start)
    dest = start * tile + rank
    n_tiles = (2 * n) // tile + N_EXPERTS
    tid = jnp.arange(n_tiles, dtype=jnp.int32)
    te = jnp.minimum(jnp.sum((tid[:, None] >= tend[None, :]).astype(jnp.int32), axis=1), N_EXPERTS - 1)
    active = (tid < tend[-1]).astype(jnp.int32)
    return dest.reshape(-1), te, active, n_tiles


def _dispatch_kernel(dest_ref, hf_ref, xs_in_ref, xs_ref, sem, *, tm):
    del xs_in_ref
    n = pl.num_programs(0) * tm
    base = pl.program_id(0) * tm

    def body(t, carry):
        row = hf_ref.at[pl.ds(t, 1), :]
        for c in range(2):
            dst = dest_ref[c * n + base + t]
            pltpu.make_async_copy(row, xs_ref.at[pl.ds(dst, 1), :], sem).start()
        return carry

    lax.fori_loop(0, tm, body, 0, unroll=8)
    done = xs_ref.at[pl.ds(0, 2 * tm), :]
    pltpu.make_async_copy(done, done, sem).wait()


def _dispatch_call(hf2, dest, n_rows, tm):
    n, d = hf2.shape
    xs0 = jnp.zeros((n_rows, d), F32)
    return pl.pallas_call(
        functools.partial(_dispatch_kernel, tm=tm),
        out_shape=jax.ShapeDtypeStruct((n_rows, d), F32),
        grid_spec=pltpu.PrefetchScalarGridSpec(
            num_scalar_prefetch=1,
            grid=(n // tm,),
            in_specs=[pl.BlockSpec((tm, d), lambda i, dest: (i, 0)),
                      pl.BlockSpec(memory_space=pl.ANY)],
            out_specs=pl.BlockSpec(memory_space=pl.ANY),
            scratch_shapes=[pltpu.SemaphoreType.DMA(())],
        ),
        input_output_aliases={2: 0},
        compiler_params=pltpu.CompilerParams(dimension_semantics=("arbitrary",), vmem_limit_bytes=VMEM_LIMIT,
                                             has_side_effects=True),
        name="moe_dispatch",
    )(dest, hf2, xs0)


def _expert_kernel(te_ref, act_ref, xs_ref, wgu_ref, wd_ref, o_ref, wgu_lp, wd_lp):
    i = pl.program_id(0)
    changed = jnp.logical_or(i == 0, te_ref[i] != te_ref[jnp.maximum(i - 1, 0)])

    @pl.when(changed)
    def _():
        wgu_lp[...] = wgu_ref[0].astype(wgu_lp.dtype)
        wd_lp[...] = wd_ref[0].astype(wd_lp.dtype)

    @pl.when(act_ref[i] == 1)
    def _():
        gu = _dot(xs_ref[...], wgu_lp[...])
        act = _silu(gu[:, :D_EXPERT]) * gu[:, D_EXPERT:]
        o_ref[...] = _dot(act, wd_lp[...])

    @pl.when(act_ref[i] == 0)
    def _():
        o_ref[...] = jnp.zeros_like(o_ref)


def _expert_call(xs, te, active, wgu, wd, tile):
    n_rows, d = xs.shape
    return pl.pallas_call(
        _expert_kernel,
        out_shape=jax.ShapeDtypeStruct((n_rows, d), F32),
        grid_spec=pltpu.PrefetchScalarGridSpec(
            num_scalar_prefetch=2,
            grid=(n_rows // tile,),
            in_specs=[pl.BlockSpec((tile, d), lambda i, te, act: (i, 0)),
                      pl.BlockSpec((1, d, 2 * D_EXPERT), lambda i, te, act: (te[i], 0, 0)),
                      pl.BlockSpec((1, D_EXPERT, d), lambda i, te, act: (te[i], 0, 0))],
            out_specs=pl.BlockSpec((tile, d), lambda i, te, act: (i, 0)),
            scratch_shapes=[pltpu.VMEM((d, 2 * D_EXPERT), BF16), pltpu.VMEM((D_EXPERT, d), BF16)],
        ),
        compiler_params=_cparams(("arbitrary",)),
        name="moe_experts",
    )(te, active, xs, wgu, wd)


def _combine_kernel(dest_ref, ys_ref, route_ref, x_ref, gt_ref, *rest, tm, final_norm):
    if final_norm:
        gf_ref, o_ref, buf, sem = rest
    else:
        o_ref, buf, sem = rest
    n_steps = pl.num_programs(0) * pl.num_programs(1)
    n = n_steps * tm
    step = pl.program_id(0) * pl.num_programs(1) + pl.program_id(1)

    def gather(s, slot):
        base = s * tm

        def body(t, carry):
            for c in range(2):
                src = dest_ref[c * n + base + t]
                pltpu.make_async_copy(ys_ref.at[pl.ds(src, 1), :], buf.at[slot, c, pl.ds(t, 1), :],
                                      sem.at[slot]).start()
            return carry

        lax.fori_loop(0, tm, body, 0, unroll=8)

    @pl.when(step == 0)
    def _():
        gather(0, 0)

    @pl.when(step + 1 < n_steps)
    def _():
        gather(step + 1, (step + 1) % 2)

    slot = step % 2
    pltpu.make_async_copy(buf.at[slot], buf.at[slot], sem.at[slot]).wait()
    route = route_ref[0]
    lane = lax.broadcasted_iota(jnp.int32, route.shape, 1)
    w1 = jnp.sum(jnp.where(lane == ROUTE_W, route, 0.0), axis=-1, keepdims=True)
    w2 = jnp.sum(jnp.where(lane == ROUTE_W + 1, route, 0.0), axis=-1, keepdims=True)
    out = x_ref[0] + gt_ref[0] * (w1 * buf[slot, 0] + w2 * buf[slot, 1])
    if final_norm:
        ms = jnp.mean(out * out, axis=-1, keepdims=True)
        out = out * lax.rsqrt(ms + RMS_EPS) * gf_ref[...]
    o_ref[0] = out


def _combine_call(ys, dest, route, x, gt, tm, final_g=None):
    b, t, d = x.shape
    extra = [] if final_g is None else [final_g.reshape(1, d)]

    def rs(arr):
        n = arr.shape[-1]
        if arr.shape[1] == 1:
            return pl.BlockSpec((1, 1, n), lambda bb, i, dest: (bb, 0, 0))
        return pl.BlockSpec((1, tm, n), lambda bb, i, dest: (bb, i, 0))

    return pl.pallas_call(
        functools.partial(_combine_kernel, tm=tm, final_norm=final_g is not None),
        out_shape=jax.ShapeDtypeStruct((b, t, d), F32),
        grid_spec=pltpu.PrefetchScalarGridSpec(
            num_scalar_prefetch=1,
            grid=(b, t // tm),
            in_specs=[pl.BlockSpec(memory_space=pl.ANY), rs(route), rs(x), rs(gt)]
                     + [pl.BlockSpec((1, d), lambda bb, i, dest: (0, 0)) for _ in extra],
            out_specs=pl.BlockSpec((1, tm, d), lambda bb, i, dest: (bb, i, 0)),
            scratch_shapes=[pltpu.VMEM((2, 2, tm, d), F32), pltpu.SemaphoreType.DMA((2,))],
        ),
        compiler_params=_cparams(("arbitrary", "arbitrary")),
        name="moe_combine",
    )(dest, ys, route, x, gt, *extra)


def _moe_call(x, g, sh, sc, gt, wr, br, wgu, wd, tm, tile, final_g=None):
    b, t, d = x.shape
    hf, route, route_t, counts = _router_call(x, g, sh, sc, wr, br, tm)
    dest, te, active, n_tiles = _route_plan(route_t, counts, tile)
    xs = _dispatch_call(hf.reshape(b * t, d), dest, n_tiles * tile, tm)
    ys = _expert_call(xs, te, active, wgu, wd, tile)
    return _combine_call(ys, dest, route, x, gt, tm, final_g)


def _t5_bucket(dist):
    dist = np.asarray(dist)
    max_exact = NUM_BUCKETS // 2
    log_ratio = np.log(np.maximum(dist, 1) / max_exact) / np.log(MAX_DISTANCE / max_exact)
    large = np.minimum(max_exact + (log_ratio * (NUM_BUCKETS - max_exact)).astype(np.int32), NUM_BUCKETS - 1)
    return np.where(dist < max_exact, dist, large).astype(np.int32)


def _attn_prompt_kernel(q_ref, kp_ref, kc_ref, vp_ref, vc_ref, bias_ref, o_ref, lse_ref):
    i = pl.program_id(2)
    q = q_ref[0].astype(BF16)
    kcat = jnp.concatenate([kp_ref[0], kc_ref[0]], axis=0).astype(BF16)
    vcat = jnp.concatenate([vp_ref[0], vc_ref[0]], axis=0).astype(BF16)
    head0 = lax.broadcasted_iota(jnp.int32, (QB, LANES), 1) < HEAD
    lane = lax.broadcasted_iota(jnp.int32, (QB, LANES), 1)
    zero = jnp.zeros((), BF16)
    first = jnp.where(i == 0, 1, 0)
    heads = range(N_HEADS)

    def masked_q(h):
        qp = q[:, LANES * (h // 2):LANES * (h // 2 + 1)]
        return jnp.where(head0, qp, zero) if h % 2 == 0 else jnp.where(head0, zero, qp)

    logits = [_dot_nt(masked_q(h), kcat[:, LANES * (h // 2):LANES * (h // 2 + 1)]) + bias_ref[first, h] for h in heads]
    m = [jnp.max(logits[h], axis=-1, keepdims=True) for h in heads]
    pr = [jnp.exp(logits[h] - m[h]) for h in heads]
    ssum = [jnp.sum(pr[h], axis=-1, keepdims=True) for h in heads]
    o = [_dot(pr[h], vcat[:, LANES * (h // 2):LANES * (h // 2 + 1)]) * (1.0 / ssum[h]) for h in heads]
    lse_tile = jnp.zeros((QB, LANES), F32)
    for h in heads:
        lse_tile = jnp.where(lane == h, m[h] + jnp.log(ssum[h]), lse_tile)
    o_ref[0] = jnp.concatenate([jnp.where(head0, o[2 * p], o[2 * p + 1]) for p in range(N_PAIRS)], axis=1)
    lse_ref[0] = lse_tile


def _attn_prompt_call(q, k, v, bias, dil):
    b, L, _ = k.shape
    nblk = L // QB
    d = D_MODEL
    cur = pl.BlockSpec((1, QB, d), lambda bb, r, i: (bb, i, r))
    prev = pl.BlockSpec((1, QB, d), lambda bb, r, i: (bb, jnp.maximum(i - 1, 0), r))
    return pl.pallas_call(
        _attn_prompt_kernel,
        out_shape=[jax.ShapeDtypeStruct((b, L, dil * d), F32), jax.ShapeDtypeStruct((b, L, dil * LANES), F32)],
        grid=(b, dil, nblk),
        in_specs=[cur, prev, cur, prev, cur, pl.BlockSpec(bias.shape, lambda bb, r, i: (0, 0, 0, 0))],
        out_specs=[cur, pl.BlockSpec((1, QB, LANES), lambda bb, r, i: (bb, i, r))],
        compiler_params=_cparams(("parallel", "parallel", "parallel")),
        name=f"attn_prompt_d{dil}",
    )(q, k, k, v, v, bias)


def _attn_merge_kernel(o0_ref, o1_ref, o2_ref, l0_ref, l1_ref, l2_ref, e_ref, wo_ref, x_ref, gt_ref, out_ref,
                       so1, so2, sl1, sl2):
    o1 = _load_dilated(o1_ref, so1, DILATIONS[1])
    o2 = _load_dilated(o2_ref, so2, DILATIONS[2])
    l0 = l0_ref[0]
    l1 = _load_dilated(l1_ref, sl1, DILATIONS[1])
    l2 = _load_dilated(l2_ref, sl2, DILATIONS[2])
    mx = jnp.maximum(jnp.maximum(l0, l1), l2)
    e0, e1, e2 = jnp.exp(l0 - mx), jnp.exp(l1 - mx), jnp.exp(l2 - mx)
    inv = 1.0 / (e0 + e1 + e2)
    em = e_ref[...]
    o = _dot3(e0 * inv, em) * o0_ref[0] + _dot3(e1 * inv, em) * o1 + _dot3(e2 * inv, em) * o2
    out_ref[0] = x_ref[0] + gt_ref[0] * _dot(o, wo_ref[...])


def _head_expand_matrix():
    return jnp.asarray(np.arange(LANES)[:, None] == (np.arange(D_MODEL) // HEAD)[None, :], BF16)


def _attn_merge_call(os_, ls_, wo, x, gt, tm):
    b, t, d = x.shape
    em = _head_expand_matrix()
    o_specs = [_dilated_spec(tm, d, dil) for dil in DILATIONS]
    l_specs = [_dilated_spec(tm, LANES, dil) for dil in DILATIONS]
    o_tile = pltpu.VMEM((N_PAIRS, tm, LANES), F32)
    l_tile = pltpu.VMEM((1, tm, LANES), F32)
    return pl.pallas_call(
        _attn_merge_kernel,
        out_shape=jax.ShapeDtypeStruct((b, t, d), F32),
        grid=(b, t // tm),
        in_specs=o_specs + l_specs + [_const_spec(em), _const_spec(wo), _row_spec(x, tm), _row_spec(gt, tm)],
        out_specs=pl.BlockSpec((1, tm, d), lambda bb, i: (bb, i, 0)),
        scratch_shapes=[o_tile, o_tile, l_tile, l_tile],
        compiler_params=_cparams(("parallel", "parallel")),
        name="attn_merge_out",
    )(*os_, *ls_, em, wo, x, gt)


SAMPLE_HEADS_PER_STEP = 8


def _attn_sample_kernel(q_ref, kn_ref, vn_ref, k_ref, v_ref, bias_ref, b0_ref, o_ref):
    n_buf = k_ref.shape[-1]
    hb = SAMPLE_HEADS_PER_STEP
    ng = len(DILATIONS)
    lane = lax.broadcasted_iota(jnp.int32, (HEAD, LANES), 1)
    los = [n_buf - BAND * dil for dil in DILATIONS]
    chains = [(hh, g) for hh in range(hb) for g in range(ng)]
    q = {(hh, g): q_ref[0, g, 0][:, hh:hh + 1] * SCALE_B for hh, g in chains}
    kn = [kn_ref[0, 0][:, hh:hh + 1] for hh in range(hb)]
    vn = [vn_ref[0, 0][:, hh:hh + 1] for hh in range(hb)]
    lk = {(hh, g): jnp.sum(k_ref[0, hh, :, los[g]:] * q[hh, g], axis=0, keepdims=True)
          + bias_ref[g, hh:hh + 1, los[g]:] for hh, g in chains}
    l0 = {(hh, g): jnp.sum(kn[hh] * q[hh, g], axis=0, keepdims=True) + b0_ref[g, 0][0:1, hh:hh + 1]
          for hh, g in chains}
    m = {c: jnp.maximum(jnp.max(lk[c], axis=1, keepdims=True), l0[c]) for c in chains}
    pk = {c: jnp.exp(lk[c] - m[c]) for c in chains}
    p0 = {c: jnp.exp(l0[c] - m[c]) for c in chains}
    ssum = {c: jnp.sum(pk[c], axis=1, keepdims=True) + p0[c] for c in chains}
    num = {(hh, g): jnp.sum(v_ref[0, hh, :, los[g]:] * pk[hh, g], axis=1, keepdims=True) + p0[hh, g] * vn[hh]
           for hh, g in chains}
    o_tile = jnp.zeros((HEAD, LANES), F32)
    for hh in range(hb):
        mx = jnp.maximum(jnp.maximum(m[hh, 0], m[hh, 1]), m[hh, 2])
        cs = [jnp.exp(m[hh, g] - mx) for g in range(ng)]
        inv = 1.0 / (cs[0] * ssum[hh, 0] + cs[1] * ssum[hh, 1] + cs[2] * ssum[hh, 2])
        o_col = (cs[0] * inv) * num[hh, 0] + (cs[1] * inv) * num[hh, 1] + (cs[2] * inv) * num[hh, 2]
        o_tile = jnp.where(lane == hh, o_col, o_tile)
    o_ref[0, 0] = o_tile


def _attn_sample_call(q, kn, vn, cache_k, cache_v, bias_p, bias_0):
    nb, n_buf = cache_k.shape[:2]
    assert n_buf == BAND * max(DILATIONS)
    hb = SAMPLE_HEADS_PER_STEP
    nhb = N_HEADS // hb
    ng = len(DILATIONS)
    kt = jnp.transpose(cache_k, (0, 2, 3, 1))
    vt = jnp.transpose(cache_v, (0, 2, 3, 1))
    qt = jnp.transpose(q.reshape(nb, ng, nhb, hb, HEAD), (0, 1, 2, 4, 3))
    knt = jnp.transpose(kn.reshape(nb, nhb, hb, HEAD), (0, 1, 3, 2))
    vnt = jnp.transpose(vn.reshape(nb, nhb, hb, HEAD), (0, 1, 3, 2))
    o = pl.pallas_call(
        _attn_sample_kernel,
        out_shape=jax.ShapeDtypeStruct((nb, nhb, HEAD, LANES), F32),
        grid=(nb, nhb),
        in_specs=[pl.BlockSpec((1, ng, 1, HEAD, hb), lambda b, j: (b, 0, j, 0, 0)),
                  pl.BlockSpec((1, 1, HEAD, hb), lambda b, j: (b, j, 0, 0)),
                  pl.BlockSpec((1, 1, HEAD, hb), lambda b, j: (b, j, 0, 0)),
                  pl.BlockSpec((1, hb, HEAD, n_buf), lambda b, j: (b, j, 0, 0)),
                  pl.BlockSpec((1, hb, HEAD, n_buf), lambda b, j: (b, j, 0, 0)),
                  pl.BlockSpec((ng, hb, n_buf), lambda b, j: (0, j, 0)),
                  pl.BlockSpec((ng, 1, 8, hb), lambda b, j: (0, j, 0, 0))],
        out_specs=pl.BlockSpec((1, 1, HEAD, LANES), lambda b, j: (b, j, 0, 0)),
        compiler_params=_cparams(("parallel", "parallel")),
        name="attn_sample",
    )(qt, knt, vnt, kt, vt, bias_p, bias_0)
    return jnp.transpose(o[..., :hb], (0, 1, 3, 2)).reshape(nb, N_HEADS * HEAD)


def _prompt_bias(rel_bias, gi, dil):
    table = rel_bias[:, gi * N_HEADS:(gi + 1) * N_HEADS].astype(F32)
    per_dist = table[_t5_bucket(np.arange(BAND + 1) * dil)]
    delta = (lax.broadcasted_iota(jnp.int32, (QB, 2 * QB), 0) + QB
             - lax.broadcasted_iota(jnp.int32, (QB, 2 * QB), 1))
    onehot = (delta[:, :, None] == jnp.arange(BAND + 1, dtype=jnp.int32)[None, None, :]).astype(F32)
    bias = jnp.einsum("qkj,jh->hqk", onehot, per_dist, precision=lax.Precision.HIGHEST)
    in_band = (delta >= 0) & (delta <= BAND)
    bias = jnp.where(in_band[None], bias, NEG_INF)
    has_prev = lax.broadcasted_iota(jnp.int32, (QB, 2 * QB), 1) >= QB
    return jnp.stack([bias, jnp.where(has_prev[None], bias, NEG_INF)])


def _sample_bias(rel_bias, n_buf):
    hb = SAMPLE_HEADS_PER_STEP
    dist = n_buf - np.arange(n_buf)
    bps, b0s = [], []
    for gi, dil in enumerate(DILATIONS):
        table = rel_bias[:, gi * N_HEADS:(gi + 1) * N_HEADS].astype(F32)
        in_window = (dist % dil == 0) & (dist <= BAND * dil)
        bps.append(jnp.where(in_window[None, :], jnp.transpose(table[_t5_bucket(dist)]), NEG_INF))
        b0s.append(jnp.broadcast_to(table[0].reshape(N_HEADS // hb, 1, hb), (N_HEADS // hb, 8, hb)))
    return jnp.stack(bps), jnp.stack(b0s)


def _trunk(x, mods, wkv0, shift0, caches, p, tm, tm_moe):
    b, t, d = x.shape
    m0, m1, mkv = mods["l0"], mods["l1"], mods["kv"]

    sh_m, sc_m, gt_m, sh_c, sc_c, gt_c = m0
    r, lw, k, v, a, bb, g, bonus = _rwkv_proj_call(x, shift0, p["g_norm"][0, 0], sh_m, sc_m, p["rw_mu"], p["rw_vecs"],
                                                   p["rw_w_rkv"], p["rw_w1"], p["rw_w2"], p["rw_a1"], p["rw_a2"],
                                                   p["rw_g1"], p["rw_g2"], tm)
    if wkv0 is None:
        y, wkv = _rwkv_scan_call(r, lw, k, v, a, bb)
        shift = _norm_mod_call(x[:, t - PREV_ROWS:], p["g_norm"][0, 0], sh_m, sc_m, PREV_ROWS)[:, -1]
    else:
        y, wkv = _rwkv_step_call(wkv0, r, lw, k, v, a, bb)
        shift = _norm_mod_call(x, p["g_norm"][0, 0], sh_m, sc_m, tm)[0]
    x = _rwkv_out_call(y, bonus, g, x, gt_m, p["rw_ln"], p["rw_w_o"], tm)
    x = _moe_call(x, p["g_norm"][0, 1], sh_c, sc_c, gt_c, p["moe_wr"][0], p["moe_br"][0],
                  p["moe_w_gu"][0], p["moe_w_down"][0], tm, tm_moe)

    sh_k, sc_k = mkv
    sh_m, sc_m, gt_m, sh_c, sc_c, gt_c = m1
    if caches is None:
        k_sh, v_sh, k1, v1, k2, v2 = _kvproj_call(x, p["w_kv"], p["g_kv"], sh_k, sc_k, tm)
        qs = _qproj_call(x, p["at_w_q"], p["g_norm"][1, 0], sh_m, sc_m, tm)
        ks, vs = (k_sh, k1, k2), (v_sh, v1, v2)
        os_, ls_ = [], []
        for gi, dil in enumerate(DILATIONS):
            o, lse = _attn_prompt_call(qs[gi], ks[gi], vs[gi], _prompt_bias(p["rel_bias"], gi, dil), dil)
            os_.append(o)
            ls_.append(lse)
        x = _attn_merge_call(os_, ls_, p["at_w_o"], x, gt_m, tm)
    else:
        kvp = _linear_call(x, p["w_kv"], tm, norm=(p["g_kv"], sh_k, sc_k), name="kv_proj")
        k_sh, v_sh = kvp[..., :d], kvp[..., d:]
        q = _linear_call(x, p["at_w_q"], tm, norm=(p["g_norm"][1, 0], sh_m, sc_m), name="q_proj")
        cache_k, cache_v = caches
        nb, n_buf = cache_k.shape[:2]
        bias_p, bias_0 = _sample_bias(p["rel_bias"], n_buf)
        o = _attn_sample_call(q.reshape(nb, len(DILATIONS), N_HEADS, HEAD), k_sh.reshape(nb, N_HEADS, HEAD),
                              v_sh.reshape(nb, N_HEADS, HEAD), cache_k, cache_v, bias_p, bias_0).reshape(b, t, d)
        x = _linear_call(o, p["at_w_o"], tm, resid=(x, gt_m), name="attn_out")
    y = _moe_call(x, p["g_norm"][1, 1], sh_c, sc_c, gt_c, p["moe_wr"][1], p["moe_br"][1],
                  p["moe_w_gu"][1], p["moe_w_down"][1], tm, tm_moe, final_g=p["g_final"])
    return y, wkv, shift, k_sh, v_sh


def _prepare_params(w_ada, b_ada, g_norm, rw_mu, rw_w_rkv, rw_w0, rw_w1, rw_w2, rw_a0, rw_a1, rw_a2, rw_g1, rw_g2,
                    rw_k_k, rw_k_a, rw_r_k, rw_ln_w, rw_ln_b, rw_w_o, moe_w_rg, moe_b_rg, moe_w_re, moe_b_re,
                    moe_w_gu, moe_w_down, w_ada_kv, b_ada_kv, g_kv, w_kv, at_w_q, at_w_o, rel_bias, g_final):
    d = D_MODEL
    zeros = jnp.zeros((3, d), F32)
    pad = LANES - N_GROUPS_E - N_EXPERTS
    depth = moe_w_rg.shape[0]
    return dict(
        g_norm=g_norm, g_kv=g_kv, g_final=g_final, rel_bias=rel_bias,
        rw_mu=rw_mu[0],
        rw_vecs=jnp.concatenate([rw_w0[0][None], rw_a0[0][None], rw_k_k[0][None], rw_k_a[0][None],
                                 rw_r_k[0].reshape(1, d), zeros], axis=0),
        rw_w_rkv=rw_w_rkv[0].astype(BF16), rw_w1=rw_w1[0], rw_w2=rw_w2[0], rw_a1=rw_a1[0], rw_a2=rw_a2[0],
        rw_g1=rw_g1[0], rw_g2=rw_g2[0],
        rw_ln=jnp.concatenate([rw_ln_w[0][None], rw_ln_b[0][None], zeros, zeros], axis=0),
        rw_w_o=rw_w_o[0].astype(BF16),
        moe_wr=jnp.pad(jnp.concatenate([moe_w_rg, moe_w_re], axis=-1), ((0, 0), (0, 0), (0, pad))),
        moe_br=jnp.pad(jnp.concatenate([moe_b_rg, moe_b_re], axis=-1), ((0, 0), (0, pad)))[:, None, :],
        moe_w_gu=moe_w_gu.reshape(depth, N_EXPERTS, d, 2 * D_EXPERT),
        moe_w_down=moe_w_down.reshape(depth, N_EXPERTS, D_EXPERT, d),
        w_kv=w_kv.astype(BF16), at_w_q=at_w_q[0].astype(BF16), at_w_o=at_w_o[0].astype(BF16),
    )


def kernel(x_prompt, x_sample, state_wkv, state_shift, cache_k, cache_v, c_prompt, c_sample, w_ada, b_ada, g_norm, rw_mu, rw_w_rkv, rw_w0, rw_w1, rw_w2, rw_a0, rw_a1, rw_a2, rw_g1, rw_g2, rw_k_k, rw_k_a, rw_r_k, rw_ln_w, rw_ln_b, rw_w_o, moe_w_rg, moe_b_rg, moe_w_re, moe_b_re, moe_w_gu, moe_w_down, w_ada_kv, b_ada_kv, g_kv, w_kv, at_w_q, at_w_o, rel_bias, g_final):
    d = D_MODEL
    bp, tp = x_prompt.shape[:2]
    nb = x_sample.shape[0]
    p = _prepare_params(w_ada, b_ada, g_norm, rw_mu, rw_w_rkv, rw_w0, rw_w1, rw_w2, rw_a0, rw_a1, rw_a2, rw_g1,
                        rw_g2, rw_k_k, rw_k_a, rw_r_k, rw_ln_w, rw_ln_b, rw_w_o, moe_w_rg, moe_b_rg, moe_w_re,
                        moe_b_re, moe_w_gu, moe_w_down, w_ada_kv, b_ada_kv, g_kv, w_kv, at_w_q, at_w_o, rel_bias,
                        g_final)

    n_c = bp + nb
    n_pad = -n_c % 8
    c_all = jnp.concatenate([c_prompt, c_sample, jnp.zeros((n_pad, d), F32)], axis=0)
    mod_l = [_ada_linear(c_all, w_ada[l], b_ada[l]) for l in range(2)]
    mod_kv = _ada_linear(c_all, w_ada_kv, b_ada_kv)

    def split(m, n, lo, hi, per_batch):
        parts = [m[lo:hi, d * j:d * (j + 1)] for j in range(n)]
        return [q[:, None, :] if per_batch else q[None] for q in parts]

    mods_p = dict(l0=split(mod_l[0], 6, 0, bp, True), l1=split(mod_l[1], 6, 0, bp, True),
                  kv=split(mod_kv, 2, 0, bp, True))
    mods_s = dict(l0=split(mod_l[0], 6, bp, n_c, False), l1=split(mod_l[1], 6, bp, n_c, False),
                  kv=split(mod_kv, 2, bp, n_c, False))

    y_p, wkv_p, shift_p, k_p, v_p = _trunk(x_prompt, mods_p, None, None, None, p, tm=256, tm_moe=512)
    keep = min(BAND * max(DILATIONS), tp)
    k_prompt = k_p[:, tp - keep:].reshape(bp, keep, N_HEADS, HEAD)
    v_prompt = v_p[:, tp - keep:].reshape(bp, keep, N_HEADS, HEAD)

    xs = x_sample.reshape(1, nb, d)
    y_s, wkv_s, shift_s, k_s, v_s = _trunk(xs, mods_s, state_wkv[0], state_shift[0][None], (cache_k, cache_v), p,
                                           tm=nb, tm_moe=nb)
    return (y_p, y_s.reshape(nb, 1, d), wkv_p[None], shift_p[None], k_prompt, v_prompt,
            wkv_s[None], shift_s[None], k_s.reshape(nb, 1, N_HEADS, HEAD), v_s.reshape(nb, 1, N_HEADS, HEAD))
```

```python
import functools

import numpy as np
import jax
import jax.numpy as jnp
from jax import lax
from jax.experimental import pallas as pl
from jax.experimental.pallas import tpu as pltpu

F32 = jnp.float32
BF16 = jnp.bfloat16

D_MODEL = 1024
N_HEADS = 16
HEAD = 64
LANES = 128
N_PAIRS = D_MODEL // LANES
N_MIX = 6
GN_EPS = 64e-5
RMS_EPS = 1e-6
NEG_INF = -1e30
DILATIONS = (1, 4, 16)
BAND = 128
QB = 128
SCALE_B = HEAD ** -0.5
NUM_BUCKETS = 32
MAX_DISTANCE = 2048
N_GROUPS_E = 4
N_EXP_PER_GROUP = 8
N_EXPERTS = N_GROUPS_E * N_EXP_PER_GROUP
D_EXPERT = 512
ROUTER_LANE0 = N_GROUPS_E
SCAN_CHUNK = 64

VMEM_LIMIT = 56 * 1024 * 1024


def _cparams(sem):
    return pltpu.CompilerParams(dimension_semantics=sem, vmem_limit_bytes=VMEM_LIMIT)


def _dot(a, b):
    return jnp.dot(a.astype(BF16), b.astype(BF16), preferred_element_type=F32)


def _dot_nt(a, b):
    return lax.dot_general(a.astype(BF16), b.astype(BF16), (((1,), (1,)), ((), ())),
                           preferred_element_type=F32)


def _split3(a):
    hi = a.astype(BF16)
    r1 = a - hi.astype(F32)
    mid = r1.astype(BF16)
    lo = (r1 - mid.astype(F32)).astype(BF16)
    return hi, mid, lo


def _dot3(a, b01):
    hi, mid, lo = _split3(a)
    b = b01.astype(BF16)
    return (jnp.dot(hi, b, preferred_element_type=F32) + jnp.dot(mid, b, preferred_element_type=F32)
            + jnp.dot(lo, b, preferred_element_type=F32))


def _dot3_left(a01, b):
    hi, mid, lo = _split3(b)
    a = a01.astype(BF16)
    return (jnp.dot(a, hi, preferred_element_type=F32) + jnp.dot(a, mid, preferred_element_type=F32)
            + jnp.dot(a, lo, preferred_element_type=F32))


def _sigmoid(x):
    return 1.0 / (1.0 + jnp.exp(-x))


def _silu(x):
    return x * _sigmoid(x)


def _softplus(z):
    return jnp.maximum(z, 0.0) + jnp.log(1.0 + jnp.exp(-jnp.abs(z)))


def _norm_mod(x, g, sh, sc):
    ms = jnp.mean(x * x, axis=-1, keepdims=True)
    return (x * lax.rsqrt(ms + RMS_EPS) * g) * (1.0 + sc) + sh


def _head_sum_matrix():
    r = lax.broadcasted_iota(jnp.int32, (LANES, LANES), 0) >= HEAD
    c = lax.broadcasted_iota(jnp.int32, (LANES, LANES), 1) >= HEAD
    return jnp.where(r == c, 1.0, 0.0).astype(BF16)


def _head_sum(z, m128):
    parts = [_dot3(z[:, LANES * p:LANES * (p + 1)], m128) for p in range(N_PAIRS)]
    return jnp.concatenate(parts, axis=1)


def _row_spec(arr, tm):
    n = arr.shape[-1]
    if arr.shape[1] == 1:
        return pl.BlockSpec((1, 1, n), lambda b, i: (b, 0, 0))
    return pl.BlockSpec((1, tm, n), lambda b, i: (b, i, 0))


def _const_spec(arr):
    nd = arr.ndim
    return pl.BlockSpec(arr.shape, lambda b, i, _nd=nd: (0,) * _nd)


def _ada_kernel(c_ref, w_ref, b_ref, o_ref):
    o_ref[...] = _dot(_silu(c_ref[...]), w_ref[0]) + b_ref[0]


def _ada_linear(c_all, w, b, layer):
    m, k = c_all.shape
    n = w.shape[2]
    tn = 1024
    return pl.pallas_call(
        _ada_kernel,
        out_shape=jax.ShapeDtypeStruct((m, n), F32),
        grid=(n // tn,),
        in_specs=[pl.BlockSpec((m, k), lambda j: (0, 0)),
                  pl.BlockSpec((1, k, tn), lambda j: (layer, 0, j)),
                  pl.BlockSpec((1, 1, tn), lambda j: (layer, 0, j))],
        out_specs=pl.BlockSpec((m, tn), lambda j: (0, j)),
        compiler_params=_cparams(("parallel",)),
        name="ada_linear",
    )(c_all, w, b.reshape(b.shape[0], 1, n))


def _norm_mod_kernel(x_ref, g_ref, sh_ref, sc_ref, o_ref):
    o_ref[0] = _norm_mod(x_ref[0], g_ref[...], sh_ref[0], sc_ref[0])


def _norm_mod_call(x, g, sh, sc, tm):
    b, t, d = x.shape
    g2 = g.reshape(1, d)
    return pl.pallas_call(
        _norm_mod_kernel,
        out_shape=jax.ShapeDtypeStruct(x.shape, F32),
        grid=(b, t // tm),
        in_specs=[_row_spec(x, tm), _const_spec(g2), _row_spec(sh, tm), _row_spec(sc, tm)],
        out_specs=pl.BlockSpec((1, tm, d), lambda bb, i: (bb, i, 0)),
        compiler_params=_cparams(("parallel", "parallel")),
        name="norm_mod",
    )(x, g2, sh, sc)


def _linear_kernel(*refs, has_norm, has_resid):
    it = iter(refs)
    x_ref = next(it)
    w_ref = next(it)
    if has_norm:
        g_ref, sh_ref, sc_ref = next(it), next(it), next(it)
    if has_resid:
        xr_ref, gt_ref = next(it), next(it)
    o_ref = next(it)
    x = x_ref[0]
    if has_norm:
        x = _norm_mod(x, g_ref[...], sh_ref[0], sc_ref[0])
    out = _dot(x, w_ref[...])
    if has_resid:
        out = xr_ref[0] + gt_ref[0] * out
    o_ref[0] = out


def _linear_call(x, w, tm, norm=None, resid=None, name="linear"):
    b, t, k = x.shape
    n = w.shape[1]
    args = [x, w]
    specs = [_row_spec(x, tm), _const_spec(w)]
    if norm is not None:
        g, sh, sc = norm
        g2 = g.reshape(1, k)
        args += [g2, sh, sc]
        specs += [_const_spec(g2), _row_spec(sh, tm), _row_spec(sc, tm)]
    if resid is not None:
        xr, gt = resid
        args += [xr, gt]
        specs += [_row_spec(xr, tm), _row_spec(gt, tm)]
    return pl.pallas_call(
        functools.partial(_linear_kernel, has_norm=norm is not None, has_resid=resid is not None),
        out_shape=jax.ShapeDtypeStruct((b, t, n), F32),
        grid=(b, t // tm),
        in_specs=specs,
        out_specs=pl.BlockSpec((1, tm, n), lambda bb, i: (bb, i, 0)),
        compiler_params=_cparams(("parallel", "parallel")),
        name=name,
    )(*args)


def _fill_tiles(scr, val):
    for c in range(val.shape[1] // LANES):
        scr[c] = val[:, LANES * c:LANES * (c + 1)]


def _store_dilated(scr, out_ref, dil):
    n_tiles, tm, _ = scr.shape
    n = n_tiles * LANES
    for r in range(dil):
        rows = [scr[c, pl.ds(r, tm // dil, stride=dil), :] for c in range(n_tiles)]
        out_ref[0, :, n * r:n * (r + 1)] = jnp.concatenate(rows, axis=1).astype(out_ref.dtype)


def _load_dilated(in_ref, scr, dil):
    n_tiles, tm, _ = scr.shape
    n = n_tiles * LANES
    for r in range(dil):
        blk = in_ref[0, :, n * r:n * (r + 1)].astype(F32)
        for c in range(n_tiles):
            scr[c, pl.ds(r, tm // dil, stride=dil), :] = blk[:, LANES * c:LANES * (c + 1)]
    return jnp.concatenate([scr[c] for c in range(n_tiles)], axis=1)


def _qproj_kernel(x_ref, g_ref, sh_ref, sc_ref, w_ref, q0_ref, q1_ref, q2_ref, s1, s2):
    d = D_MODEL
    q = _dot(_norm_mod(x_ref[0], g_ref[...], sh_ref[0], sc_ref[0]), w_ref[...]) * SCALE_B
    q0_ref[0] = q[:, :d].astype(q0_ref.dtype)
    _fill_tiles(s1, q[:, d:2 * d])
    _store_dilated(s1, q1_ref, DILATIONS[1])
    _fill_tiles(s2, q[:, 2 * d:])
    _store_dilated(s2, q2_ref, DILATIONS[2])


def _dilated_shape(b, t, n, dil, dtype):
    return jax.ShapeDtypeStruct((b, t // dil, dil * n), dtype)


def _dilated_spec(tm, n, dil):
    return pl.BlockSpec((1, tm // dil, dil * n), lambda bb, i: (bb, i, 0))


def _qproj_call(x, w, g, sh, sc, tm):
    b, t, d = x.shape
    g2 = g.reshape(1, d)
    tile = pltpu.VMEM((N_PAIRS, tm, LANES), F32)
    return pl.pallas_call(
        _qproj_kernel,
        out_shape=[_dilated_shape(b, t, d, dil, BF16) for dil in DILATIONS],
        grid=(b, t // tm),
        in_specs=[_row_spec(x, tm), _const_spec(g2), _row_spec(sh, tm), _row_spec(sc, tm), _const_spec(w)],
        out_specs=[_dilated_spec(tm, d, dil) for dil in DILATIONS],
        scratch_shapes=[tile, tile],
        compiler_params=_cparams(("parallel", "parallel")),
        name="q_proj",
    )(x, g2, sh, sc, w)


def _kvproj_kernel(x_ref, g_ref, sh_ref, sc_ref, w_ref, k_ref, v_ref, k1_ref, v1_ref, k2_ref, v2_ref, sk, sv):
    d = D_MODEL
    kv = _dot(_norm_mod(x_ref[0], g_ref[...], sh_ref[0], sc_ref[0]), w_ref[...])
    k_ref[0] = kv[:, :d]
    v_ref[0] = kv[:, d:]
    _fill_tiles(sk, kv[:, :d])
    _fill_tiles(sv, kv[:, d:])
    _store_dilated(sk, k1_ref, DILATIONS[1])
    _store_dilated(sv, v1_ref, DILATIONS[1])
    _store_dilated(sk, k2_ref, DILATIONS[2])
    _store_dilated(sv, v2_ref, DILATIONS[2])


def _kvproj_call(x, w, g, sh, sc, tm):
    b, t, d = x.shape
    g2 = g.reshape(1, d)
    tile = pltpu.VMEM((N_PAIRS, tm, LANES), F32)
    plain = jax.ShapeDtypeStruct((b, t, d), F32)
    return pl.pallas_call(
        _kvproj_kernel,
        out_shape=[plain, plain] + [_dilated_shape(b, t, d, dil, BF16) for dil in DILATIONS[1:] for _ in range(2)],
        grid=(b, t // tm),
        in_specs=[_row_spec(x, tm), _const_spec(g2), _row_spec(sh, tm), _row_spec(sc, tm), _const_spec(w)],
        out_specs=[_dilated_spec(tm, d, 1)] * 2 + [_dilated_spec(tm, d, dil) for dil in DILATIONS[1:] for _ in range(2)],
        scratch_shapes=[tile, tile],
        compiler_params=_cparams(("parallel", "parallel")),
        name="kv_proj",
    )(x, g2, sh, sc, w)


PREV_ROWS = 8


def _rwkv_proj_kernel(x_ref, aux_ref, gn_ref, sh_ref, sc_ref, mu_ref, vec_ref, wrkv_ref, w1_ref, w2_ref, a1_ref, a2_ref,
                      g1_ref, g2_ref, r_ref, lw_ref, k_ref, v_ref, a_ref, b_ref, g_ref, bonus_ref, *, sequence):
    h = _norm_mod(x_ref[0], gn_ref[...], sh_ref[0], sc_ref[0])
    if sequence:
        prev = _norm_mod(aux_ref[0], gn_ref[...], sh_ref[0], sc_ref[0])[PREV_ROWS - 1:PREV_ROWS, :]
        prev = jnp.where(pl.program_id(1) > 0, prev, 0.0)
        row = lax.broadcasted_iota(jnp.int32, h.shape, 0)
        hp = jnp.where(row == 0, prev, pltpu.roll(h, 1, axis=0))
    else:
        hp = aux_ref[0]
    xx = hp - h

    def mix(i):
        return h + xx * mu_ref[i:i + 1, :]

    w0, a0, k_k, k_a, r_k = (vec_ref[i:i + 1, :] for i in range(5))
    m128 = _head_sum_matrix()
    r = _dot(mix(0), wrkv_ref[0])
    k = _dot(mix(1), wrkv_ref[1])
    v = _dot(mix(2), wrkv_ref[2])
    wl = w0 + _dot(jnp.tanh(_dot(mix(3), w1_ref[...])), w2_ref[...])
    w_log = -_softplus(-wl) - 0.5
    a_sig = _sigmoid(a0 + _dot(_dot(mix(4), a1_ref[...]), a2_ref[...]))
    g = _dot(_sigmoid(_dot(mix(5), g1_ref[...])), g2_ref[...])
    kk = k * k_k
    kk = kk / jnp.maximum(jnp.sqrt(_head_sum(kk * kk, m128)), 1e-12)
    k2 = k * (1.0 + (a_sig - 1.0) * k_a)
    r_ref[0] = r
    lw_ref[0] = -jnp.exp(w_log)
    k_ref[0] = k2
    v_ref[0] = v
    a_ref[0] = -kk
    b_ref[0] = kk * a_sig
    g_ref[0] = g
    bonus_ref[0] = _head_sum(r * k2 * r_k, m128) * v


def _rwkv_proj_call(x, shift, gn, sh, sc, mu, vecs, wrkv, w1, w2, a1, a2, g1, g2, tm):
    b, t, d = x.shape
    gn2 = gn.reshape(1, d)
    consts = [mu, vecs, wrkv, w1, w2, a1, a2, g1, g2]
    out = jax.ShapeDtypeStruct((b, t, d), F32)
    ospec = pl.BlockSpec((1, tm, d), lambda bb, i: (bb, i, 0))
    if shift is None:
        per = tm // PREV_ROWS
        aux, aux_spec = x, pl.BlockSpec((1, PREV_ROWS, d), lambda bb, i: (bb, jnp.maximum(i * per - 1, 0), 0))
    else:
        aux, aux_spec = shift, _row_spec(shift, tm)
    return pl.pallas_call(
        functools.partial(_rwkv_proj_kernel, sequence=shift is None),
        out_shape=[out] * 8,
        grid=(b, t // tm),
        in_specs=[_row_spec(x, tm), aux_spec, _const_spec(gn2), _row_spec(sh, tm), _row_spec(sc, tm)]
                 + [_const_spec(c) for c in consts],
        out_specs=[ospec] * 8,
        compiler_params=_cparams(("parallel", "parallel")),
        name="rwkv_proj",
    )(x, aux, gn2, sh, sc, *consts)


def _rwkv_scan_kernel(r_ref, lw_ref, k_ref, v_ref, a_ref, b_ref, y_ref, hout_ref, h_scr):
    c = pl.program_id(0)
    nc = pl.num_programs(0)
    nb = r_ref.shape[0]
    C = SCAN_CHUNK
    n2 = 2 * C

    @pl.when(c == 0)
    def _():
        h_scr[...] = jnp.zeros_like(h_scr)

    ri = lax.broadcasted_iota(jnp.int32, (C, C), 0)
    ci = lax.broadcasted_iota(jnp.int32, (C, C), 1)
    tril = jnp.where(ri >= ci, 1.0, 0.0).astype(BF16)
    head0 = lax.broadcasted_iota(jnp.int32, (C, LANES), 1) < HEAD
    tt = lax.broadcasted_iota(jnp.int32, (n2, n2), 0) & (C - 1)
    ss = lax.broadcasted_iota(jnp.int32, (n2, n2), 1) & (C - 1)
    strict = tt > ss
    incl = tt >= ss
    n_double = C.bit_length() - 1

    def two(xs):
        return jnp.concatenate([jnp.where(head0, xs, 0.0), jnp.where(head0, 0.0, xs)], axis=0)

    ar, bk, v2, w2t, gcol = [], [], [], [], []
    for bi in range(nb):
        lw = lw_ref[bi]
        cum = _dot3_left(tril, lw)
        gam = jnp.exp(cum)
        ginv = jnp.exp(-cum)
        g_last = gam[C - 1:C, :]
        rt = r_ref[bi] * gam
        at = a_ref[bi] * jnp.exp(cum - lw)
        bt = b_ref[bi] * ginv
        kt = k_ref[bi] * ginv
        bh = bt * g_last
        kh = kt * g_last
        v = v_ref[bi]
        for p in range(N_PAIRS):
            sl = slice(LANES * p, LANES * (p + 1))
            ar.append(jnp.concatenate([two(at[:, sl]), two(rt[:, sl])], axis=0).astype(BF16))
            bk.append(jnp.concatenate([two(bt[:, sl]), two(kt[:, sl])], axis=0).astype(BF16))
            v2.append(two(v[:, sl]).astype(BF16))
            w2t.append(jnp.transpose(jnp.concatenate([two(bh[:, sl]), two(kh[:, sl])], axis=0)).astype(BF16))
            gcol.append(jnp.transpose(jnp.broadcast_to(g_last[:, sl], (LANES, LANES))))
    n = len(ar)
    hp = [h_scr[i] for i in range(n)]
    s = [_dot_nt(ar[i], bk[i]) for i in range(n)]
    arh = [_dot(ar[i], hp[i]) for i in range(n)]
    pw = [jnp.where(strict, s[i][:n2, :n2], 0.0).astype(BF16) for i in range(n)]
    lt = [jnp.concatenate([jnp.where(strict, s[i][:n2, n2:], 0.0), jnp.where(incl, s[i][n2:, n2:], 0.0)],
                          axis=0) for i in range(n)]
    t_rb = [jnp.where(incl, s[i][n2:, :n2], 0.0) for i in range(n)]
    lv = [_dot(lt[i], v2[i]) for i in range(n)]
    u = [arh[i][:n2] + lv[i][:n2] for i in range(n)]
    for it in range(n_double):
        pu = [_dot(pw[i], u[i]) for i in range(n)]
        if it + 1 < n_double:
            pw = [_dot(pw[i], pw[i]).astype(BF16) for i in range(n)]
        u = [u[i] + pu[i] for i in range(n)]
    tu = [_dot(t_rb[i], u[i]) for i in range(n)]
    hn = [_dot(w2t[i], jnp.concatenate([u[i].astype(BF16), v2[i]], axis=0)) for i in range(n)]
    for i in range(n):
        h_scr[i] = gcol[i] * hp[i] + hn[i]
    for bi in range(nb):
        ys = []
        for p in range(N_PAIRS):
            i = bi * N_PAIRS + p
            y2 = arh[i][n2:] + lv[i][n2:] + tu[i]
            ys.append(y2[:C] + y2[C:])
        y_ref[bi] = jnp.concatenate(ys, axis=1)

    @pl.when(c == nc - 1)
    def _():
        hout_ref[...] = h_scr[...]


def _rwkv_scan_call(r, lw, k, v, a, bb):
    b, t, d = r.shape
    C = SCAN_CHUNK
    spec = pl.BlockSpec((b, C, d), lambda c: (0, c, 0))
    y, hout = pl.pallas_call(
        _rwkv_scan_kernel,
        out_shape=[jax.ShapeDtypeStruct((b, t, d), F32),
                   jax.ShapeDtypeStruct((b * N_PAIRS, LANES, LANES), F32)],
        grid=(t // C,),
        in_specs=[spec] * 6,
        out_specs=[spec, pl.BlockSpec((b * N_PAIRS, LANES, LANES), lambda c: (0, 0, 0))],
        scratch_shapes=[pltpu.VMEM((b * N_PAIRS, LANES, LANES), F32)],
        compiler_params=_cparams(("arbitrary",)),
        name="rwkv_scan",
    )(r, lw, k, v, a, bb)
    hh = hout.reshape(b, N_PAIRS, 2, HEAD, 2, HEAD)
    hh = jnp.stack([hh[:, :, 0, :, 0, :], hh[:, :, 1, :, 1, :]], axis=2).reshape(b, N_HEADS, HEAD, HEAD)
    return y, jnp.swapaxes(hh, -1, -2)


def _rwkv_step_kernel(s_ref, z_ref, so_ref, y_ref):
    r, lw, k, v, a, b = (z_ref[c, 0] for c in range(6))
    w = jnp.exp(lw)
    ys = []
    for i in range(HEAD):
        s = s_ref[0, i]
        sa = jnp.sum(s * a, axis=0, keepdims=True)
        sn = s * w + sa * b + v[i:i + 1, :] * k
        so_ref[0, i] = sn
        ys.append(jnp.sum(sn * r, axis=0, keepdims=True))
    y_ref[0] = jnp.concatenate(ys, axis=0)


def _rwkv_step_call(state, r, lw, k, v, a, bb):
    nb = state.shape[0]
    st = jnp.transpose(state, (1, 2, 3, 0))
    z = jnp.stack([jnp.transpose(q[0]).reshape(N_HEADS, HEAD, nb) for q in (r, lw, k, v, a, bb)])
    so, y = pl.pallas_call(
        _rwkv_step_kernel,
        out_shape=[jax.ShapeDtypeStruct((N_HEADS, HEAD, HEAD, nb), F32),
                   jax.ShapeDtypeStruct((N_HEADS, HEAD, nb), F32)],
        grid=(N_HEADS,),
        in_specs=[pl.BlockSpec((1, HEAD, HEAD, nb), lambda h: (h, 0, 0, 0)),
                  pl.BlockSpec((6, 1, HEAD, nb), lambda h: (0, h, 0, 0))],
        out_specs=[pl.BlockSpec((1, HEAD, HEAD, nb), lambda h: (h, 0, 0, 0)),
                   pl.BlockSpec((1, HEAD, nb), lambda h: (h, 0, 0))],
        compiler_params=_cparams(("parallel",)),
        name="rwkv_step",
    )(st, z)
    return jnp.transpose(y.reshape(D_MODEL, nb))[None], jnp.transpose(so, (3, 0, 1, 2))


def _rwkv_out_kernel(y_ref, bonus_ref, g_ref, x_ref, gt_ref, ln_ref, wo_ref, o_ref):
    y = y_ref[0]
    m128 = _head_sum_matrix()
    mean = _head_sum(y, m128) * (1.0 / HEAD)
    dlt = y - mean
    var = _head_sum(dlt * dlt, m128) * (1.0 / HEAD)
    yn = dlt * lax.rsqrt(var + GN_EPS) * ln_ref[0:1, :] + ln_ref[1:2, :]
    out = _dot((yn + bonus_ref[0]) * g_ref[0], wo_ref[...])
    o_ref[0] = x_ref[0] + gt_ref[0] * out


def _rwkv_out_call(y, bonus, g, x, gt, ln, wo, tm):
    b, t, d = y.shape
    return pl.pallas_call(
        _rwkv_out_kernel,
        out_shape=jax.ShapeDtypeStruct((b, t, d), F32),
        grid=(b, t // tm),
        in_specs=[_row_spec(y, tm), _row_spec(bonus, tm), _row_spec(g, tm), _row_spec(x, tm), _row_spec(gt, tm),
                  _const_spec(ln), _const_spec(wo)],
        out_specs=pl.BlockSpec((1, tm, d), lambda bb, i: (bb, i, 0)),
        compiler_params=_cparams(("parallel", "parallel")),
        name="rwkv_out",
    )(y, bonus, g, x, gt, ln, wo)


ROUTE_E, ROUTE_W, ROUTE_RANK = 0, 2, 4


def _router_kernel(x_ref, g_ref, sh_ref, sc_ref, wr_ref, br_ref, hf_ref, route_ref, route_t_ref, cnt_ref, carry):
    first = jnp.logical_and(pl.program_id(0) == 0, pl.program_id(1) == 0)

    @pl.when(first)
    def _():
        carry[...] = jnp.zeros_like(carry)

    hf = _norm_mod(x_ref[0], g_ref[...], sh_ref[0], sc_ref[0])
    hf_ref[0] = hf
    logits = jnp.dot(hf, wr_ref[...], preferred_element_type=F32, precision=lax.Precision.HIGHEST) + br_ref[...]
    lane = lax.broadcasted_iota(jnp.int32, logits.shape, 1).astype(F32)
    big = float(1 << 20)
    gl = jnp.where(lane < N_GROUPS_E, logits, NEG_INF)
    gmax = jnp.max(gl, axis=-1, keepdims=True)
    p_top = 1.0 / jnp.sum(jnp.exp(gl - gmax), axis=-1, keepdims=True)
    g_sel = jnp.min(jnp.where(gl == gmax, lane, big), axis=-1, keepdims=True)
    lo = ROUTER_LANE0 + N_EXP_PER_GROUP * g_sel
    el = jnp.where((lane >= lo) & (lane < lo + N_EXP_PER_GROUP), logits, NEG_INF)
    v1 = jnp.max(el, axis=-1, keepdims=True)
    i1 = jnp.min(jnp.where(el == v1, lane, big), axis=-1, keepdims=True)
    el2 = jnp.where(lane == i1, NEG_INF, el)
    v2 = jnp.max(el2, axis=-1, keepdims=True)
    i2 = jnp.min(jnp.where(el2 == v2, lane, big), axis=-1, keepdims=True)
    e21 = jnp.exp(v2 - v1)
    w1 = p_top / (1.0 + e21)
    w2 = p_top * e21 / (1.0 + e21)
    tm = logits.shape[0]
    onehot = jnp.where(lane == i1, 1.0, 0.0) + jnp.where(lane == i2, 1.0, 0.0)
    rr = lax.broadcasted_iota(jnp.int32, (tm, tm), 0)
    cc = lax.broadcasted_iota(jnp.int32, (tm, tm), 1)
    earlier = jnp.where(rr > cc, 1.0, 0.0).astype(BF16)
    before = jnp.dot(earlier, onehot.astype(BF16), preferred_element_type=F32) + carry[0:1, :]
    rank1 = jnp.sum(jnp.where(lane == i1, before, 0.0), axis=-1, keepdims=True)
    rank2 = jnp.sum(jnp.where(lane == i2, before, 0.0), axis=-1, keepdims=True)
    carry[...] = carry[...] + jnp.sum(onehot, axis=0, keepdims=True)
    cnt_ref[...] = carry[...]
    route = jnp.zeros_like(logits)
    for ln, val in ((ROUTE_E, i1 - ROUTER_LANE0), (ROUTE_E + 1, i2 - ROUTER_LANE0), (ROUTE_W, w1), (ROUTE_W + 1, w2),
                    (ROUTE_RANK, rank1), (ROUTE_RANK + 1, rank2)):
        route = jnp.where(lane == ln, val, route)
    route_ref[0] = route
    route_t_ref[...] = jnp.transpose(route)[:8, :]


def _router_call(x, g, sh, sc, wr, br, tm):
    b, t, d = x.shape
    g2 = g.reshape(1, d)
    nt = t // tm
    return pl.pallas_call(
        _router_kernel,
        out_shape=[jax.ShapeDtypeStruct((b, t, d), F32), jax.ShapeDtypeStruct((b, t, LANES), F32),
                   jax.ShapeDtypeStruct((8, b * t), F32), jax.ShapeDtypeStruct((8, LANES), F32)],
        grid=(b, nt),
        in_specs=[_row_spec(x, tm), _const_spec(g2), _row_spec(sh, tm), _row_spec(sc, tm),
                  _const_spec(wr), _const_spec(br)],
        out_specs=[pl.BlockSpec((1, tm, d), lambda bb, i: (bb, i, 0)),
                   pl.BlockSpec((1, tm, LANES), lambda bb, i: (bb, i, 0)),
                   pl.BlockSpec((8, tm), lambda bb, i: (0, bb * nt + i)),
                   pl.BlockSpec((8, LANES), lambda bb, i: (0, 0))],
        scratch_shapes=[pltpu.VMEM((8, LANES), F32)],
        compiler_params=_cparams(("arbitrary", "arbitrary")),
        name="moe_router",
    )(x, g2, sh, sc, wr, br)


def _route_plan(route_t, counts, tile, layer):
    n = route_t.shape[1]
    e = route_t[ROUTE_E:ROUTE_E + 2].astype(jnp.int32)
    rank = route_t[ROUTE_RANK:ROUTE_RANK + 2].astype(jnp.int32)
    cnt = counts[0, ROUTER_LANE0:ROUTER_LANE0 + N_EXPERTS].astype(jnp.int32)
    nt = (cnt + tile - 1) // tile
    tend = jnp.cumsum(nt)
    tstart = tend - nt
    start = jnp.zeros_like(e)
    for ex in range(N_EXPERTS):
        start = jnp.where(e == ex, tstart[ex], start)
    dest = start * tile + rank
    n_tiles = (2 * n) // tile + N_EXPERTS
    tid = jnp.arange(n_tiles, dtype=jnp.int32)
    te = jnp.minimum(jnp.sum((tid[:, None] >= tend[None, :]).astype(jnp.int32), axis=1), N_EXPERTS - 1)
    active = (tid < tend[-1]).astype(jnp.int32)
    return dest.reshape(-1), te + layer * N_EXPERTS, active, tend, n_tiles


def _dispatch_kernel(dest_ref, tend_ref, hf_ref, xs_ref, zeros, sem, zsem, *, tm, tile):
    n = pl.num_programs(0) * tm
    base = pl.program_id(0) * tm

    @pl.when(pl.program_id(0) == 0)
    def _():
        zeros[...] = jnp.zeros_like(zeros)
        n_tiles = xs_ref.shape[0] // tile
        used = tend_ref[N_EXPERTS - 1]

        def target(j):
            if j < N_EXPERTS:
                return tend_ref[j] > (tend_ref[j - 1] if j > 0 else 0), tend_ref[j] - 1
            return used + (j - N_EXPERTS) < n_tiles, used + (j - N_EXPERTS)

        for j in range(2 * N_EXPERTS):
            ok, tile_idx = target(j)

            @pl.when(ok)
            def _():
                pltpu.make_async_copy(zeros, xs_ref.at[pl.ds(tile_idx * tile, tile), :], zsem).start()
        for j in range(2 * N_EXPERTS):
            ok, _unused = target(j)

            @pl.when(ok)
            def _():
                pltpu.make_async_copy(zeros, xs_ref.at[pl.ds(0, tile), :], zsem).wait()

    def body(t, carry):
        row = hf_ref.at[pl.ds(t, 1), :]
        for c in range(2):
            dst = dest_ref[c * n + base + t]
            pltpu.make_async_copy(row, xs_ref.at[pl.ds(dst, 1), :], sem).start()
        return carry

    lax.fori_loop(0, tm, body, 0, unroll=8)
    done = xs_ref.at[pl.ds(0, 2 * tm), :]
    pltpu.make_async_copy(done, done, sem).wait()


def _dispatch_call(hf2, dest, tend, n_rows, tm, tile):
    n, d = hf2.shape
    return pl.pallas_call(
        functools.partial(_dispatch_kernel, tm=tm, tile=tile),
        out_shape=jax.ShapeDtypeStruct((n_rows, d), F32),
        grid_spec=pltpu.PrefetchScalarGridSpec(
            num_scalar_prefetch=2,
            grid=(n // tm,),
            in_specs=[pl.BlockSpec((tm, d), lambda i, dest, tend: (i, 0))],
            out_specs=pl.BlockSpec(memory_space=pl.ANY),
            scratch_shapes=[pltpu.VMEM((tile, d), F32), pltpu.SemaphoreType.DMA(()), pltpu.SemaphoreType.DMA(())],
        ),
        compiler_params=_cparams(("arbitrary",)),
        name="moe_dispatch",
    )(dest, tend, hf2)


def _expert_kernel(te_ref, act_ref, xs_ref, wgu_ref, wd_ref, o_ref, wgu_lp, wd_lp):
    i = pl.program_id(0)
    changed = jnp.logical_or(i == 0, te_ref[i] != te_ref[jnp.maximum(i - 1, 0)])

    @pl.when(changed)
    def _():
        wgu_lp[...] = wgu_ref[0].astype(wgu_lp.dtype)
        wd_lp[...] = wd_ref[0].astype(wd_lp.dtype)

    @pl.when(act_ref[i] == 1)
    def _():
        gu = _dot(xs_ref[...], wgu_lp[...])
        act = _silu(gu[:, :D_EXPERT]) * gu[:, D_EXPERT:]
        o_ref[...] = _dot(act, wd_lp[...])

    @pl.when(act_ref[i] == 0)
    def _():
        o_ref[...] = jnp.zeros_like(o_ref)


def _expert_call(xs, te, active, wgu, wd, tile):
    n_rows, d = xs.shape
    return pl.pallas_call(
        _expert_kernel,
        out_shape=jax.ShapeDtypeStruct((n_rows, d), F32),
        grid_spec=pltpu.PrefetchScalarGridSpec(
            num_scalar_prefetch=2,
            grid=(n_rows // tile,),
            in_specs=[pl.BlockSpec((tile, d), lambda i, te, act: (i, 0)),
                      pl.BlockSpec((1, d, 2 * D_EXPERT), lambda i, te, act: (te[i], 0, 0)),
                      pl.BlockSpec((1, D_EXPERT, d), lambda i, te, act: (te[i], 0, 0))],
            out_specs=pl.BlockSpec((tile, d), lambda i, te, act: (i, 0)),
            scratch_shapes=[pltpu.VMEM((d, 2 * D_EXPERT), BF16), pltpu.VMEM((D_EXPERT, d), BF16)],
        ),
        compiler_params=_cparams(("arbitrary",)),
        name="moe_experts",
    )(te, active, xs, wgu, wd)


def _combine_kernel(dest_ref, ys_ref, route_ref, x_ref, gt_ref, *rest, tm, final_norm):
    if final_norm:
        gf_ref, o_ref, buf, sem = rest
    else:
        o_ref, buf, sem = rest
    n_steps = pl.num_programs(0) * pl.num_programs(1)
    n = n_steps * tm
    step = pl.program_id(0) * pl.num_programs(1) + pl.program_id(1)

    def gather(s, slot):
        base = s * tm

        def body(t, carry):
            for c in range(2):
                src = dest_ref[c * n + base + t]
                pltpu.make_async_copy(ys_ref.at[pl.ds(src, 1), :], buf.at[slot, c, pl.ds(t, 1), :],
                                      sem.at[slot]).start()
            return carry

        lax.fori_loop(0, tm, body, 0, unroll=8)

    @pl.when(step == 0)
    def _():
        gather(0, 0)

    @pl.when(step + 1 < n_steps)
    def _():
        gather(step + 1, (step + 1) % 2)

    slot = step % 2
    pltpu.make_async_copy(buf.at[slot], buf.at[slot], sem.at[slot]).wait()
    route = route_ref[0]
    lane = lax.broadcasted_iota(jnp.int32, route.shape, 1)
    w1 = jnp.sum(jnp.where(lane == ROUTE_W, route, 0.0), axis=-1, keepdims=True)
    w2 = jnp.sum(jnp.where(lane == ROUTE_W + 1, route, 0.0), axis=-1, keepdims=True)
    out = x_ref[0] + gt_ref[0] * (w1 * buf[slot, 0] + w2 * buf[slot, 1])
    if final_norm:
        ms = jnp.mean(out * out, axis=-1, keepdims=True)
        out = out * lax.rsqrt(ms + RMS_EPS) * gf_ref[...]
    o_ref[0] = out


def _combine_call(ys, dest, route, x, gt, tm, final_g=None):
    b, t, d = x.shape
    extra = [] if final_g is None else [final_g.reshape(1, d)]

    def rs(arr):
        n = arr.shape[-1]
        if arr.shape[1] == 1:
            return pl.BlockSpec((1, 1, n), lambda bb, i, dest: (bb, 0, 0))
        return pl.BlockSpec((1, tm, n), lambda bb, i, dest: (bb, i, 0))

    return pl.pallas_call(
        functools.partial(_combine_kernel, tm=tm, final_norm=final_g is not None),
        out_shape=jax.ShapeDtypeStruct((b, t, d), F32),
        grid_spec=pltpu.PrefetchScalarGridSpec(
            num_scalar_prefetch=1,
            grid=(b, t // tm),
            in_specs=[pl.BlockSpec(memory_space=pl.ANY), rs(route), rs(x), rs(gt)]
                     + [pl.BlockSpec((1, d), lambda bb, i, dest: (0, 0)) for _ in extra],
            out_specs=pl.BlockSpec((1, tm, d), lambda bb, i, dest: (bb, i, 0)),
            scratch_shapes=[pltpu.VMEM((2, 2, tm, d), F32), pltpu.SemaphoreType.DMA((2,))],
        ),
        compiler_params=_cparams(("arbitrary", "arbitrary")),
        name="moe_combine",
    )(dest, ys, route, x, gt, *extra)


def _moe_call(x, g, sh, sc, gt, wr, br, wgu, wd, layer, tm, tile, final_g=None):
    b, t, d = x.shape
    hf, route, route_t, counts = _router_call(x, g, sh, sc, wr, br, tm)
    dest, te, active, tend, n_tiles = _route_plan(route_t, counts, tile, layer)
    xs = _dispatch_call(hf.reshape(b * t, d), dest, tend, n_tiles * tile, tm, tile)
    ys = _expert_call(xs, te, active, wgu, wd, tile)
    return _combine_call(ys, dest, route, x, gt, tm, final_g)


def _t5_bucket(dist):
    dist = np.asarray(dist)
    max_exact = NUM_BUCKETS // 2
    log_ratio = np.log(np.maximum(dist, 1) / max_exact) / np.log(MAX_DISTANCE / max_exact)
    large = np.minimum(max_exact + (log_ratio * (NUM_BUCKETS - max_exact)).astype(np.int32), NUM_BUCKETS - 1)
    return np.where(dist < max_exact, dist, large).astype(np.int32)


def _attn_prompt_kernel(q_ref, kp_ref, kc_ref, vp_ref, vc_ref, bias_ref, o_ref, lse_ref):
    i = pl.program_id(2)
    q = q_ref[0].astype(BF16)
    kcat = jnp.concatenate([kp_ref[0], kc_ref[0]], axis=0).astype(BF16)
    vcat = jnp.concatenate([vp_ref[0], vc_ref[0]], axis=0).astype(BF16)
    head0 = lax.broadcasted_iota(jnp.int32, (QB, LANES), 1) < HEAD
    lane = lax.broadcasted_iota(jnp.int32, (QB, LANES), 1)
    zero = jnp.zeros((), BF16)
    first = jnp.where(i == 0, 1, 0)
    heads = range(N_HEADS)

    def masked_q(h):
        qp = q[:, LANES * (h // 2):LANES * (h // 2 + 1)]
        return jnp.where(head0, qp, zero) if h % 2 == 0 else jnp.where(head0, zero, qp)

    logits = [_dot_nt(masked_q(h), kcat[:, LANES * (h // 2):LANES * (h // 2 + 1)]) + bias_ref[first, h] for h in heads]
    m = [jnp.max(logits[h], axis=-1, keepdims=True) for h in heads]
    pr = [jnp.exp(logits[h] - m[h]) for h in heads]
    ssum = [jnp.sum(pr[h], axis=-1, keepdims=True) for h in heads]
    o = [_dot(pr[h], vcat[:, LANES * (h // 2):LANES * (h // 2 + 1)]) * (1.0 / ssum[h]) for h in heads]
    lse_tile = jnp.zeros((QB, LANES), F32)
    for h in heads:
        lse_tile = jnp.where(lane == h, m[h] + jnp.log(ssum[h]), lse_tile)
    o_ref[0] = jnp.concatenate([jnp.where(head0, o[2 * p], o[2 * p + 1]) for p in range(N_PAIRS)], axis=1)
    lse_ref[0] = lse_tile


def _attn_prompt_call(q, k, v, bias, dil):
    b, L, _ = k.shape
    nblk = L // QB
    d = D_MODEL
    cur = pl.BlockSpec((1, QB, d), lambda bb, r, i: (bb, i, r))
    prev = pl.BlockSpec((1, QB, d), lambda bb, r, i: (bb, jnp.maximum(i - 1, 0), r))
    return pl.pallas_call(
        _attn_prompt_kernel,
        out_shape=[jax.ShapeDtypeStruct((b, L, dil * d), F32), jax.ShapeDtypeStruct((b, L, dil * LANES), F32)],
        grid=(b, dil, nblk),
        in_specs=[cur, prev, cur, prev, cur, pl.BlockSpec(bias.shape, lambda bb, r, i: (0, 0, 0, 0))],
        out_specs=[cur, pl.BlockSpec((1, QB, LANES), lambda bb, r, i: (bb, i, r))],
        compiler_params=_cparams(("parallel", "parallel", "parallel")),
        name=f"attn_prompt_d{dil}",
    )(q, k, k, v, v, bias)


def _attn_merge_kernel(o0_ref, o1_ref, o2_ref, l0_ref, l1_ref, l2_ref, e_ref, wo_ref, x_ref, gt_ref, out_ref,
                       so1, so2, sl1, sl2):
    o1 = _load_dilated(o1_ref, so1, DILATIONS[1])
    o2 = _load_dilated(o2_ref, so2, DILATIONS[2])
    l0 = l0_ref[0]
    l1 = _load_dilated(l1_ref, sl1, DILATIONS[1])
    l2 = _load_dilated(l2_ref, sl2, DILATIONS[2])
    mx = jnp.maximum(jnp.maximum(l0, l1), l2)
    e0, e1, e2 = jnp.exp(l0 - mx), jnp.exp(l1 - mx), jnp.exp(l2 - mx)
    inv = 1.0 / (e0 + e1 + e2)
    em = e_ref[...]
    o = _dot3(e0 * inv, em) * o0_ref[0] + _dot3(e1 * inv, em) * o1 + _dot3(e2 * inv, em) * o2
    out_ref[0] = x_ref[0] + gt_ref[0] * _dot(o, wo_ref[...])


def _head_expand_matrix():
    return jnp.asarray(np.arange(LANES)[:, None] == (np.arange(D_MODEL) // HEAD)[None, :], BF16)


def _attn_merge_call(os_, ls_, wo, x, gt, tm):
    b, t, d = x.shape
    em = _head_expand_matrix()
    o_specs = [_dilated_spec(tm, d, dil) for dil in DILATIONS]
    l_specs = [_dilated_spec(tm, LANES, dil) for dil in DILATIONS]
    o_tile = pltpu.VMEM((N_PAIRS, tm, LANES), F32)
    l_tile = pltpu.VMEM((1, tm, LANES), F32)
    return pl.pallas_call(
        _attn_merge_kernel,
        out_shape=jax.ShapeDtypeStruct((b, t, d), F32),
        grid=(b, t // tm),
        in_specs=o_specs + l_specs + [_const_spec(em), _const_spec(wo), _row_spec(x, tm), _row_spec(gt, tm)],
        out_specs=pl.BlockSpec((1, tm, d), lambda bb, i: (bb, i, 0)),
        scratch_shapes=[o_tile, o_tile, l_tile, l_tile],
        compiler_params=_cparams(("parallel", "parallel")),
        name="attn_merge_out",
    )(*os_, *ls_, em, wo, x, gt)


SAMPLE_HEADS_PER_STEP = 8


def _attn_sample_kernel(q_ref, kn_ref, vn_ref, k_ref, v_ref, bias_ref, b0_ref, o_ref):
    n_buf = k_ref.shape[-1]
    hb = SAMPLE_HEADS_PER_STEP
    ng = len(DILATIONS)
    lane = lax.broadcasted_iota(jnp.int32, (HEAD, LANES), 1)
    los = [n_buf - BAND * dil for dil in DILATIONS]
    chains = [(hh, g) for hh in range(hb) for g in range(ng)]
    q = {(hh, g): q_ref[0, g, 0][:, hh:hh + 1] * SCALE_B for hh, g in chains}
    kn = [kn_ref[0, 0][:, hh:hh + 1] for hh in range(hb)]
    vn = [vn_ref[0, 0][:, hh:hh + 1] for hh in range(hb)]
    lk = {(hh, g): jnp.sum(k_ref[0, hh, :, los[g]:] * q[hh, g], axis=0, keepdims=True)
          + bias_ref[g, hh:hh + 1, los[g]:] for hh, g in chains}
    l0 = {(hh, g): jnp.sum(kn[hh] * q[hh, g], axis=0, keepdims=True) + b0_ref[g, 0][0:1, hh:hh + 1]
          for hh, g in chains}
    m = {c: jnp.maximum(jnp.max(lk[c], axis=1, keepdims=True), l0[c]) for c in chains}
    pk = {c: jnp.exp(lk[c] - m[c]) for c in chains}
    p0 = {c: jnp.exp(l0[c] - m[c]) for c in chains}
    ssum = {c: jnp.sum(pk[c], axis=1, keepdims=True) + p0[c] for c in chains}
    num = {(hh, g): jnp.sum(v_ref[0, hh, :, los[g]:] * pk[hh, g], axis=1, keepdims=True) + p0[hh, g] * vn[hh]
           for hh, g in chains}
    o_tile = jnp.zeros((HEAD, LANES), F32)
    for hh in range(hb):
        mx = jnp.maximum(jnp.maximum(m[hh, 0], m[hh, 1]), m[hh, 2])
        cs = [jnp.exp(m[hh, g] - mx) for g in range(ng)]
        inv = 1.0 / (cs[0] * ssum[hh, 0] + cs[1] * ssum[hh, 1] + cs[2] * ssum[hh, 2])
        o_col = (cs[0] * inv) * num[hh, 0] + (cs[1] * inv) * num[hh, 1] + (cs[2] * inv) * num[hh, 2]
        o_tile = jnp.where(lane == hh, o_col, o_tile)
    o_ref[0, 0] = o_tile


def _attn_sample_call(q, kn, vn, cache_k, cache_v, bias_p, bias_0):
    nb, n_buf = cache_k.shape[:2]
    assert n_buf == BAND * max(DILATIONS)
    hb = SAMPLE_HEADS_PER_STEP
    nhb = N_HEADS // hb
    ng = len(DILATIONS)
    kt = jnp.transpose(cache_k, (0, 2, 3, 1))
    vt = jnp.transpose(cache_v, (0, 2, 3, 1))
    qt = jnp.transpose(q.reshape(nb, ng, nhb, hb, HEAD), (0, 1, 2, 4, 3))
    knt = jnp.transpose(kn.reshape(nb, nhb, hb, HEAD), (0, 1, 3, 2))
    vnt = jnp.transpose(vn.reshape(nb, nhb, hb, HEAD), (0, 1, 3, 2))
    o = pl.pallas_call(
        _attn_sample_kernel,
        out_shape=jax.ShapeDtypeStruct((nb, nhb, HEAD, LANES), F32),
        grid=(nb, nhb),
        in_specs=[pl.BlockSpec((1, ng, 1, HEAD, hb), lambda b, j: (b, 0, j, 0, 0)),
                  pl.BlockSpec((1, 1, HEAD, hb), lambda b, j: (b, j, 0, 0)),
                  pl.BlockSpec((1, 1, HEAD, hb), lambda b, j: (b, j, 0, 0)),
                  pl.BlockSpec((1, hb, HEAD, n_buf), lambda b, j: (b, j, 0, 0)),
                  pl.BlockSpec((1, hb, HEAD, n_buf), lambda b, j: (b, j, 0, 0)),
                  pl.BlockSpec((ng, hb, n_buf), lambda b, j: (0, j, 0)),
                  pl.BlockSpec((ng, 1, 8, hb), lambda b, j: (0, j, 0, 0))],
        out_specs=pl.BlockSpec((1, 1, HEAD, LANES), lambda b, j: (b, j, 0, 0)),
        compiler_params=_cparams(("parallel", "parallel")),
        name="attn_sample",
    )(qt, knt, vnt, kt, vt, bias_p, bias_0)
    return jnp.transpose(o[..., :hb], (0, 1, 3, 2)).reshape(nb, N_HEADS * HEAD)


def _prompt_bias(rel_bias, gi, dil):
    table = rel_bias[:, gi * N_HEADS:(gi + 1) * N_HEADS].astype(F32)
    per_dist = table[_t5_bucket(np.arange(BAND + 1) * dil)]
    delta = (lax.broadcasted_iota(jnp.int32, (QB, 2 * QB), 0) + QB
             - lax.broadcasted_iota(jnp.int32, (QB, 2 * QB), 1))
    onehot = (delta[:, :, None] == jnp.arange(BAND + 1, dtype=jnp.int32)[None, None, :]).astype(F32)
    bias = jnp.einsum("qkj,jh->hqk", onehot, per_dist, precision=lax.Precision.HIGHEST)
    in_band = (delta >= 0) & (delta <= BAND)
    bias = jnp.where(in_band[None], bias, NEG_INF)
    has_prev = lax.broadcasted_iota(jnp.int32, (QB, 2 * QB), 1) >= QB
    return jnp.stack([bias, jnp.where(has_prev[None], bias, NEG_INF)])


def _sample_bias(rel_bias, n_buf):
    hb = SAMPLE_HEADS_PER_STEP
    dist = n_buf - np.arange(n_buf)
    bps, b0s = [], []
    for gi, dil in enumerate(DILATIONS):
        table = rel_bias[:, gi * N_HEADS:(gi + 1) * N_HEADS].astype(F32)
        in_window = (dist % dil == 0) & (dist <= BAND * dil)
        bps.append(jnp.where(in_window[None, :], jnp.transpose(table[_t5_bucket(dist)]), NEG_INF))
        b0s.append(jnp.broadcast_to(table[0].reshape(N_HEADS // hb, 1, hb), (N_HEADS // hb, 8, hb)))
    return jnp.stack(bps), jnp.stack(b0s)


def _trunk(x, mods, wkv0, shift0, caches, p, tm, tm_moe):
    b, t, d = x.shape
    m0, m1, mkv = mods["l0"], mods["l1"], mods["kv"]

    sh_m, sc_m, gt_m, sh_c, sc_c, gt_c = m0
    r, lw, k, v, a, bb, g, bonus = _rwkv_proj_call(x, shift0, p["g_norm"][0, 0], sh_m, sc_m, p["rw_mu"], p["rw_vecs"],
                                                   p["rw_w_rkv"], p["rw_w1"], p["rw_w2"], p["rw_a1"], p["rw_a2"],
                                                   p["rw_g1"], p["rw_g2"], tm)
    if wkv0 is None:
        y, wkv = _rwkv_scan_call(r, lw, k, v, a, bb)
        shift = _norm_mod_call(x[:, t - PREV_ROWS:], p["g_norm"][0, 0], sh_m, sc_m, PREV_ROWS)[:, -1]
    else:
        y, wkv = _rwkv_step_call(wkv0, r, lw, k, v, a, bb)
        shift = _norm_mod_call(x, p["g_norm"][0, 0], sh_m, sc_m, tm)[0]
    x = _rwkv_out_call(y, bonus, g, x, gt_m, p["rw_ln"], p["rw_w_o"], tm)
    x = _moe_call(x, p["g_norm"][0, 1], sh_c, sc_c, gt_c, p["moe_wr"][0], p["moe_br"][0],
                  p["moe_w_gu"], p["moe_w_down"], 0, tm, tm_moe)

    sh_k, sc_k = mkv
    sh_m, sc_m, gt_m, sh_c, sc_c, gt_c = m1
    if caches is None:
        k_sh, v_sh, k1, v1, k2, v2 = _kvproj_call(x, p["w_kv"], p["g_kv"], sh_k, sc_k, tm)
        qs = _qproj_call(x, p["at_w_q"], p["g_norm"][1, 0], sh_m, sc_m, tm)
        ks, vs = (k_sh, k1, k2), (v_sh, v1, v2)
        os_, ls_ = [], []
        for gi, dil in enumerate(DILATIONS):
            o, lse = _attn_prompt_call(qs[gi], ks[gi], vs[gi], _prompt_bias(p["rel_bias"], gi, dil), dil)
            os_.append(o)
            ls_.append(lse)
        x = _attn_merge_call(os_, ls_, p["at_w_o"], x, gt_m, tm)
    else:
        kvp = _linear_call(x, p["w_kv"], tm, norm=(p["g_kv"], sh_k, sc_k), name="kv_proj")
        k_sh, v_sh = kvp[..., :d], kvp[..., d:]
        q = _linear_call(x, p["at_w_q"], tm, norm=(p["g_norm"][1, 0], sh_m, sc_m), name="q_proj")
        cache_k, cache_v = caches
        nb, n_buf = cache_k.shape[:2]
        bias_p, bias_0 = _sample_bias(p["rel_bias"], n_buf)
        o = _attn_sample_call(q.reshape(nb, len(DILATIONS), N_HEADS, HEAD), k_sh.reshape(nb, N_HEADS, HEAD),
                              v_sh.reshape(nb, N_HEADS, HEAD), cache_k, cache_v, bias_p, bias_0).reshape(b, t, d)
        x = _linear_call(o, p["at_w_o"], tm, resid=(x, gt_m), name="attn_out")
    y = _moe_call(x, p["g_norm"][1, 1], sh_c, sc_c, gt_c, p["moe_wr"][1], p["moe_br"][1],
                  p["moe_w_gu"], p["moe_w_down"], 1, tm, tm_moe, final_g=p["g_final"])
    return y, wkv, shift, k_sh, v_sh


def _prepare_params(w_ada, b_ada, g_norm, rw_mu, rw_w_rkv, rw_w0, rw_w1, rw_w2, rw_a0, rw_a1, rw_a2, rw_g1, rw_g2,
                    rw_k_k, rw_k_a, rw_r_k, rw_ln_w, rw_ln_b, rw_w_o, moe_w_rg, moe_b_rg, moe_w_re, moe_b_re,
                    moe_w_gu, moe_w_down, w_ada_kv, b_ada_kv, g_kv, w_kv, at_w_q, at_w_o, rel_bias, g_final):
    d = D_MODEL
    zeros = jnp.zeros((3, d), F32)
    pad = LANES - N_GROUPS_E - N_EXPERTS
    depth = moe_w_rg.shape[0]
    return dict(
        g_norm=g_norm, g_kv=g_kv, g_final=g_final, rel_bias=rel_bias,
        rw_mu=rw_mu[0],
        rw_vecs=jnp.concatenate([rw_w0[0][None], rw_a0[0][None], rw_k_k[0][None], rw_k_a[0][None],
                                 rw_r_k[0].reshape(1, d), zeros], axis=0),
        rw_w_rkv=rw_w_rkv[0].astype(BF16), rw_w1=rw_w1[0], rw_w2=rw_w2[0], rw_a1=rw_a1[0], rw_a2=rw_a2[0],
        rw_g1=rw_g1[0], rw_g2=rw_g2[0],
        rw_ln=jnp.concatenate([rw_ln_w[0][None], rw_ln_b[0][None], zeros, zeros], axis=0),
        rw_w_o=rw_w_o[0].astype(BF16),
        moe_wr=jnp.pad(jnp.concatenate([moe_w_rg, moe_w_re], axis=-1), ((0, 0), (0, 0), (0, pad))),
        moe_br=jnp.pad(jnp.concatenate([moe_b_rg, moe_b_re], axis=-1), ((0, 0), (0, pad)))[:, None, :],
        moe_w_gu=moe_w_gu.reshape(depth * N_EXPERTS, d, 2 * D_EXPERT),
        moe_w_down=moe_w_down.reshape(depth * N_EXPERTS, D_EXPERT, d),
        w_kv=w_kv.astype(BF16), at_w_q=at_w_q[0].astype(BF16), at_w_o=at_w_o[0].astype(BF16),
    )


def kernel(x_prompt, x_sample, state_wkv, state_shift, cache_k, cache_v, c_prompt, c_sample, w_ada, b_ada, g_norm, rw_mu, rw_w_rkv, rw_w0, rw_w1, rw_w2, rw_a0, rw_a1, rw_a2, rw_g1, rw_g2, rw_k_k, rw_k_a, rw_r_k, rw_ln_w, rw_ln_b, rw_w_o, moe_w_rg, moe_b_rg, moe_w_re, moe_b_re, moe_w_gu, moe_w_down, w_ada_kv, b_ada_kv, g_kv, w_kv, at_w_q, at_w_o, rel_bias, g_final):
    d = D_MODEL
    bp, tp = x_prompt.shape[:2]
    nb = x_sample.shape[0]
    p = _prepare_params(w_ada, b_ada, g_norm, rw_mu, rw_w_rkv, rw_w0, rw_w1, rw_w2, rw_a0, rw_a1, rw_a2, rw_g1,
                        rw_g2, rw_k_k, rw_k_a, rw_r_k, rw_ln_w, rw_ln_b, rw_w_o, moe_w_rg, moe_b_rg, moe_w_re,
                        moe_b_re, moe_w_gu, moe_w_down, w_ada_kv, b_ada_kv, g_kv, w_kv, at_w_q, at_w_o, rel_bias,
                        g_final)

    n_c = bp + nb
    n_pad = -n_c % 8
    c_all = jnp.concatenate([c_prompt, c_sample, jnp.zeros((n_pad, d), F32)], axis=0)
    mod_l = [_ada_linear(c_all, w_ada, b_ada, l) for l in range(2)]
    mod_kv = _ada_linear(c_all, w_ada_kv[None], b_ada_kv[None], 0)

    def split(m, n, lo, hi, per_batch):
        parts = [m[lo:hi, d * j:d * (j + 1)] for j in range(n)]
        return [q[:, None, :] if per_batch else q[None] for q in parts]

    mods_p = dict(l0=split(mod_l[0], 6, 0, bp, True), l1=split(mod_l[1], 6, 0, bp, True),
                  kv=split(mod_kv, 2, 0, bp, True))
    mods_s = dict(l0=split(mod_l[0], 6, bp, n_c, False), l1=split(mod_l[1], 6, bp, n_c, False),
                  kv=split(mod_kv, 2, bp, n_c, False))

    y_p, wkv_p, shift_p, k_p, v_p = _trunk(x_prompt, mods_p, None, None, None, p, tm=256, tm_moe=512)
    keep = min(BAND * max(DILATIONS), tp)
    k_prompt = k_p[:, tp - keep:].reshape(bp, keep, N_HEADS, HEAD)
    v_prompt = v_p[:, tp - keep:].reshape(bp, keep, N_HEADS, HEAD)

    xs = x_sample.reshape(1, nb, d)
    y_s, wkv_s, shift_s, k_s, v_s = _trunk(xs, mods_s, state_wkv[0], state_shift[0][None], (cache_k, cache_v), p,
                                           tm=nb, tm_moe=nb)
    return (y_p, y_s.reshape(nb, 1, d), wkv_p[None], shift_p[None], k_prompt, v_prompt,
            wkv_s[None], shift_s[None], k_s.reshape(nb, 1, N_HEADS, HEAD), v_s.reshape(nb, 1, N_HEADS, HEAD))
```

```python
import functools

import numpy as np
import jax
import jax.numpy as jnp
from jax import lax
from jax.experimental import pallas as pl
from jax.experimental.pallas import tpu as pltpu

F32 = jnp.float32
BF16 = jnp.bfloat16

D_MODEL = 1024
N_HEADS = 16
HEAD = 64
LANES = 128
N_PAIRS = D_MODEL // LANES
N_MIX = 6
GN_EPS = 64e-5
RMS_EPS = 1e-6
NEG_INF = -1e30
DILATIONS = (1, 4, 16)
BAND = 128
QB = 128
SCALE_B = HEAD ** -0.5
NUM_BUCKETS = 32
MAX_DISTANCE = 2048
N_GROUPS_E = 4
N_EXP_PER_GROUP = 8
N_EXPERTS = N_GROUPS_E * N_EXP_PER_GROUP
D_EXPERT = 512
ROUTER_LANE0 = N_GROUPS_E
SCAN_CHUNK = 64

VMEM_LIMIT = 56 * 1024 * 1024


def _cparams(sem):
    return pltpu.CompilerParams(dimension_semantics=sem, vmem_limit_bytes=VMEM_LIMIT)


def _dot(a, b):
    return jnp.dot(a.astype(BF16), b.astype(BF16), preferred_element_type=F32)


def _dot_nt(a, b):
    return lax.dot_general(a.astype(BF16), b.astype(BF16), (((1,), (1,)), ((), ())),
                           preferred_element_type=F32)


def _split3(a):
    hi = a.astype(BF16)
    r1 = a - hi.astype(F32)
    mid = r1.astype(BF16)
    lo = (r1 - mid.astype(F32)).astype(BF16)
    return hi, mid, lo


def _dot3(a, b01):
    hi, mid, lo = _split3(a)
    b = b01.astype(BF16)
    return (jnp.dot(hi, b, preferred_element_type=F32) + jnp.dot(mid, b, preferred_element_type=F32)
            + jnp.dot(lo, b, preferred_element_type=F32))


def _dot3_left(a01, b):
    hi, mid, lo = _split3(b)
    a = a01.astype(BF16)
    return (jnp.dot(a, hi, preferred_element_type=F32) + jnp.dot(a, mid, preferred_element_type=F32)
            + jnp.dot(a, lo, preferred_element_type=F32))


def _sigmoid(x):
    return 1.0 / (1.0 + jnp.exp(-x))


def _silu(x):
    return x * _sigmoid(x)


def _softplus(z):
    return jnp.maximum(z, 0.0) + jnp.log(1.0 + jnp.exp(-jnp.abs(z)))


def _norm_mod(x, g, sh, sc):
    ms = jnp.mean(x * x, axis=-1, keepdims=True)
    return (x * lax.rsqrt(ms + RMS_EPS) * g) * (1.0 + sc) + sh


def _head_sum_matrix():
    r = lax.broadcasted_iota(jnp.int32, (LANES, LANES), 0) >= HEAD
    c = lax.broadcasted_iota(jnp.int32, (LANES, LANES), 1) >= HEAD
    return jnp.where(r == c, 1.0, 0.0).astype(BF16)


def _head_sum(z, m128):
    parts = [_dot3(z[:, LANES * p:LANES * (p + 1)], m128) for p in range(N_PAIRS)]
    return jnp.concatenate(parts, axis=1)


def _row_spec(arr, tm):
    n = arr.shape[-1]
    if arr.shape[1] == 1:
        return pl.BlockSpec((1, 1, n), lambda b, i: (b, 0, 0))
    return pl.BlockSpec((1, tm, n), lambda b, i: (b, i, 0))


def _const_spec(arr):
    nd = arr.ndim
    return pl.BlockSpec(arr.shape, lambda b, i, _nd=nd: (0,) * _nd)


def _ada_kernel(c_ref, w_ref, b_ref, o_ref):
    o_ref[...] = _dot(_silu(c_ref[...]), w_ref[0]) + b_ref[0]


def _ada_linear(c_all, w, b, layer):
    m, k = c_all.shape
    n = w.shape[2]
    tn = 1024
    return pl.pallas_call(
        _ada_kernel,
        out_shape=jax.ShapeDtypeStruct((m, n), F32),
        grid=(n // tn,),
        in_specs=[pl.BlockSpec((m, k), lambda j: (0, 0)),
                  pl.BlockSpec((1, k, tn), lambda j: (layer, 0, j)),
                  pl.BlockSpec((1, 1, tn), lambda j: (layer, 0, j))],
        out_specs=pl.BlockSpec((m, tn), lambda j: (0, j)),
        compiler_params=_cparams(("parallel",)),
        name="ada_linear",
    )(c_all, w, b.reshape(b.shape[0], 1, n))


def _norm_mod_kernel(x_ref, g_ref, sh_ref, sc_ref, o_ref):
    o_ref[0] = _norm_mod(x_ref[0], g_ref[...], sh_ref[0], sc_ref[0])


def _norm_mod_call(x, g, sh, sc, tm):
    b, t, d = x.shape
    g2 = g.reshape(1, d)
    return pl.pallas_call(
        _norm_mod_kernel,
        out_shape=jax.ShapeDtypeStruct(x.shape, F32),
        grid=(b, t // tm),
        in_specs=[_row_spec(x, tm), _const_spec(g2), _row_spec(sh, tm), _row_spec(sc, tm)],
        out_specs=pl.BlockSpec((1, tm, d), lambda bb, i: (bb, i, 0)),
        compiler_params=_cparams(("parallel", "parallel")),
        name="norm_mod",
    )(x, g2, sh, sc)


def _linear_kernel(*refs, has_norm, has_resid):
    it = iter(refs)
    x_ref = next(it)
    w_ref = next(it)
    if has_norm:
        g_ref, sh_ref, sc_ref = next(it), next(it), next(it)
    if has_resid:
        xr_ref, gt_ref = next(it), next(it)
    o_ref = next(it)
    x = x_ref[0]
    if has_norm:
        x = _norm_mod(x, g_ref[...], sh_ref[0], sc_ref[0])
    out = _dot(x, w_ref[...])
    if has_resid:
        out = xr_ref[0] + gt_ref[0] * out
    o_ref[0] = out


def _linear_call(x, w, tm, norm=None, resid=None, name="linear"):
    b, t, k = x.shape
    n = w.shape[1]
    args = [x, w]
    specs = [_row_spec(x, tm), _const_spec(w)]
    if norm is not None:
        g, sh, sc = norm
        g2 = g.reshape(1, k)
        args += [g2, sh, sc]
        specs += [_const_spec(g2), _row_spec(sh, tm), _row_spec(sc, tm)]
    if resid is not None:
        xr, gt = resid
        args += [xr, gt]
        specs += [_row_spec(xr, tm), _row_spec(gt, tm)]
    return pl.pallas_call(
        functools.partial(_linear_kernel, has_norm=norm is not None, has_resid=resid is not None),
        out_shape=jax.ShapeDtypeStruct((b, t, n), F32),
        grid=(b, t // tm),
        in_specs=specs,
        out_specs=pl.BlockSpec((1, tm, n), lambda bb, i: (bb, i, 0)),
        compiler_params=_cparams(("parallel", "parallel")),
        name=name,
    )(*args)


def _fill_tiles(scr, val):
    for c in range(val.shape[1] // LANES):
        scr[c] = val[:, LANES * c:LANES * (c + 1)]


def _store_dilated(scr, out_ref, dil):
    n_tiles, tm, _ = scr.shape
    n = n_tiles * LANES
    for r in range(dil):
        rows = [scr[c, pl.ds(r, tm // dil, stride=dil), :] for c in range(n_tiles)]
        out_ref[0, :, n * r:n * (r + 1)] = jnp.concatenate(rows, axis=1).astype(out_ref.dtype)


def _load_dilated(in_ref, scr, dil):
    n_tiles, tm, _ = scr.shape
    n = n_tiles * LANES
    for r in range(dil):
        blk = in_ref[0, :, n * r:n * (r + 1)].astype(F32)
        for c in range(n_tiles):
            scr[c, pl.ds(r, tm // dil, stride=dil), :] = blk[:, LANES * c:LANES * (c + 1)]
    return jnp.concatenate([scr[c] for c in range(n_tiles)], axis=1)


def _qproj_kernel(x_ref, g_ref, sh_ref, sc_ref, w_ref, q0_ref, q1_ref, q2_ref, s1, s2):
    d = D_MODEL
    q = _dot(_norm_mod(x_ref[0], g_ref[...], sh_ref[0], sc_ref[0]), w_ref[...]) * SCALE_B
    q0_ref[0] = q[:, :d].astype(q0_ref.dtype)
    _fill_tiles(s1, q[:, d:2 * d])
    _store_dilated(s1, q1_ref, DILATIONS[1])
    _fill_tiles(s2, q[:, 2 * d:])
    _store_dilated(s2, q2_ref, DILATIONS[2])


def _dilated_shape(b, t, n, dil, dtype):
    return jax.ShapeDtypeStruct((b, t // dil, dil * n), dtype)


def _dilated_spec(tm, n, dil):
    return pl.BlockSpec((1, tm // dil, dil * n), lambda bb, i: (bb, i, 0))


def _qproj_call(x, w, g, sh, sc, tm):
    b, t, d = x.shape
    g2 = g.reshape(1, d)
    tile = pltpu.VMEM((N_PAIRS, tm, LANES), F32)
    return pl.pallas_call(
        _qproj_kernel,
        out_shape=[_dilated_shape(b, t, d, dil, BF16) for dil in DILATIONS],
        grid=(b, t // tm),
        in_specs=[_row_spec(x, tm), _const_spec(g2), _row_spec(sh, tm), _row_spec(sc, tm), _const_spec(w)],
        out_specs=[_dilated_spec(tm, d, dil) for dil in DILATIONS],
        scratch_shapes=[tile, tile],
        compiler_params=_cparams(("parallel", "parallel")),
        name="q_proj",
    )(x, g2, sh, sc, w)


def _kvproj_kernel(x_ref, g_ref, sh_ref, sc_ref, w_ref, k_ref, v_ref, k1_ref, v1_ref, k2_ref, v2_ref, sk, sv):
    d = D_MODEL
    kv = _dot(_norm_mod(x_ref[0], g_ref[...], sh_ref[0], sc_ref[0]), w_ref[...])
    k_ref[0] = kv[:, :d]
    v_ref[0] = kv[:, d:]
    _fill_tiles(sk, kv[:, :d])
    _fill_tiles(sv, kv[:, d:])
    _store_dilated(sk, k1_ref, DILATIONS[1])
    _store_dilated(sv, v1_ref, DILATIONS[1])
    _store_dilated(sk, k2_ref, DILATIONS[2])
    _store_dilated(sv, v2_ref, DILATIONS[2])


def _kvproj_call(x, w, g, sh, sc, tm):
    b, t, d = x.shape
    g2 = g.reshape(1, d)
    tile = pltpu.VMEM((N_PAIRS, tm, LANES), F32)
    plain = jax.ShapeDtypeStruct((b, t, d), F32)
    return pl.pallas_call(
        _kvproj_kernel,
        out_shape=[plain, plain] + [_dilated_shape(b, t, d, dil, BF16) for dil in DILATIONS[1:] for _ in range(2)],
        grid=(b, t // tm),
        in_specs=[_row_spec(x, tm), _const_spec(g2), _row_spec(sh, tm), _row_spec(sc, tm), _const_spec(w)],
        out_specs=[_dilated_spec(tm, d, 1)] * 2 + [_dilated_spec(tm, d, dil) for dil in DILATIONS[1:] for _ in range(2)],
        scratch_shapes=[tile, tile],
        compiler_params=_cparams(("parallel", "parallel")),
        name="kv_proj",
    )(x, g2, sh, sc, w)


PREV_ROWS = 8


def _rwkv_proj_kernel(x_ref, aux_ref, gn_ref, sh_ref, sc_ref, mu_ref, vec_ref, wrkv_ref, w1_ref, w2_ref, a1_ref, a2_ref,
                      g1_ref, g2_ref, r_ref, lw_ref, k_ref, v_ref, a_ref, b_ref, g_ref, bonus_ref, *, sequence):
    h = _norm_mod(x_ref[0], gn_ref[...], sh_ref[0], sc_ref[0])
    if sequence:
        prev = _norm_mod(aux_ref[0], gn_ref[...], sh_ref[0], sc_ref[0])[PREV_ROWS - 1:PREV_ROWS, :]
        prev = jnp.where(pl.program_id(1) > 0, prev, 0.0)
        row = lax.broadcasted_iota(jnp.int32, h.shape, 0)
        hp = jnp.where(row == 0, prev, pltpu.roll(h, 1, axis=0))
    else:
        hp = aux_ref[0]
    xx = hp - h

    def mix(i):
        return h + xx * mu_ref[i:i + 1, :]

    w0, a0, k_k, k_a, r_k = (vec_ref[i:i + 1, :] for i in range(5))
    m128 = _head_sum_matrix()
    r = _dot(mix(0), wrkv_ref[0])
    k = _dot(mix(1), wrkv_ref[1])
    v = _dot(mix(2), wrkv_ref[2])
    wl = w0 + _dot(jnp.tanh(_dot(mix(3), w1_ref[...])), w2_ref[...])
    w_log = -_softplus(-wl) - 0.5
    a_sig = _sigmoid(a0 + _dot(_dot(mix(4), a1_ref[...]), a2_ref[...]))
    g = _dot(_sigmoid(_dot(mix(5), g1_ref[...])), g2_ref[...])
    kk = k * k_k
    kk = kk / jnp.maximum(jnp.sqrt(_head_sum(kk * kk, m128)), 1e-12)
    k2 = k * (1.0 + (a_sig - 1.0) * k_a)
    r_ref[0] = r
    lw_ref[0] = -jnp.exp(w_log)
    k_ref[0] = k2
    v_ref[0] = v
    a_ref[0] = -kk
    b_ref[0] = kk * a_sig
    g_ref[0] = g
    bonus_ref[0] = _head_sum(r * k2 * r_k, m128) * v


def _rwkv_proj_call(x, shift, gn, sh, sc, mu, vecs, wrkv, w1, w2, a1, a2, g1, g2, tm):
    b, t, d = x.shape
    gn2 = gn.reshape(1, d)
    consts = [mu, vecs, wrkv, w1, w2, a1, a2, g1, g2]
    out = jax.ShapeDtypeStruct((b, t, d), F32)
    ospec = pl.BlockSpec((1, tm, d), lambda bb, i: (bb, i, 0))
    if shift is None:
        per = tm // PREV_ROWS
        aux, aux_spec = x, pl.BlockSpec((1, PREV_ROWS, d), lambda bb, i: (bb, jnp.maximum(i * per - 1, 0), 0))
    else:
        aux, aux_spec = shift, _row_spec(shift, tm)
    return pl.pallas_call(
        functools.partial(_rwkv_proj_kernel, sequence=shift is None),
        out_shape=[out] * 8,
        grid=(b, t // tm),
        in_specs=[_row_spec(x, tm), aux_spec, _const_spec(gn2), _row_spec(sh, tm), _row_spec(sc, tm)]
                 + [_const_spec(c) for c in consts],
        out_specs=[ospec] * 8,
        compiler_params=_cparams(("parallel", "parallel")),
        name="rwkv_proj",
    )(x, aux, gn2, sh, sc, *consts)


def _rwkv_scan_kernel(r_ref, lw_ref, k_ref, v_ref, a_ref, b_ref, y_ref, hout_ref, h_scr):
    c = pl.program_id(0)
    nc = pl.num_programs(0)
    nb = r_ref.shape[0]
    C = SCAN_CHUNK
    n2 = 2 * C

    @pl.when(c == 0)
    def _():
        h_scr[...] = jnp.zeros_like(h_scr)

    ri = lax.broadcasted_iota(jnp.int32, (C, C), 0)
    ci = lax.broadcasted_iota(jnp.int32, (C, C), 1)
    tril = jnp.where(ri >= ci, 1.0, 0.0).astype(BF16)
    head0 = lax.broadcasted_iota(jnp.int32, (C, LANES), 1) < HEAD
    tt = lax.broadcasted_iota(jnp.int32, (n2, n2), 0) & (C - 1)
    ss = lax.broadcasted_iota(jnp.int32, (n2, n2), 1) & (C - 1)
    strict = tt > ss
    incl = tt >= ss
    n_double = C.bit_length() - 1

    def two(xs):
        return jnp.concatenate([jnp.where(head0, xs, 0.0), jnp.where(head0, 0.0, xs)], axis=0)

    ar, bk, v2, w2t, gcol = [], [], [], [], []
    for bi in range(nb):
        lw = lw_ref[bi]
        cum = _dot3_left(tril, lw)
        gam = jnp.exp(cum)
        ginv = jnp.exp(-cum)
        g_last = gam[C - 1:C, :]
        rt = r_ref[bi] * gam
        at = a_ref[bi] * jnp.exp(cum - lw)
        bt = b_ref[bi] * ginv
        kt = k_ref[bi] * ginv
        bh = bt * g_last
        kh = kt * g_last
        v = v_ref[bi]
        for p in range(N_PAIRS):
            sl = slice(LANES * p, LANES * (p + 1))
            ar.append(jnp.concatenate([two(at[:, sl]), two(rt[:, sl])], axis=0).astype(BF16))
            bk.append(jnp.concatenate([two(bt[:, sl]), two(kt[:, sl])], axis=0).astype(BF16))
            v2.append(two(v[:, sl]).astype(BF16))
            w2t.append(jnp.transpose(jnp.concatenate([two(bh[:, sl]), two(kh[:, sl])], axis=0)).astype(BF16))
            gcol.append(jnp.transpose(jnp.broadcast_to(g_last[:, sl], (LANES, LANES))))
    n = len(ar)
    hp = [h_scr[i] for i in range(n)]
    s = [_dot_nt(ar[i], bk[i]) for i in range(n)]
    arh = [_dot(ar[i], hp[i]) for i in range(n)]
    pw = [jnp.where(strict, s[i][:n2, :n2], 0.0).astype(BF16) for i in range(n)]
    lt = [jnp.concatenate([jnp.where(strict, s[i][:n2, n2:], 0.0), jnp.where(incl, s[i][n2:, n2:], 0.0)],
                          axis=0) for i in range(n)]
    t_rb = [jnp.where(incl, s[i][n2:, :n2], 0.0) for i in range(n)]
    lv = [_dot(lt[i], v2[i]) for i in range(n)]
    u = [arh[i][:n2] + lv[i][:n2] for i in range(n)]
    for it in range(n_double):
        pu = [_dot(pw[i], u[i]) for i in range(n)]
        if it + 1 < n_double:
            pw = [_dot(pw[i], pw[i]).astype(BF16) for i in range(n)]
        u = [u[i] + pu[i] for i in range(n)]
    tu = [_dot(t_rb[i], u[i]) for i in range(n)]
    hn = [_dot(w2t[i], jnp.concatenate([u[i].astype(BF16), v2[i]], axis=0)) for i in range(n)]
    for i in range(n):
        h_scr[i] = gcol[i] * hp[i] + hn[i]
    for bi in range(nb):
        ys = []
        for p in range(N_PAIRS):
            i = bi * N_PAIRS + p
            y2 = arh[i][n2:] + lv[i][n2:] + tu[i]
            ys.append(y2[:C] + y2[C:])
        y_ref[bi] = jnp.concatenate(ys, axis=1)

    @pl.when(c == nc - 1)
    def _():
        hout_ref[...] = h_scr[...]


def _rwkv_scan_call(r, lw, k, v, a, bb):
    b, t, d = r.shape
    C = SCAN_CHUNK
    spec = pl.BlockSpec((b, C, d), lambda c: (0, c, 0))
    y, hout = pl.pallas_call(
        _rwkv_scan_kernel,
        out_shape=[jax.ShapeDtypeStruct((b, t, d), F32),
                   jax.ShapeDtypeStruct((b * N_PAIRS, LANES, LANES), F32)],
        grid=(t // C,),
        in_specs=[spec] * 6,
        out_specs=[spec, pl.BlockSpec((b * N_PAIRS, LANES, LANES), lambda c: (0, 0, 0))],
        scratch_shapes=[pltpu.VMEM((b * N_PAIRS, LANES, LANES), F32)],
        compiler_params=_cparams(("arbitrary",)),
        name="rwkv_scan",
    )(r, lw, k, v, a, bb)
    hh = hout.reshape(b, N_PAIRS, 2, HEAD, 2, HEAD)
    hh = jnp.stack([hh[:, :, 0, :, 0, :], hh[:, :, 1, :, 1, :]], axis=2).reshape(b, N_HEADS, HEAD, HEAD)
    return y, jnp.swapaxes(hh, -1, -2)


def _rwkv_step_kernel(s_ref, z_ref, so_ref, y_ref):
    r, lw, k, v, a, b = (z_ref[c, 0] for c in range(6))
    w = jnp.exp(lw)
    ys = []
    for i in range(HEAD):
        s = s_ref[0, i]
        sa = jnp.sum(s * a, axis=0, keepdims=True)
        sn = s * w + sa * b + v[i:i + 1, :] * k
        so_ref[0, i] = sn
        ys.append(jnp.sum(sn * r, axis=0, keepdims=True))
    y_ref[0] = jnp.concatenate(ys, axis=0)


def _rwkv_step_call(state, r, lw, k, v, a, bb):
    nb = state.shape[0]
    st = jnp.transpose(state, (1, 2, 3, 0))
    z = jnp.stack([jnp.transpose(q[0]).reshape(N_HEADS, HEAD, nb) for q in (r, lw, k, v, a, bb)])
    so, y = pl.pallas_call(
        _rwkv_step_kernel,
        out_shape=[jax.ShapeDtypeStruct((N_HEADS, HEAD, HEAD, nb), F32),
                   jax.ShapeDtypeStruct((N_HEADS, HEAD, nb), F32)],
        grid=(N_HEADS,),
        in_specs=[pl.BlockSpec((1, HEAD, HEAD, nb), lambda h: (h, 0, 0, 0)),
                  pl.BlockSpec((6, 1, HEAD, nb), lambda h: (0, h, 0, 0))],
        out_specs=[pl.BlockSpec((1, HEAD, HEAD, nb), lambda h: (h, 0, 0, 0)),
                   pl.BlockSpec((1, HEAD, nb), lambda h: (h, 0, 0))],
        compiler_params=_cparams(("parallel",)),
        name="rwkv_step",
    )(st, z)
    return jnp.transpose(y.reshape(D_MODEL, nb))[None], jnp.transpose(so, (3, 0, 1, 2))


def _rwkv_out_kernel(y_ref, bonus_ref, g_ref, x_ref, gt_ref, ln_ref, wo_ref, o_ref):
    y = y_ref[0]
    m128 = _head_sum_matrix()
    mean = _head_sum(y, m128) * (1.0 / HEAD)
    dlt = y - mean
    var = _head_sum(dlt * dlt, m128) * (1.0 / HEAD)
    yn = dlt * lax.rsqrt(var + GN_EPS) * ln_ref[0:1, :] + ln_ref[1:2, :]
    out = _dot((yn + bonus_ref[0]) * g_ref[0], wo_ref[...])
    o_ref[0] = x_ref[0] + gt_ref[0] * out


def _rwkv_out_call(y, bonus, g, x, gt, ln, wo, tm):
    b, t, d = y.shape
    return pl.pallas_call(
        _rwkv_out_kernel,
        out_shape=jax.ShapeDtypeStruct((b, t, d), F32),
        grid=(b, t // tm),
        in_specs=[_row_spec(y, tm), _row_spec(bonus, tm), _row_spec(g, tm), _row_spec(x, tm), _row_spec(gt, tm),
                  _const_spec(ln), _const_spec(wo)],
        out_specs=pl.BlockSpec((1, tm, d), lambda bb, i: (bb, i, 0)),
        compiler_params=_cparams(("parallel", "parallel")),
        name="rwkv_out",
    )(y, bonus, g, x, gt, ln, wo)


ROUTE_E, ROUTE_W, ROUTE_RANK = 0, 2, 4


def _router_kernel(x_ref, g_ref, sh_ref, sc_ref, wr_ref, br_ref, hf_ref, route_ref, route_t_ref, cnt_ref, carry):
    first = jnp.logical_and(pl.program_id(0) == 0, pl.program_id(1) == 0)

    @pl.when(first)
    def _():
        carry[...] = jnp.zeros_like(carry)

    hf = _norm_mod(x_ref[0], g_ref[...], sh_ref[0], sc_ref[0])
    hf_ref[0] = hf
    logits = jnp.dot(hf, wr_ref[...], preferred_element_type=F32, precision=lax.Precision.HIGHEST) + br_ref[...]
    lane = lax.broadcasted_iota(jnp.int32, logits.shape, 1).astype(F32)
    big = float(1 << 20)
    gl = jnp.where(lane < N_GROUPS_E, logits, NEG_INF)
    gmax = jnp.max(gl, axis=-1, keepdims=True)
    p_top = 1.0 / jnp.sum(jnp.exp(gl - gmax), axis=-1, keepdims=True)
    g_sel = jnp.min(jnp.where(gl == gmax, lane, big), axis=-1, keepdims=True)
    lo = ROUTER_LANE0 + N_EXP_PER_GROUP * g_sel
    el = jnp.where((lane >= lo) & (lane < lo + N_EXP_PER_GROUP), logits, NEG_INF)
    v1 = jnp.max(el, axis=-1, keepdims=True)
    i1 = jnp.min(jnp.where(el == v1, lane, big), axis=-1, keepdims=True)
    el2 = jnp.where(lane == i1, NEG_INF, el)
    v2 = jnp.max(el2, axis=-1, keepdims=True)
    i2 = jnp.min(jnp.where(el2 == v2, lane, big), axis=-1, keepdims=True)
    e21 = jnp.exp(v2 - v1)
    w1 = p_top / (1.0 + e21)
    w2 = p_top * e21 / (1.0 + e21)
    tm = logits.shape[0]
    onehot = jnp.where(lane == i1, 1.0, 0.0) + jnp.where(lane == i2, 1.0, 0.0)
    rr = lax.broadcasted_iota(jnp.int32, (tm, tm), 0)
    cc = lax.broadcasted_iota(jnp.int32, (tm, tm), 1)
    earlier = jnp.where(rr > cc, 1.0, 0.0).astype(BF16)
    before = jnp.dot(earlier, onehot.astype(BF16), preferred_element_type=F32) + carry[0:1, :]
    rank1 = jnp.sum(jnp.where(lane == i1, before, 0.0), axis=-1, keepdims=True)
    rank2 = jnp.sum(jnp.where(lane == i2, before, 0.0), axis=-1, keepdims=True)
    carry[...] = carry[...] + jnp.sum(onehot, axis=0, keepdims=True)
    cnt_ref[...] = carry[...]
    route = jnp.zeros_like(logits)
    for ln, val in ((ROUTE_E, i1 - ROUTER_LANE0), (ROUTE_E + 1, i2 - ROUTER_LANE0), (ROUTE_W, w1), (ROUTE_W + 1, w2),
                    (ROUTE_RANK, rank1), (ROUTE_RANK + 1, rank2)):
        route = jnp.where(lane == ln, val, route)
    route_ref[0] = route
    route_t_ref[...] = jnp.transpose(route)[:8, :]


def _router_call(x, g, sh, sc, wr, br, tm):
    b, t, d = x.shape
    g2 = g.reshape(1, d)
    nt = t // tm
    return pl.pallas_call(
        _router_kernel,
        out_shape=[jax.ShapeDtypeStruct((b, t, d), F32), jax.ShapeDtypeStruct((b, t, LANES), F32),
                   jax.ShapeDtypeStruct((8, b * t), F32), jax.ShapeDtypeStruct((8, LANES), F32)],
        grid=(b, nt),
        in_specs=[_row_spec(x, tm), _const_spec(g2), _row_spec(sh, tm), _row_spec(sc, tm),
                  _const_spec(wr), _const_spec(br)],
        out_specs=[pl.BlockSpec((1, tm, d), lambda bb, i: (bb, i, 0)),
                   pl.BlockSpec((1, tm, LANES), lambda bb, i: (bb, i, 0)),
                   pl.BlockSpec((8, tm), lambda bb, i: (0, bb * nt + i)),
                   pl.BlockSpec((8, LANES), lambda bb, i: (0, 0))],
        scratch_shapes=[pltpu.VMEM((8, LANES), F32)],
        compiler_params=_cparams(("arbitrary", "arbitrary")),
        name="moe_router",
    )(x, g2, sh, sc, wr, br)


def _route_plan(route_ts, counts, tile, layer):
    cnts = [c[0, ROUTER_LANE0:ROUTER_LANE0 + N_EXPERTS].astype(jnp.int32) for c in counts]
    nt = (sum(cnts) + tile - 1) // tile
    tend = jnp.cumsum(nt)
    tstart = tend - nt
    dests, n_pairs = [], 0
    first_row = tstart * tile
    for route_t, cnt in zip(route_ts, cnts):
        e = route_t[ROUTE_E:ROUTE_E + 2].astype(jnp.int32)
        rank = route_t[ROUTE_RANK:ROUTE_RANK + 2].astype(jnp.int32)
        start = jnp.zeros_like(e)
        for ex in range(N_EXPERTS):
            start = jnp.where(e == ex, first_row[ex], start)
        dests.append((start + rank).reshape(-1))
        first_row = first_row + cnt
        n_pairs += 2 * route_t.shape[1]
    n_tiles = -(-n_pairs // tile) + N_EXPERTS
    tid = jnp.arange(n_tiles, dtype=jnp.int32)
    te = jnp.minimum(jnp.sum((tid[:, None] >= tend[None, :]).astype(jnp.int32), axis=1), N_EXPERTS - 1)
    active = (tid < tend[-1]).astype(jnp.int32)
    return dests, te + layer * N_EXPERTS, active, tend, n_tiles


def _dispatch_kernel(dest_ref, tend_ref, *rest, layout, tile):
    hf_refs = rest[:len(layout)]
    xs_ref, zeros, sem, zsem = rest[len(layout):]
    step = pl.program_id(0)

    @pl.when(step == 0)
    def _():
        zeros[...] = jnp.zeros_like(zeros)
        n_tiles = xs_ref.shape[0] // tile
        used = tend_ref[N_EXPERTS - 1]

        def target(j):
            if j < N_EXPERTS:
                return tend_ref[j] > (tend_ref[j - 1] if j > 0 else 0), tend_ref[j] - 1
            return used + (j - N_EXPERTS) < n_tiles, used + (j - N_EXPERTS)

        for j in range(2 * N_EXPERTS):
            ok, tile_idx = target(j)

            @pl.when(ok)
            def _():
                pltpu.make_async_copy(zeros, xs_ref.at[pl.ds(tile_idx * tile, tile), :], zsem).start()
        for j in range(2 * N_EXPERTS):
            ok, _unused = target(j)

            @pl.when(ok)
            def _():
                pltpu.make_async_copy(zeros, xs_ref.at[pl.ds(0, tile), :], zsem).wait()

    for hf_ref, (first, steps, tm, off) in zip(hf_refs, layout):
        @pl.when(jnp.logical_and(step >= first, step < first + steps))
        def _():
            n = steps * tm
            base = off + (step - first) * tm

            def body(t, carry):
                row = hf_ref.at[pl.ds(t, 1), :]
                for c in range(2):
                    dst = dest_ref[base + c * n + t]
                    pltpu.make_async_copy(row, xs_ref.at[pl.ds(dst, 1), :], sem).start()
                return carry

            lax.fori_loop(0, tm, body, 0, unroll=16)
            done = xs_ref.at[pl.ds(0, 2 * tm), :]
            pltpu.make_async_copy(done, done, sem).wait()


def _dispatch_call(hfs, dests, tend, n_rows, tms, tile):
    d = D_MODEL
    layout, specs, first, off = [], [], 0, 0
    for hf, tm in zip(hfs, tms):
        steps = hf.shape[0] // tm
        layout.append((first, steps, tm, off))
        specs.append(pl.BlockSpec((tm, d), lambda i, dest, tend, _f=first, _s=steps: (jnp.clip(i - _f, 0, _s - 1), 0)))
        first += steps
        off += 2 * hf.shape[0]
    return pl.pallas_call(
        functools.partial(_dispatch_kernel, layout=tuple(layout), tile=tile),
        out_shape=jax.ShapeDtypeStruct((n_rows, d), F32),
        grid_spec=pltpu.PrefetchScalarGridSpec(
            num_scalar_prefetch=2,
            grid=(first,),
            in_specs=specs,
            out_specs=pl.BlockSpec(memory_space=pl.ANY),
            scratch_shapes=[pltpu.VMEM((tile, d), F32), pltpu.SemaphoreType.DMA(()), pltpu.SemaphoreType.DMA(())],
        ),
        compiler_params=_cparams(("arbitrary",)),
        name="moe_dispatch",
    )(jnp.concatenate(dests), tend, *hfs)


def _expert_kernel(te_ref, act_ref, xs_ref, wgu_ref, wd_ref, o_ref, wgu_lp, wd_lp):
    i = pl.program_id(0)
    changed = jnp.logical_or(i == 0, te_ref[i] != te_ref[jnp.maximum(i - 1, 0)])

    @pl.when(changed)
    def _():
        wgu_lp[...] = wgu_ref[0].astype(wgu_lp.dtype)
        wd_lp[...] = wd_ref[0].astype(wd_lp.dtype)

    @pl.when(act_ref[i] == 1)
    def _():
        gu = _dot(xs_ref[...], wgu_lp[...])
        act = _silu(gu[:, :D_EXPERT]) * gu[:, D_EXPERT:]
        o_ref[...] = _dot(act, wd_lp[...])

    @pl.when(act_ref[i] == 0)
    def _():
        o_ref[...] = jnp.zeros_like(o_ref)


def _expert_call(xs, te, active, wgu, wd, tile):
    n_rows, d = xs.shape
    return pl.pallas_call(
        _expert_kernel,
        out_shape=jax.ShapeDtypeStruct((n_rows, d), F32),
        grid_spec=pltpu.PrefetchScalarGridSpec(
            num_scalar_prefetch=2,
            grid=(n_rows // tile,),
            in_specs=[pl.BlockSpec((tile, d), lambda i, te, act: (i, 0)),
                      pl.BlockSpec((1, d, 2 * D_EXPERT), lambda i, te, act: (te[i], 0, 0)),
                      pl.BlockSpec((1, D_EXPERT, d), lambda i, te, act: (te[i], 0, 0))],
            out_specs=pl.BlockSpec((tile, d), lambda i, te, act: (i, 0)),
            scratch_shapes=[pltpu.VMEM((d, 2 * D_EXPERT), BF16), pltpu.VMEM((D_EXPERT, d), BF16)],
        ),
        compiler_params=_cparams(("arbitrary",)),
        name="moe_experts",
    )(te, active, xs, wgu, wd)


def _combine_kernel(dest_ref, ys_ref, route_ref, x_ref, gt_ref, *rest, tm, final_norm):
    if final_norm:
        gf_ref, o_ref, buf, sem = rest
    else:
        o_ref, buf, sem = rest
    n_steps = pl.num_programs(0) * pl.num_programs(1)
    n = n_steps * tm
    step = pl.program_id(0) * pl.num_programs(1) + pl.program_id(1)

    def gather(s, slot):
        base = s * tm

        def body(t, carry):
            for c in range(2):
                src = dest_ref[c * n + base + t]
                pltpu.make_async_copy(ys_ref.at[pl.ds(src, 1), :], buf.at[slot, c, pl.ds(t, 1), :],
                                      sem.at[slot]).start()
            return carry

        lax.fori_loop(0, tm, body, 0, unroll=16)

    @pl.when(step == 0)
    def _():
        gather(0, 0)

    @pl.when(step + 1 < n_steps)
    def _():
        gather(step + 1, (step + 1) % 2)

    slot = step % 2
    pltpu.make_async_copy(buf.at[slot], buf.at[slot], sem.at[slot]).wait()
    route = route_ref[0]
    lane = lax.broadcasted_iota(jnp.int32, route.shape, 1)
    w1 = jnp.sum(jnp.where(lane == ROUTE_W, route, 0.0), axis=-1, keepdims=True)
    w2 = jnp.sum(jnp.where(lane == ROUTE_W + 1, route, 0.0), axis=-1, keepdims=True)
    out = x_ref[0] + gt_ref[0] * (w1 * buf[slot, 0] + w2 * buf[slot, 1])
    if final_norm:
        ms = jnp.mean(out * out, axis=-1, keepdims=True)
        out = out * lax.rsqrt(ms + RMS_EPS) * gf_ref[...]
    o_ref[0] = out


def _combine_call(ys, dest, route, x, gt, tm, final_g=None):
    b, t, d = x.shape
    extra = [] if final_g is None else [final_g.reshape(1, d)]

    def rs(arr):
        n = arr.shape[-1]
        if arr.shape[1] == 1:
            return pl.BlockSpec((1, 1, n), lambda bb, i, dest: (bb, 0, 0))
        return pl.BlockSpec((1, tm, n), lambda bb, i, dest: (bb, i, 0))

    return pl.pallas_call(
        functools.partial(_combine_kernel, tm=tm, final_norm=final_g is not None),
        out_shape=jax.ShapeDtypeStruct((b, t, d), F32),
        grid_spec=pltpu.PrefetchScalarGridSpec(
            num_scalar_prefetch=1,
            grid=(b, t // tm),
            in_specs=[pl.BlockSpec(memory_space=pl.ANY), rs(route), rs(x), rs(gt)]
                     + [pl.BlockSpec((1, d), lambda bb, i, dest: (0, 0)) for _ in extra],
            out_specs=pl.BlockSpec((1, tm, d), lambda bb, i, dest: (bb, i, 0)),
            scratch_shapes=[pltpu.VMEM((2, 2, tm, d), F32), pltpu.SemaphoreType.DMA((2,))],
        ),
        compiler_params=_cparams(("arbitrary", "arbitrary")),
        name="moe_combine",
    )(dest, ys, route, x, gt, *extra)


def _moe_call(groups, g, wr, br, wgu, wd, layer, tile, final_g=None):
    d = D_MODEL
    routed = [_router_call(x, g, sh, sc, wr, br, tm) for x, sh, sc, gt, tm in groups]
    dests, te, active, tend, n_tiles = _route_plan([r[2] for r in routed], [r[3] for r in routed], tile, layer)
    xs = _dispatch_call([r[0].reshape(-1, d) for r in routed], dests, tend, n_tiles * tile,
                        [grp[4] for grp in groups], tile)
    ys = _expert_call(xs, te, active, wgu, wd, tile)
    return [_combine_call(ys, dest, route, x, gt, tm, final_g)
            for (x, sh, sc, gt, tm), (_, route, _, _), dest in zip(groups, routed, dests)]


def _t5_bucket(dist):
    dist = np.asarray(dist)
    max_exact = NUM_BUCKETS // 2
    log_ratio = np.log(np.maximum(dist, 1) / max_exact) / np.log(MAX_DISTANCE / max_exact)
    large = np.minimum(max_exact + (log_ratio * (NUM_BUCKETS - max_exact)).astype(np.int32), NUM_BUCKETS - 1)
    return np.where(dist < max_exact, dist, large).astype(np.int32)


def _attn_prompt_kernel(q_ref, kp_ref, kc_ref, vp_ref, vc_ref, bias_ref, o_ref, lse_ref):
    i = pl.program_id(2)
    q = q_ref[0].astype(BF16)
    kcat = jnp.concatenate([kp_ref[0], kc_ref[0]], axis=0).astype(BF16)
    vcat = jnp.concatenate([vp_ref[0], vc_ref[0]], axis=0).astype(BF16)
    head0 = lax.broadcasted_iota(jnp.int32, (QB, LANES), 1) < HEAD
    lane = lax.broadcasted_iota(jnp.int32, (QB, LANES), 1)
    zero = jnp.zeros((), BF16)
    first = jnp.where(i == 0, 1, 0)
    heads = range(N_HEADS)

    def masked_q(h):
        qp = q[:, LANES * (h // 2):LANES * (h // 2 + 1)]
        return jnp.where(head0, qp, zero) if h % 2 == 0 else jnp.where(head0, zero, qp)

    logits = [_dot_nt(masked_q(h), kcat[:, LANES * (h // 2):LANES * (h // 2 + 1)]) + bias_ref[first, h] for h in heads]
    m = [jnp.max(logits[h], axis=-1, keepdims=True) for h in heads]
    pr = [jnp.exp(logits[h] - m[h]) for h in heads]
    ssum = [jnp.sum(pr[h], axis=-1, keepdims=True) for h in heads]
    o = [_dot(pr[h], vcat[:, LANES * (h // 2):LANES * (h // 2 + 1)]) * (1.0 / ssum[h]) for h in heads]
    lse_tile = jnp.zeros((QB, LANES), F32)
    for h in heads:
        lse_tile = jnp.where(lane == h, m[h] + jnp.log(ssum[h]), lse_tile)
    o_ref[0] = jnp.concatenate([jnp.where(head0, o[2 * p], o[2 * p + 1]) for p in range(N_PAIRS)], axis=1)
    lse_ref[0] = lse_tile


def _attn_prompt_call(q, k, v, bias, dil):
    b, L, _ = k.shape
    nblk = L // QB
    d = D_MODEL
    cur = pl.BlockSpec((1, QB, d), lambda bb, r, i: (bb, i, r))
    prev = pl.BlockSpec((1, QB, d), lambda bb, r, i: (bb, jnp.maximum(i - 1, 0), r))
    return pl.pallas_call(
        _attn_prompt_kernel,
        out_shape=[jax.ShapeDtypeStruct((b, L, dil * d), F32), jax.ShapeDtypeStruct((b, L, dil * LANES), F32)],
        grid=(b, dil, nblk),
        in_specs=[cur, prev, cur, prev, cur, pl.BlockSpec(bias.shape, lambda bb, r, i: (0, 0, 0, 0))],
        out_specs=[cur, pl.BlockSpec((1, QB, LANES), lambda bb, r, i: (bb, i, r))],
        compiler_params=_cparams(("parallel", "parallel", "parallel")),
        name=f"attn_prompt_d{dil}",
    )(q, k, k, v, v, bias)


def _attn_merge_kernel(o0_ref, o1_ref, o2_ref, l0_ref, l1_ref, l2_ref, e_ref, wo_ref, x_ref, gt_ref, out_ref,
                       so1, so2, sl1, sl2):
    o1 = _load_dilated(o1_ref, so1, DILATIONS[1])
    o2 = _load_dilated(o2_ref, so2, DILATIONS[2])
    l0 = l0_ref[0]
    l1 = _load_dilated(l1_ref, sl1, DILATIONS[1])
    l2 = _load_dilated(l2_ref, sl2, DILATIONS[2])
    mx = jnp.maximum(jnp.maximum(l0, l1), l2)
    e0, e1, e2 = jnp.exp(l0 - mx), jnp.exp(l1 - mx), jnp.exp(l2 - mx)
    inv = 1.0 / (e0 + e1 + e2)
    em = e_ref[...]
    o = _dot3(e0 * inv, em) * o0_ref[0] + _dot3(e1 * inv, em) * o1 + _dot3(e2 * inv, em) * o2
    out_ref[0] = x_ref[0] + gt_ref[0] * _dot(o, wo_ref[...])


def _head_expand_matrix():
    return jnp.asarray(np.arange(LANES)[:, None] == (np.arange(D_MODEL) // HEAD)[None, :], BF16)


def _attn_merge_call(os_, ls_, wo, x, gt, tm):
    b, t, d = x.shape
    em = _head_expand_matrix()
    o_specs = [_dilated_spec(tm, d, dil) for dil in DILATIONS]
    l_specs = [_dilated_spec(tm, LANES, dil) for dil in DILATIONS]
    o_tile = pltpu.VMEM((N_PAIRS, tm, LANES), F32)
    l_tile = pltpu.VMEM((1, tm, LANES), F32)
    return pl.pallas_call(
        _attn_merge_kernel,
        out_shape=jax.ShapeDtypeStruct((b, t, d), F32),
        grid=(b, t // tm),
        in_specs=o_specs + l_specs + [_const_spec(em), _const_spec(wo), _row_spec(x, tm), _row_spec(gt, tm)],
        out_specs=pl.BlockSpec((1, tm, d), lambda bb, i: (bb, i, 0)),
        scratch_shapes=[o_tile, o_tile, l_tile, l_tile],
        compiler_params=_cparams(("parallel", "parallel")),
        name="attn_merge_out",
    )(*os_, *ls_, em, wo, x, gt)


SAMPLE_HEADS_PER_STEP = 8


def _attn_sample_kernel(q_ref, kn_ref, vn_ref, k_ref, v_ref, bias_ref, b0_ref, o_ref):
    n_buf = k_ref.shape[-1]
    hb = SAMPLE_HEADS_PER_STEP
    ng = len(DILATIONS)
    lane = lax.broadcasted_iota(jnp.int32, (HEAD, LANES), 1)
    los = [n_buf - BAND * dil for dil in DILATIONS]
    chains = [(hh, g) for hh in range(hb) for g in range(ng)]
    q = {(hh, g): q_ref[0, g, 0][:, hh:hh + 1] * SCALE_B for hh, g in chains}
    kn = [kn_ref[0, 0][:, hh:hh + 1] for hh in range(hb)]
    vn = [vn_ref[0, 0][:, hh:hh + 1] for hh in range(hb)]
    lk = {(hh, g): jnp.sum(k_ref[0, hh, :, los[g]:] * q[hh, g], axis=0, keepdims=True)
          + bias_ref[g, hh:hh + 1, los[g]:] for hh, g in chains}
    l0 = {(hh, g): jnp.sum(kn[hh] * q[hh, g], axis=0, keepdims=True) + b0_ref[g, 0][0:1, hh:hh + 1]
          for hh, g in chains}
    m = {c: jnp.maximum(jnp.max(lk[c], axis=1, keepdims=True), l0[c]) for c in chains}
    pk = {c: jnp.exp(lk[c] - m[c]) for c in chains}
    p0 = {c: jnp.exp(l0[c] - m[c]) for c in chains}
    ssum = {c: jnp.sum(pk[c], axis=1, keepdims=True) + p0[c] for c in chains}
    num = {(hh, g): jnp.sum(v_ref[0, hh, :, los[g]:] * pk[hh, g], axis=1, keepdims=True) + p0[hh, g] * vn[hh]
           for hh, g in chains}
    o_tile = jnp.zeros((HEAD, LANES), F32)
    for hh in range(hb):
        mx = jnp.maximum(jnp.maximum(m[hh, 0], m[hh, 1]), m[hh, 2])
        cs = [jnp.exp(m[hh, g] - mx) for g in range(ng)]
        inv = 1.0 / (cs[0] * ssum[hh, 0] + cs[1] * ssum[hh, 1] + cs[2] * ssum[hh, 2])
        o_col = (cs[0] * inv) * num[hh, 0] + (cs[1] * inv) * num[hh, 1] + (cs[2] * inv) * num[hh, 2]
        o_tile = jnp.where(lane == hh, o_col, o_tile)
    o_ref[0, 0] = o_tile


def _attn_sample_call(q, kn, vn, cache_k, cache_v, bias_p, bias_0):
    nb, n_buf = cache_k.shape[:2]
    assert n_buf == BAND * max(DILATIONS)
    hb = SAMPLE_HEADS_PER_STEP
    nhb = N_HEADS // hb
    ng = len(DILATIONS)
    kt = jnp.transpose(cache_k, (0, 2, 3, 1))
    vt = jnp.transpose(cache_v, (0, 2, 3, 1))
    qt = jnp.transpose(q.reshape(nb, ng, nhb, hb, HEAD), (0, 1, 2, 4, 3))
    knt = jnp.transpose(kn.reshape(nb, nhb, hb, HEAD), (0, 1, 3, 2))
    vnt = jnp.transpose(vn.reshape(nb, nhb, hb, HEAD), (0, 1, 3, 2))
    o = pl.pallas_call(
        _attn_sample_kernel,
        out_shape=jax.ShapeDtypeStruct((nb, nhb, HEAD, LANES), F32),
        grid=(nb, nhb),
        in_specs=[pl.BlockSpec((1, ng, 1, HEAD, hb), lambda b, j: (b, 0, j, 0, 0)),
                  pl.BlockSpec((1, 1, HEAD, hb), lambda b, j: (b, j, 0, 0)),
                  pl.BlockSpec((1, 1, HEAD, hb), lambda b, j: (b, j, 0, 0)),
                  pl.BlockSpec((1, hb, HEAD, n_buf), lambda b, j: (b, j, 0, 0)),
                  pl.BlockSpec((1, hb, HEAD, n_buf), lambda b, j: (b, j, 0, 0)),
                  pl.BlockSpec((ng, hb, n_buf), lambda b, j: (0, j, 0)),
                  pl.BlockSpec((ng, 1, 8, hb), lambda b, j: (0, j, 0, 0))],
        out_specs=pl.BlockSpec((1, 1, HEAD, LANES), lambda b, j: (b, j, 0, 0)),
        compiler_params=_cparams(("parallel", "parallel")),
        name="attn_sample",
    )(qt, knt, vnt, kt, vt, bias_p, bias_0)
    return jnp.transpose(o[..., :hb], (0, 1, 3, 2)).reshape(nb, N_HEADS * HEAD)


def _prompt_bias(rel_bias, gi, dil):
    table = rel_bias[:, gi * N_HEADS:(gi + 1) * N_HEADS].astype(F32)
    per_dist = table[_t5_bucket(np.arange(BAND + 1) * dil)]
    delta = (lax.broadcasted_iota(jnp.int32, (QB, 2 * QB), 0) + QB
             - lax.broadcasted_iota(jnp.int32, (QB, 2 * QB), 1))
    onehot = (delta[:, :, None] == jnp.arange(BAND + 1, dtype=jnp.int32)[None, None, :]).astype(F32)
    bias = jnp.einsum("qkj,jh->hqk", onehot, per_dist, precision=lax.Precision.HIGHEST)
    in_band = (delta >= 0) & (delta <= BAND)
    bias = jnp.where(in_band[None], bias, NEG_INF)
    has_prev = lax.broadcasted_iota(jnp.int32, (QB, 2 * QB), 1) >= QB
    return jnp.stack([bias, jnp.where(has_prev[None], bias, NEG_INF)])


def _sample_bias(rel_bias, n_buf):
    hb = SAMPLE_HEADS_PER_STEP
    dist = n_buf - np.arange(n_buf)
    bps, b0s = [], []
    for gi, dil in enumerate(DILATIONS):
        table = rel_bias[:, gi * N_HEADS:(gi + 1) * N_HEADS].astype(F32)
        in_window = (dist % dil == 0) & (dist <= BAND * dil)
        bps.append(jnp.where(in_window[None, :], jnp.transpose(table[_t5_bucket(dist)]), NEG_INF))
        b0s.append(jnp.broadcast_to(table[0].reshape(N_HEADS // hb, 1, hb), (N_HEADS // hb, 8, hb)))
    return jnp.stack(bps), jnp.stack(b0s)


def _rwkv_mixer(x, mods, wkv0, shift0, p, tm):
    b, t, d = x.shape
    sh_m, sc_m, gt_m = mods[:3]
    r, lw, k, v, a, bb, g, bonus = _rwkv_proj_call(x, shift0, p["g_norm"][0, 0], sh_m, sc_m, p["rw_mu"], p["rw_vecs"],
                                                   p["rw_w_rkv"], p["rw_w1"], p["rw_w2"], p["rw_a1"], p["rw_a2"],
                                                   p["rw_g1"], p["rw_g2"], tm)
    if wkv0 is None:
        y, wkv = _rwkv_scan_call(r, lw, k, v, a, bb)
        shift = _norm_mod_call(x[:, t - PREV_ROWS:], p["g_norm"][0, 0], sh_m, sc_m, PREV_ROWS)[:, -1]
    else:
        y, wkv = _rwkv_step_call(wkv0, r, lw, k, v, a, bb)
        shift = _norm_mod_call(x, p["g_norm"][0, 0], sh_m, sc_m, tm)[0]
    return _rwkv_out_call(y, bonus, g, x, gt_m, p["rw_ln"], p["rw_w_o"], tm), wkv, shift


def _attn_mixer(x, mods, mods_kv, caches, p, tm):
    b, t, d = x.shape
    sh_k, sc_k = mods_kv
    sh_m, sc_m, gt_m = mods[:3]
    if caches is None:
        k_sh, v_sh, k1, v1, k2, v2 = _kvproj_call(x, p["w_kv"], p["g_kv"], sh_k, sc_k, tm)
        qs = _qproj_call(x, p["at_w_q"], p["g_norm"][1, 0], sh_m, sc_m, tm)
        ks, vs = (k_sh, k1, k2), (v_sh, v1, v2)
        os_, ls_ = [], []
        for gi, dil in enumerate(DILATIONS):
            o, lse = _attn_prompt_call(qs[gi], ks[gi], vs[gi], _prompt_bias(p["rel_bias"], gi, dil), dil)
            os_.append(o)
            ls_.append(lse)
        x = _attn_merge_call(os_, ls_, p["at_w_o"], x, gt_m, tm)
    else:
        kvp = _linear_call(x, p["w_kv"], tm, norm=(p["g_kv"], sh_k, sc_k), name="kv_proj")
        k_sh, v_sh = kvp[..., :d], kvp[..., d:]
        q = _linear_call(x, p["at_w_q"], tm, norm=(p["g_norm"][1, 0], sh_m, sc_m), name="q_proj")
        cache_k, cache_v = caches
        nb, n_buf = cache_k.shape[:2]
        bias_p, bias_0 = _sample_bias(p["rel_bias"], n_buf)
        o = _attn_sample_call(q.reshape(nb, len(DILATIONS), N_HEADS, HEAD), k_sh.reshape(nb, N_HEADS, HEAD),
                              v_sh.reshape(nb, N_HEADS, HEAD), cache_k, cache_v, bias_p, bias_0).reshape(b, t, d)
        x = _linear_call(o, p["at_w_o"], tm, resid=(x, gt_m), name="attn_out")
    return x, k_sh, v_sh


def _prepare_params(w_ada, b_ada, g_norm, rw_mu, rw_w_rkv, rw_w0, rw_w1, rw_w2, rw_a0, rw_a1, rw_a2, rw_g1, rw_g2,
                    rw_k_k, rw_k_a, rw_r_k, rw_ln_w, rw_ln_b, rw_w_o, moe_w_rg, moe_b_rg, moe_w_re, moe_b_re,
                    moe_w_gu, moe_w_down, w_ada_kv, b_ada_kv, g_kv, w_kv, at_w_q, at_w_o, rel_bias, g_final):
    d = D_MODEL
    zeros = jnp.zeros((3, d), F32)
    pad = LANES - N_GROUPS_E - N_EXPERTS
    depth = moe_w_rg.shape[0]
    return dict(
        g_norm=g_norm, g_kv=g_kv, g_final=g_final, rel_bias=rel_bias,
        rw_mu=rw_mu[0],
        rw_vecs=jnp.concatenate([rw_w0[0][None], rw_a0[0][None], rw_k_k[0][None], rw_k_a[0][None],
                                 rw_r_k[0].reshape(1, d), zeros], axis=0),
        rw_w_rkv=rw_w_rkv[0].astype(BF16), rw_w1=rw_w1[0], rw_w2=rw_w2[0], rw_a1=rw_a1[0], rw_a2=rw_a2[0],
        rw_g1=rw_g1[0], rw_g2=rw_g2[0],
        rw_ln=jnp.concatenate([rw_ln_w[0][None], rw_ln_b[0][None], zeros, zeros], axis=0),
        rw_w_o=rw_w_o[0].astype(BF16),
        moe_wr=jnp.pad(jnp.concatenate([moe_w_rg, moe_w_re], axis=-1), ((0, 0), (0, 0), (0, pad))),
        moe_br=jnp.pad(jnp.concatenate([moe_b_rg, moe_b_re], axis=-1), ((0, 0), (0, pad)))[:, None, :],
        moe_w_gu=moe_w_gu.reshape(depth * N_EXPERTS, d, 2 * D_EXPERT),
        moe_w_down=moe_w_down.reshape(depth * N_EXPERTS, D_EXPERT, d),
        w_kv=w_kv.astype(BF16), at_w_q=at_w_q[0].astype(BF16), at_w_o=at_w_o[0].astype(BF16),
    )


def kernel(x_prompt, x_sample, state_wkv, state_shift, cache_k, cache_v, c_prompt, c_sample, w_ada, b_ada, g_norm, rw_mu, rw_w_rkv, rw_w0, rw_w1, rw_w2, rw_a0, rw_a1, rw_a2, rw_g1, rw_g2, rw_k_k, rw_k_a, rw_r_k, rw_ln_w, rw_ln_b, rw_w_o, moe_w_rg, moe_b_rg, moe_w_re, moe_b_re, moe_w_gu, moe_w_down, w_ada_kv, b_ada_kv, g_kv, w_kv, at_w_q, at_w_o, rel_bias, g_final):
    d = D_MODEL
    bp, tp = x_prompt.shape[:2]
    nb = x_sample.shape[0]
    p = _prepare_params(w_ada, b_ada, g_norm, rw_mu, rw_w_rkv, rw_w0, rw_w1, rw_w2, rw_a0, rw_a1, rw_a2, rw_g1,
                        rw_g2, rw_k_k, rw_k_a, rw_r_k, rw_ln_w, rw_ln_b, rw_w_o, moe_w_rg, moe_b_rg, moe_w_re,
                        moe_b_re, moe_w_gu, moe_w_down, w_ada_kv, b_ada_kv, g_kv, w_kv, at_w_q, at_w_o, rel_bias,
                        g_final)

    n_c = bp + nb
    n_pad = -n_c % 8
    c_all = jnp.concatenate([c_prompt, c_sample, jnp.zeros((n_pad, d), F32)], axis=0)
    mod_l = [_ada_linear(c_all, w_ada, b_ada, l) for l in range(2)]
    mod_kv = _ada_linear(c_all, w_ada_kv[None], b_ada_kv[None], 0)

    def split(m, n, lo, hi, per_batch):
        parts = [m[lo:hi, d * j:d * (j + 1)] for j in range(n)]
        return [q[:, None, :] if per_batch else q[None] for q in parts]

    mods_p = dict(l0=split(mod_l[0], 6, 0, bp, True), l1=split(mod_l[1], 6, 0, bp, True),
                  kv=split(mod_kv, 2, 0, bp, True))
    mods_s = dict(l0=split(mod_l[0], 6, bp, n_c, False), l1=split(mod_l[1], 6, bp, n_c, False),
                  kv=split(mod_kv, 2, bp, n_c, False))

    tm_p, tm_s, moe_tile = 256, nb, 512
    xs = x_sample.reshape(1, nb, d)

    def moe(x_p, x_s, layer, final_g=None):
        groups = [(x, *mods["l%d" % layer][3:], tm) for x, mods, tm in ((x_p, mods_p, tm_p), (x_s, mods_s, tm_s))]
        return _moe_call(groups, p["g_norm"][layer, 1], p["moe_wr"][layer], p["moe_br"][layer],
                         p["moe_w_gu"], p["moe_w_down"], layer, moe_tile, final_g)

    x_p, wkv_p, shift_p = _rwkv_mixer(x_prompt, mods_p["l0"], None, None, p, tm_p)
    x_s, wkv_s, shift_s = _rwkv_mixer(xs, mods_s["l0"], state_wkv[0], state_shift[0][None], p, tm_s)
    x_p, x_s = moe(x_p, x_s, 0)
    x_p, k_p, v_p = _attn_mixer(x_p, mods_p["l1"], mods_p["kv"], None, p, tm_p)
    x_s, k_s, v_s = _attn_mixer(x_s, mods_s["l1"], mods_s["kv"], (cache_k, cache_v), p, tm_s)
    y_p, y_s = moe(x_p, x_s, 1, final_g=p["g_final"])

    keep = min(BAND * max(DILATIONS), tp)
    k_prompt = k_p[:, tp - keep:].reshape(bp, keep, N_HEADS, HEAD)
    v_prompt = v_p[:, tp - keep:].reshape(bp, keep, N_HEADS, HEAD)
    return (y_p, y_s.reshape(nb, 1, d), wkv_p[None], shift_p[None], k_prompt, v_prompt,
            wkv_s[None], shift_s[None], k_s.reshape(nb, 1, N_HEADS, HEAD), v_s.reshape(nb, 1, N_HEADS, HEAD))
```
